```python
import math
import jax, jax.numpy as jnp
from jax import lax
import numpy as np

D_MODEL = 2048
BATCH = 1
SEQ = 16384
DEPTH = 4

GRID_W = 64
CTX_LEN = 256
Q_BLOCK = 128
ROPE_THETA = 10000.0
EPS = 1e-6

N_BRANCH = 3
BRANCH_W = D_MODEL // 2
RET_HEADS = 8
RET_KEY_DIM = BRANCH_W // (2 * RET_HEADS)
RET_VAL_DIM = BRANCH_W // RET_HEADS
RET_CHUNK = 128
DIFF_HEADS = 8
DIFF_HEAD_DIM = BRANCH_W // (2 * DIFF_HEADS)
DIFF_VAL_DIM = 2 * DIFF_HEAD_DIM
GQA_HEAD_DIM = 128
GQA_Q_HEADS = BRANCH_W // GQA_HEAD_DIM
GQA_KV_HEADS = 2
GQA_GROUP = GQA_Q_HEADS // GQA_KV_HEADS

RET_QK_W = RET_HEADS * RET_KEY_DIM
RET_V_W = RET_HEADS * RET_VAL_DIM
DIFF_QK_W = DIFF_HEADS * 2 * DIFF_HEAD_DIM
DIFF_V_W = DIFF_HEADS * DIFF_VAL_DIM
GQA_Q_W = GQA_Q_HEADS * GQA_HEAD_DIM
GQA_KV_W = GQA_KV_HEADS * GQA_HEAD_DIM
IN_SIZES = (RET_QK_W, RET_QK_W, RET_V_W, BRANCH_W,
            DIFF_QK_W, DIFF_QK_W, DIFF_V_W, BRANCH_W,
            GQA_Q_W, GQA_KV_W, GQA_KV_W, BRANCH_W,
            N_BRANCH * D_MODEL)
IN_COLS = sum(IN_SIZES)

kernel_name = "hybrid_parallel_retention_diffattn_gqa_dit"

F32 = jnp.float32


def rms_norm(x, gain=None):
    xf = x.astype(F32)
    y = xf * lax.rsqrt(jnp.mean(xf * xf, axis=-1, keepdims=True) + EPS)
    if gain is not None:
        y = y * gain.astype(F32)
    return y.astype(x.dtype)


def split_proj(z):
    parts, start = [], 0
    for size in IN_SIZES:
        parts.append(z[..., start:start + size])
        start += size
    return parts


def heads(z, n):
    b, t, _ = z.shape
    return z.reshape(b, t, n, -1).transpose(0, 2, 1, 3)


def rope_rotate(x, cos, sin):
    half = x.shape[-1] // 2
    x1, x2 = x[..., :half], x[..., half:]
    cos = cos.astype(x.dtype)
    sin = sin.astype(x.dtype)
    return jnp.concatenate([x1 * cos - x2 * sin, x1 * sin + x2 * cos], axis=-1)


def axial_rope_tables(n_tokens, head_dim):
    n_rows = n_tokens // GRID_W
    rows = jnp.repeat(jnp.arange(n_rows), GRID_W).astype(F32)
    cols = jnp.tile(jnp.arange(GRID_W), n_rows).astype(F32)
    n_freq = head_dim // 4
    freqs = ROPE_THETA ** (-jnp.arange(n_freq, dtype=F32) / n_freq)
    ang = jnp.concatenate([rows[:, None] * freqs, cols[:, None] * freqs], axis=-1)
    return jnp.cos(ang), jnp.sin(ang)


def seq_rope_tables(n_tokens, head_dim):
    freqs = 1.0 / (ROPE_THETA ** jnp.linspace(0.0, 1.0, head_dim // 2, dtype=F32))
    ang = jnp.arange(n_tokens, dtype=F32)[:, None] * freqs
    return jnp.cos(ang), jnp.sin(ang)


def ret_states(k, v, log_g, s0):
    b, h, t, dk = k.shape
    n = t // RET_CHUNK
    kc = k.reshape(b, h, n, RET_CHUNK, dk)
    vc = v.reshape(b, h, n, RET_CHUNK, -1)
    j = jnp.arange(RET_CHUNK)
    w = jnp.exp(log_g[:, None] * (RET_CHUNK - 1 - j))
    kv = jnp.einsum('bhncd,bhnce,hc->nbhde', kc, vc, w)
    g_chunk = jnp.exp(log_g * RET_CHUNK)[None, :, None, None]

    def step(s, kv_n):
        return g_chunk * s + kv_n, s

    s_final, s_prev = lax.scan(step, s0, kv)
    return s_prev, s_final


def ret_output(q, k, v, log_g, s_prev, strict):
    b, h, t, dk = q.shape
    n = t // RET_CHUNK
    qc = q.reshape(b, h, n, RET_CHUNK, dk)
    kc = k.reshape(b, h, n, RET_CHUNK, dk)
    vc = v.reshape(b, h, n, RET_CHUNK, -1)
    i = jnp.arange(RET_CHUNK)
    rel = i[:, None] - i[None, :]
    mask = (rel > 0) if strict else (rel >= 0)
    decay = jnp.where(mask[None], jnp.exp(log_g[:, None, None] * jnp.where(mask, rel, 0)[None]), 0.0)
    att = jnp.einsum('bhnid,bhnjd->bhnij', qc, kc) * decay[:, None]
    o = jnp.einsum('bhnij,bhnje->bhnie', att, vc)
    cross = jnp.exp(log_g[:, None] * (i + 1))
    o = o + jnp.einsum('bhnid,nbhde->bhnie', qc * cross[:, None, :, None], s_prev)
    return o.reshape(b, h, t, -1)


def retention_finish(o, g):
    b, h, t, dv = o.shape
    o = rms_norm(o).astype(g.dtype).transpose(0, 2, 1, 3).reshape(b, t, h * dv)
    return o * jax.nn.silu(g)


def retention_branch(zl, zc, log_rate, with_ctx_out):
    ql, kl, vl, gl = zl
    qc, kc, vc, gc = zc
    b, t, _ = ql.shape
    k_scale = RET_KEY_DIM ** -0.5
    cos, sin = seq_rope_tables(t, RET_KEY_DIM)
    ql = rope_rotate(heads(ql, RET_HEADS), cos, sin)
    kl = rope_rotate(heads(kl, RET_HEADS), cos, sin) * k_scale
    vl = heads(vl, RET_HEADS)
    qc = heads(qc, RET_HEADS)
    kc = heads(kc, RET_HEADS) * k_scale
    vc = heads(vc, RET_HEADS)
    log_g = -jnp.exp(log_rate.astype(F32))
    s0 = jnp.zeros((b, RET_HEADS, RET_KEY_DIM, RET_VAL_DIM), F32)
    flip = lambda a: jnp.flip(a, axis=2)
    pc_f, sc_f = ret_states(kc, vc, log_g[0], s0)
    pc_b, sc_b = ret_states(flip(kc), flip(vc), log_g[1], s0)
    pl_f, _ = ret_states(kl, vl, log_g[0], sc_f)
    pl_b, _ = ret_states(flip(kl), flip(vl), log_g[1], sc_b)
    ol = (ret_output(ql, kl, vl, log_g[0], pl_f, False)
          + flip(ret_output(flip(ql), flip(kl), flip(vl), log_g[1], pl_b, True)))
    out_l = retention_finish(ol, gl)
    out_c = None
    if with_ctx_out:
        oc = (ret_output(qc, kc, vc, log_g[0], pc_f, False)
              + flip(ret_output(flip(qc), flip(kc), flip(vc), log_g[1], pc_b, True)))
        out_c = retention_finish(oc, gc)
    return out_l, out_c


def diff_attend(q, k, v, lam):
    b, h, _, t, d = q.shape
    nb = t // Q_BLOCK
    qb = jnp.moveaxis(q.reshape(b, h, 2, nb, Q_BLOCK, d), 3, 0)
    scale = d ** -0.5

    def block(qi):
        s = jnp.einsum('bhcqd,bhcsd->bhcqs', qi, k).astype(F32) * scale
        p = jax.nn.softmax(s, axis=-1)
        pd = p[:, :, 0] - lam * p[:, :, 1]
        return jnp.einsum('bhqs,bhse->bhqe', pd.astype(v.dtype), v)

    o = lax.map(block, qb)
    return jnp.moveaxis(o, 0, 2).reshape(b, h, t, -1)


def diff_finish(o, g, subln_gain, lambda_init):
    b, h, t, dv = o.shape
    o = rms_norm(o, subln_gain) * (1.0 - lambda_init)
    o = o.transpose(0, 2, 1, 3).reshape(b, t, h * dv)
    return o * jax.nn.silu(g)


def diff_branch(zl, zc, lam_params, subln_gain, lambda_init, with_ctx_out):
    ql, kl, vl, gl = zl
    qc, kc, vc, gc = zc

    def qk_heads(z):
        bb, tt, _ = z.shape
        return z.reshape(bb, tt, DIFF_HEADS, 2, DIFF_HEAD_DIM).transpose(0, 2, 3, 1, 4)

    t = ql.shape[1]
    cos, sin = axial_rope_tables(t, DIFF_HEAD_DIM)
    ql = rope_rotate(qk_heads(ql), cos, sin)
    kl = rope_rotate(qk_heads(kl), cos, sin)
    vl = heads(vl, DIFF_HEADS)
    qc, kc, vc = qk_heads(qc), qk_heads(kc), heads(vc, DIFF_HEADS)
    lp = lam_params.astype(F32)
    lam = jnp.exp(jnp.sum(lp[0] * lp[1])) - jnp.exp(jnp.sum(lp[2] * lp[3])) + lambda_init
    k_all = jnp.concatenate([kl, kc], axis=3)
    v_all = jnp.concatenate([vl, vc], axis=2)
    out_l = diff_finish(diff_attend(ql, k_all, v_all, lam), gl, subln_gain, lambda_init)
    out_c = None
    if with_ctx_out:
        out_c = diff_finish(diff_attend(qc, kc, vc, lam), gc, subln_gain, lambda_init)
    return out_l, out_c


def gqa_attend(q, k, v):
    b, kh, g, t, d = q.shape
    nb = t // Q_BLOCK
    qb = jnp.moveaxis(q.reshape(b, kh, g, nb, Q_BLOCK, d), 3, 0)
    scale = d ** -0.5

    def block(qi):
        s = jnp.einsum('bkgqd,bksd->bkgqs', qi, k).astype(F32) * scale
        p = jax.nn.softmax(s, axis=-1).astype(v.dtype)
        return jnp.einsum('bkgqs,bkse->bkgqe', p, v)

    o = lax.map(block, qb)
    return jnp.moveaxis(o, 0, 3).reshape(b, kh, g, t, -1)


def gqa_finish(o, g):
    b, kh, gg, t, d = o.shape
    o = o.transpose(0, 3, 1, 2, 4).reshape(b, t, kh * gg * d)
    return o * jax.nn.silu(g)


def gqa_branch(zl, zc, q_gain, k_gain, with_ctx_out):
    ql, kl, vl, gl = zl
    qc, kc, vc, gc = zc

    def q_heads(z):
        bb, tt, _ = z.shape
        return z.reshape(bb, tt, GQA_KV_HEADS, GQA_GROUP, GQA_HEAD_DIM).transpose(0, 2, 3, 1, 4)

    t = ql.shape[1]
    cos, sin = axial_rope_tables(t, GQA_HEAD_DIM)
    ql = rope_rotate(rms_norm(q_heads(ql), q_gain), cos, sin)
    kl = rope_rotate(rms_norm(heads(kl, GQA_KV_HEADS), k_gain), cos, sin)
    vl = heads(vl, GQA_KV_HEADS)
    qc = rms_norm(q_heads(qc), q_gain)
    kc = rms_norm(heads(kc, GQA_KV_HEADS), k_gain)
    vc = heads(vc, GQA_KV_HEADS)
    k_all = jnp.concatenate([kl, kc], axis=2)
    v_all = jnp.concatenate([vl, vc], axis=2)
    out_l = gqa_finish(gqa_attend(ql, k_all, v_all), gl)
    out_c = None
    if with_ctx_out:
        out_c = gqa_finish(gqa_attend(qc, kc, vc), gc)
    return out_l, out_c


def merge_branches(branches, z_gate, w_branch, w_out):
    br = jnp.stack(branches, axis=2)
    y = jnp.einsum('btnw,nwd->btnd', br, w_branch)
    gates = jax.nn.sigmoid(z_gate.reshape(z_gate.shape[:-1] + (N_BRANCH, D_MODEL)))
    return jnp.einsum('btnd,btnd->btd', gates, y) @ w_out


def mixer(h_lat, h_ctx, w_in, ret_log_rate, diff_lambda, diff_subln_gain, gqa_q_gain,
          gqa_k_gain, w_branch, w_out, lambda_init, with_ctx_out):
    zl = split_proj(h_lat @ w_in)
    zc = split_proj(h_ctx @ w_in)
    ra, rc = retention_branch(zl[0:4], zc[0:4], ret_log_rate, with_ctx_out)
    da, dc = diff_branch(zl[4:8], zc[4:8], diff_lambda, diff_subln_gain, lambda_init, with_ctx_out)
    ga, gc = gqa_branch(zl[8:12], zc[8:12], gqa_q_gain, gqa_k_gain, with_ctx_out)
    out_l = merge_branches([ra, da, ga], zl[12], w_branch, w_out)
    out_c = merge_branches([rc, dc, gc], zc[12], w_branch, w_out) if with_ctx_out else None
    return out_l, out_c


def setup_inputs(seed: int = 0) -> dict:
    key = jax.random.key(seed)
    ks = jax.random.split(key, 16)
    nrm = jax.random.normal
    ret_base = jnp.log(-jnp.log1p(-(2.0 ** (-5.0 - jnp.arange(RET_HEADS, dtype=F32)))))
    return {
        "x": nrm(ks[0], (BATCH, SEQ, D_MODEL), F32),
        "c": nrm(ks[1], (BATCH, D_MODEL), F32),
        "ctx": nrm(ks[2], (BATCH, CTX_LEN, D_MODEL), F32),
        "c_ctx": nrm(ks[3], (D_MODEL,), F32),
        "norm_gain": 1.0 + 0.02 * nrm(ks[4], (DEPTH, D_MODEL), F32),
        "w_ada": nrm(ks[5], (DEPTH, D_MODEL, 3 * D_MODEL), F32) * (0.5 * D_MODEL ** -0.5),
        "b_ada": 0.01 * nrm(ks[6], (DEPTH, 3 * D_MODEL), F32),
        "w_in": nrm(ks[7], (DEPTH, D_MODEL, IN_COLS), F32) * D_MODEL ** -0.5,
        "ret_log_rate": ret_base[None, None, :] + 0.05 * nrm(ks[8], (DEPTH, 2, RET_HEADS), F32),
        "diff_lambda": 0.1 * nrm(ks[9], (DEPTH, 4, DIFF_HEAD_DIM), F32),
        "diff_subln_gain": 1.0 + 0.02 * nrm(ks[10], (DEPTH, DIFF_VAL_DIM), F32),
        "gqa_q_gain": 1.0 + 0.02 * nrm(ks[11], (DEPTH, GQA_HEAD_DIM), F32),
        "gqa_k_gain": 1.0 + 0.02 * nrm(ks[12], (DEPTH, GQA_HEAD_DIM), F32),
        "w_branch": nrm(ks[13], (DEPTH, N_BRANCH, BRANCH_W, D_MODEL), F32) * BRANCH_W ** -0.5,
        "w_out": nrm(ks[14], (DEPTH, D_MODEL, D_MODEL), F32) * D_MODEL ** -0.5,
        "final_norm_gain": 1.0 + 0.02 * nrm(ks[15], (D_MODEL,), F32),
    }


def reference(x, c, ctx, c_ctx, norm_gain, w_ada, b_ada, w_in, ret_log_rate, diff_lambda,
              diff_subln_gain, gqa_q_gain, gqa_k_gain, w_branch, w_out, final_norm_gain):
    sc = jax.nn.silu(c)
    scc = jax.nn.silu(c_ctx)
    for l in range(DEPTH):
        last = l == DEPTH - 1
        shift, scale, gate = jnp.split(sc @ w_ada[l] + b_ada[l], 3, axis=-1)
        shift_c, scale_c, gate_c = jnp.split(scc @ w_ada[l] + b_ada[l], 3, axis=-1)
        h = rms_norm(x, norm_gain[l]) * (1.0 + scale[:, None]) + shift[:, None]
        hc = rms_norm(ctx, norm_gain[l]) * (1.0 + scale_c) + shift_c
        lambda_init = 0.8 - 0.6 * math.exp(-0.3 * l)
        out, out_c = mixer(h, hc, w_in[l], ret_log_rate[l], diff_lambda[l], diff_subln_gain[l],
                           gqa_q_gain[l], gqa_k_gain[l], w_branch[l], w_out[l], lambda_init,
                           not last)
        x = x + gate[:, None] * out
        if not last:
            ctx = ctx + gate_c * out_c
    return rms_norm(x, final_norm_gain)
```

```python
import functools
import math

import jax
import jax.numpy as jnp
from jax import lax
from jax.experimental import pallas as pl
from jax.experimental.pallas import tpu as pltpu

F32 = jnp.float32
BF16 = jnp.bfloat16

EPS = 1e-6
ROPE_THETA = 10000.0
GRID_W = 64
LANES = 128
VMEM_LIMIT = 56 * 1024 * 1024

HEADS = 8
HEAD_W = 128
KEY_W = 64
GQA_KV = 2
GQA_GROUP = 4
CHUNK = 128

Z_GATES, Z_RQ, Z_RK, Z_RV, Z_RG = 0, 6144, 6656, 7168, 8192
Z_DQ, Z_DK, Z_DV, Z_DG = 9216, 10240, 11264, 12288
Z_GQ, Z_GG, Z_GK, Z_GV = 13312, 14336, 15360, 15616
Z_COLS = 15872
W_TILE = 512
NORM_ROWS = 128


def _pick(n, candidates):
    for c in candidates:
        if n % c == 0:
            return c
    raise ValueError(f"no tile in {candidates} divides {n}")


def _params(*sem):
    return pltpu.CompilerParams(dimension_semantics=sem, vmem_limit_bytes=VMEM_LIMIT)


def _sigmoid(x):
    return 1.0 / (1.0 + jnp.exp(-x))


def _silu(x):
    return x * _sigmoid(x)


def _ada_kernel(c_ref, w_ref, b_ref, o_ref):
    s = _silu(c_ref[...])
    o_ref[...] = jnp.dot(s, w_ref[...], preferred_element_type=F32,
                         precision=lax.Precision.HIGHEST) + b_ref[...]


def _ada_all(c8, w_ada, b_ada):
    depth, d, d3 = w_ada.shape
    tn = 1024
    return pl.pallas_call(
        _ada_kernel,
        grid=(depth, d3 // tn),
        in_specs=[
            pl.BlockSpec((8, d), lambda l, j: (0, 0)),
            pl.BlockSpec((None, d, tn), lambda l, j: (l, 0, j)),
            pl.BlockSpec((None, 1, tn), lambda l, j: (l, 0, j)),
        ],
        out_specs=pl.BlockSpec((None, 8, tn), lambda l, j: (l, 0, j)),
        out_shape=jax.ShapeDtypeStruct((depth, 8, d3), F32),
        compiler_params=_params("arbitrary", "arbitrary"),
    )(c8, w_ada, b_ada.reshape(depth, 1, d3))


def _row_mod(mod_ref, lo, hi, is_ctx):
    return jnp.where(is_ctx, mod_ref[1:2, lo:hi], mod_ref[0:1, lo:hi])


def _inproj_kernel(x_ref, mod_ref, g_ref, w_ref, z_ref, h_ref, *, n_lat, tm, d):
    i = pl.program_id(0)

    @pl.when(pl.program_id(1) == 0)
    def _():
        def norm_rows(r, carry):
            sl = pl.ds(pl.multiple_of(r * NORM_ROWS, NORM_ROWS), NORM_ROWS)
            x = x_ref[sl, :]
            y = x * lax.rsqrt(jnp.mean(x * x, axis=-1, keepdims=True) + EPS) * g_ref[...]
            rows = i * tm + r * NORM_ROWS + lax.broadcasted_iota(jnp.int32, (NORM_ROWS, 1), 0)
            is_ctx = rows >= n_lat
            shift = _row_mod(mod_ref, 0, d, is_ctx)
            scale = _row_mod(mod_ref, d, 2 * d, is_ctx)
            h_ref[sl, :] = (y * (1.0 + scale) + shift).astype(BF16)
            return carry

        lax.fori_loop(0, tm // NORM_ROWS, norm_rows, 0)

    z_ref[...] = jnp.dot(h_ref[...], w_ref[...], preferred_element_type=F32).astype(BF16)


def _w_block(j):
    return jnp.where(j < 12, j + 19, jnp.where(j < 28, j - 12, jnp.where(j < 30, j - 11, 16)))


def _in_proj(xx, mod, gain, w, n_lat):
    n, d = xx.shape
    tm = _pick(n, (1280, 640, 256))
    kern = functools.partial(_inproj_kernel, n_lat=n_lat, tm=tm, d=d)
    return pl.pallas_call(
        kern,
        grid=(n // tm, Z_COLS // W_TILE),
        in_specs=[
            pl.BlockSpec((tm, d), lambda i, j: (i, 0)),
            pl.BlockSpec((8, 3 * d), lambda i, j: (0, 0)),
            pl.BlockSpec((1, d), lambda i, j: (0, 0)),
            pl.BlockSpec((d, W_TILE), lambda i, j: (0, _w_block(j))),
        ],
        out_specs=pl.BlockSpec((tm, W_TILE), lambda i, j: (i, j)),
        out_shape=jax.ShapeDtypeStruct((n, Z_COLS), BF16),
        scratch_shapes=[pltpu.VMEM((tm, d), BF16)],
        compiler_params=_params("arbitrary", "arbitrary"),
    )(xx, mod, gain.reshape(1, d), w)


def _swap_halves(x, half):
    if 2 * half == LANES:
        return pltpu.roll(x, half, 1)
    lane = lax.broadcasted_iota(jnp.int32, x.shape, 1)
    return jnp.where(lane % (2 * half) < half,
                     pltpu.roll(x, LANES - half, 1), pltpu.roll(x, half, 1))


def _rope(x, c, s, half):
    return x * c + _swap_halves(x, half) * s


def _prep_kernel(rq_ref, rk_ref, dq_ref, dk_ref, dv_ref, gq_ref, gk_ref, gv_ref,
                 sc_ref, ss_ref, ac_ref, as_ref, bc_ref, bs_ref, qg_ref, kg_ref,
                 orq_ref, ork_ref, odq_ref, odk_ref, odv_ref, ogq_ref, ogk_ref, ogv_ref):
    tp = rq_ref.shape[0]
    lane = lax.broadcasted_iota(jnp.int32, (tp, LANES), 1)
    low = lane < KEY_W
    sc, ss = sc_ref[...], ss_ref[...]
    ac, as_ = ac_ref[...], as_ref[...]
    bc, bs = bc_ref[...], bs_ref[...]
    ones = jnp.ones((tp, LANES), BF16)

    k_scale = KEY_W ** -0.5
    for p in range(HEADS // 2):
        sl = slice(p * LANES, (p + 1) * LANES)
        for src, dst, mul in ((rq_ref, orq_ref, 1.0), (rk_ref, ork_ref, k_scale)):
            y = _rope(src[:, sl].astype(F32), sc, ss, KEY_W // 2) * mul
            dst[:, (2 * p) * LANES:(2 * p + 1) * LANES] = jnp.where(low, y, 0.0).astype(BF16)
            dst[:, (2 * p + 1) * LANES:(2 * p + 2) * LANES] = jnp.where(
                low, pltpu.roll(y, KEY_W, 1), 0.0).astype(BF16)

    d_scale = KEY_W ** -0.5
    for h in range(HEADS):
        sl = slice(h * LANES, (h + 1) * LANES)
        q = _rope(dq_ref[:, sl].astype(F32), ac, as_, KEY_W // 2) * d_scale
        odq_ref[0, :, sl] = jnp.where(low, q, 0.0).astype(BF16)
        odq_ref[1, :, sl] = jnp.where(low, 0.0, q).astype(BF16)
        odk_ref[:, sl] = _rope(dk_ref[:, sl].astype(F32), ac, as_, KEY_W // 2).astype(BF16)
        odv_ref[:, (2 * h) * LANES:(2 * h + 1) * LANES] = dv_ref[:, sl]
        odv_ref[:, (2 * h + 1) * LANES:(2 * h + 2) * LANES] = ones

    g_scale = HEAD_W ** -0.5

    def normed(x, gain):
        return x * lax.rsqrt(jnp.mean(x * x, axis=-1, keepdims=True) + EPS) * gain

    for h in range(HEADS):
        sl = slice(h * LANES, (h + 1) * LANES)
        q = _rope(normed(gq_ref[:, sl].astype(F32), qg_ref[...]), bc, bs, HEAD_W // 2)
        ogq_ref[:, sl] = (q * g_scale).astype(BF16)
    for h in range(GQA_KV):
        sl = slice(h * LANES, (h + 1) * LANES)
        k = _rope(normed(gk_ref[:, sl].astype(F32), kg_ref[...]), bc, bs, HEAD_W // 2)
        ogk_ref[:, sl] = k.astype(BF16)
        ogv_ref[:, (2 * h) * LANES:(2 * h + 1) * LANES] = gv_ref[:, sl]
        ogv_ref[:, (2 * h + 1) * LANES:(2 * h + 2) * LANES] = ones


def _prep(z, tabs, q_gain, k_gain):
    n = z.shape[0]
    tp = _pick(n, (640, 256))

    def zspec(off, width):
        return pl.BlockSpec((tp, width), lambda i, b=off // width: (i, b))

    tab = pl.BlockSpec((tp, LANES), lambda i: (i, 0))
    vec = pl.BlockSpec((1, LANES), lambda i: (0, 0))

    def ospec(width):
        return pl.BlockSpec((tp, width), lambda i: (i, 0))

    return pl.pallas_call(
        _prep_kernel,
        grid=(n // tp,),
        in_specs=[zspec(Z_RQ, 512), zspec(Z_RK, 512), zspec(Z_DQ, 1024), zspec(Z_DK, 1024),
                  zspec(Z_DV, 1024), zspec(Z_GQ, 1024), zspec(Z_GK, 256), zspec(Z_GV, 256),
                  tab, tab, tab, tab, tab, tab, vec, vec],
        out_specs=[ospec(1024), ospec(1024),
                   pl.BlockSpec((2, tp, 1024), lambda i: (0, i, 0)),
                   ospec(1024), ospec(2048), ospec(1024), ospec(256), ospec(512)],
        out_shape=[jax.ShapeDtypeStruct((n, 1024), BF16), jax.ShapeDtypeStruct((n, 1024), BF16),
                   jax.ShapeDtypeStruct((2, n, 1024), BF16), jax.ShapeDtypeStruct((n, 1024), BF16),
                   jax.ShapeDtypeStruct((n, 2048), BF16), jax.ShapeDtypeStruct((n, 1024), BF16),
                   jax.ShapeDtypeStruct((n, 256), BF16), jax.ShapeDtypeStruct((n, 512), BF16)],
        compiler_params=_params("arbitrary"),
    )(z, z, z, z, z, z, z, z, *tabs, q_gain.reshape(1, LANES), k_gain.reshape(1, LANES))


def _rope_tables(n_lat, n_ctx):
    def pattern(cos, sin):
        reps = LANES // (2 * cos.shape[1])
        c = jnp.tile(jnp.concatenate([cos, cos], axis=1), (1, reps))
        s = jnp.tile(jnp.concatenate([-sin, sin], axis=1), (1, reps))
        c = jnp.concatenate([c, jnp.ones((n_ctx, LANES), F32)], axis=0)
        s = jnp.concatenate([s, jnp.zeros((n_ctx, LANES), F32)], axis=0)
        return c, s

    def axial(head_dim):
        n_rows = n_lat // GRID_W
        rows = jnp.repeat(jnp.arange(n_rows), GRID_W).astype(F32)
        cols = jnp.tile(jnp.arange(GRID_W), n_rows).astype(F32)
        n_freq = head_dim // 4
        freqs = ROPE_THETA ** (-jnp.arange(n_freq, dtype=F32) / n_freq)
        ang = jnp.concatenate([rows[:, None] * freqs, cols[:, None] * freqs], axis=-1)
        return jnp.cos(ang), jnp.sin(ang)

    freqs = 1.0 / (ROPE_THETA ** jnp.linspace(0.0, 1.0, KEY_W // 2, dtype=F32))
    ang = jnp.arange(n_lat, dtype=F32)[:, None] * freqs
    return (*pattern(jnp.cos(ang), jnp.sin(ang)), *pattern(*axial(KEY_W)), *pattern(*axial(HEAD_W)))


def _log_decay(lr_ref, direction, h, shape):
    return -jnp.exp(jnp.full(shape, lr_ref[direction, h], F32))


def _ret_sum_kernel(lr_ref, k_ref, v_ref, kv_ref):
    h = pl.program_id(0)
    j = lax.broadcasted_iota(jnp.int32, (CHUNK, LANES), 0).astype(F32)
    w_f = jnp.exp(_log_decay(lr_ref, 0, h, (CHUNK, LANES)) * (CHUNK - 1 - j))
    w_b = jnp.exp(_log_decay(lr_ref, 1, h, (CHUNK, LANES)) * j)
    for c in range(k_ref.shape[0] // CHUNK):
        rows = slice(c * CHUNK, (c + 1) * CHUNK)
        k = k_ref[rows, :].astype(F32)
        kk = (k * w_f + pltpu.roll(k * w_b, KEY_W, 1)).T.astype(BF16)
        kv_ref[c] = jnp.dot(kk, v_ref[rows, :], preferred_element_type=F32)


def _ret_scan_kernel(lr_ref, kv_ref, st_ref, *, n_lat_chunks):
    h = pl.program_id(0)
    nc = kv_ref.shape[0]
    shape = (KEY_W, LANES)
    g_f = jnp.exp(_log_decay(lr_ref, 0, h, shape) * CHUNK)
    g_b = jnp.exp(_log_decay(lr_ref, 1, h, shape) * CHUNK)

    def fwd(c, s):
        st_ref[c, 0:KEY_W, :] = s.astype(BF16)
        return g_f * s + kv_ref[c, 0:KEY_W, :]

    def bwd(t, s):
        c = nc - 1 - t
        st_ref[c, KEY_W:, :] = s.astype(BF16)
        return g_b * s + kv_ref[c, KEY_W:, :]

    zero = jnp.zeros(shape, F32)
    s = lax.fori_loop(n_lat_chunks, nc, fwd, zero)
    lax.fori_loop(0, n_lat_chunks, fwd, s)
    lax.fori_loop(0, nc, bwd, zero)


def _ret_out_kernel(lr_ref, q_ref, k_ref, v_ref, st_ref, o_ref):
    h = pl.program_id(0)
    i = lax.broadcasted_iota(jnp.int32, (CHUNK, CHUNK), 0)
    j = lax.broadcasted_iota(jnp.int32, (CHUNK, CHUNK), 1)
    rel = (i - j).astype(F32)
    lg_f = _log_decay(lr_ref, 0, h, (CHUNK, CHUNK))
    lg_b = _log_decay(lr_ref, 1, h, (CHUNK, CHUNK))
    decay = jnp.where(i >= j, jnp.exp(lg_f * jnp.maximum(rel, 0.0)),
                      jnp.exp(lg_b * jnp.maximum(-rel, 0.0)))
    pos = i.astype(F32)
    cross_f = jnp.exp(lg_f * (pos + 1.0))
    cross_b = jnp.exp(lg_b * (CHUNK - pos))
    for c in range(q_ref.shape[0] // CHUNK):
        rows = slice(c * CHUNK, (c + 1) * CHUNK)
        q = q_ref[rows, :]
        att = lax.dot_general(q, k_ref[rows, :], (((1,), (1,)), ((), ())),
                              preferred_element_type=F32) * decay
        qf = q.astype(F32)
        qs = (qf * cross_f + pltpu.roll(qf * cross_b, KEY_W, 1)).astype(BF16)
        lhs = jnp.concatenate([att.astype(BF16), qs], axis=1)
        rhs = jnp.concatenate([v_ref[rows, :], st_ref[c]], axis=0)
        o_ref[rows, :] = jnp.dot(lhs, rhs, preferred_element_type=F32)


def _retention(rq, rk, z, log_rate, n_lat):
    n = z.shape[0]
    nc = n // CHUNK
    tr = _pick(n, (1280, 640, 256))
    cpt = tr // CHUNK
    smem = pl.BlockSpec(memory_space=pltpu.SMEM)
    head_rows = pl.BlockSpec((tr, LANES), lambda h, i: (i, h))
    v_rows = pl.BlockSpec((tr, LANES), lambda h, i: (i, Z_RV // LANES + h))
    chunk_mats = pl.BlockSpec((None, cpt, CHUNK, LANES), lambda h, i: (h, i, 0, 0))

    kv = pl.pallas_call(
        _ret_sum_kernel,
        grid=(HEADS, n // tr),
        in_specs=[smem, head_rows, v_rows],
        out_specs=chunk_mats,
        out_shape=jax.ShapeDtypeStruct((HEADS, nc, CHUNK, LANES), F32),
        compiler_params=_params("arbitrary", "arbitrary"),
    )(log_rate, rk, z)

    all_chunks = pl.BlockSpec((None, nc, CHUNK, LANES), lambda h: (h, 0, 0, 0))
    st = pl.pallas_call(
        functools.partial(_ret_scan_kernel, n_lat_chunks=n_lat // CHUNK),
        grid=(HEADS,),
        in_specs=[smem, all_chunks],
        out_specs=all_chunks,
        out_shape=jax.ShapeDtypeStruct((HEADS, nc, CHUNK, LANES), BF16),
        compiler_params=_params("arbitrary"),
    )(log_rate, kv)

    return pl.pallas_call(
        _ret_out_kernel,
        grid=(HEADS, n // tr),
        in_specs=[smem, head_rows, head_rows, v_rows, chunk_mats],
        out_specs=head_rows,
        out_shape=jax.ShapeDtypeStruct((n, HEADS * LANES), F32),
        compiler_params=_params("arbitrary", "arbitrary"),
    )(log_rate, rq, rk, z, st)


def _flash_kernel(q_ref, k_ref, v_ref, o_ref, *, tk):
    tq = q_ref.shape[0]
    q = q_ref[...]

    def step(c, carry):
        m, acc = carry
        rows = pl.ds(pl.multiple_of(c * tk, tk), tk)
        s = lax.dot_general(q, k_ref[rows, :], (((1,), (1,)), ((), ())),
                            preferred_element_type=F32)
        m_new = jnp.maximum(m, jnp.max(s, axis=-1, keepdims=True))
        p = jnp.exp(s - m_new)
        acc = jnp.exp(m - m_new) * acc + jnp.dot(p.astype(BF16), v_ref[rows, :],
                                                preferred_element_type=F32)
        return m_new, acc

    m0 = jnp.full((tq, 1), -jnp.inf, F32)
    acc0 = jnp.zeros((tq, 2 * LANES), F32)
    _, acc = lax.fori_loop(0, k_ref.shape[0] // tk, step, (m0, acc0))
    o_ref[...] = (acc[:, :LANES] / acc[:, LANES:]).astype(o_ref.dtype)


def _flash(q, k, v, out_shape, q_map, kv_map, o_map, grid, tq, n_keys, key_block, prev=None):
    tk = _pick(n_keys, (1280, 640, 256))
    q_block = (None,) * (q.ndim - 2) + (tq, LANES)
    o_block = (None,) * (len(out_shape.shape) - 2) + (tq, LANES)
    in_specs = [
        pl.BlockSpec(q_block, q_map),
        pl.BlockSpec((n_keys, LANES), lambda g, r: (key_block, kv_map(g)),
                     pipeline_mode=pl.Buffered(1)),
        pl.BlockSpec((n_keys, 2 * LANES), lambda g, r: (key_block, kv_map(g)),
                     pipeline_mode=pl.Buffered(1)),
    ]
    args = [q, k, v]
    aliases = {}
    kern = functools.partial(_flash_kernel, tk=tk)
    if prev is not None:
        in_specs.append(pl.BlockSpec(memory_space=pl.ANY))
        args.append(prev)
        aliases = {3: 0}
        kern = lambda q_ref, k_ref, v_ref, _, o_ref: _flash_kernel(q_ref, k_ref, v_ref, o_ref, tk=tk)
    return pl.pallas_call(
        kern,
        grid=grid,
        in_specs=in_specs,
        out_specs=pl.BlockSpec(o_block, o_map),
        out_shape=out_shape,
        input_output_aliases=aliases,
        compiler_params=_params("arbitrary", "arbitrary"),
    )(*args)


def _diff_attention(dq, dk, dv, n_lat, with_ctx):
    n = dk.shape[0]
    n_ctx = n - n_lat
    tq = _pick(n_lat, (1024, 512, 256))
    nt = n_lat // tq
    shape = jax.ShapeDtypeStruct((2, n, HEADS * LANES), F32)
    o = _flash(dq, dk, dv, shape,
               lambda g, r: (r // nt, r % nt, g), lambda g: g, lambda g, r: (r // nt, r % nt, g),
               (HEADS, 2 * nt), tq, n, 0)
    if with_ctx:
        cb = n_lat // n_ctx
        o = _flash(dq, dk, dv, shape,
                   lambda g, r: (r, cb, g), lambda g: g, lambda g, r: (r, cb, g),
                   (HEADS, 2), n_ctx, n_ctx, cb, prev=o)
    return o


def _gqa_attention(gq, gk, gv, n_lat, with_ctx):
    n = gk.shape[0]
    n_ctx = n - n_lat
    tq = _pick(n_lat, (1024, 512, 256))
    nt = n_lat // tq
    shape = jax.ShapeDtypeStruct((n, HEADS * LANES), BF16)
    o = _flash(gq, gk, gv, shape,
               lambda g, r: (r % nt, g * GQA_GROUP + r // nt), lambda g: g,
               lambda g, r: (r % nt, g * GQA_GROUP + r // nt),
               (GQA_KV, GQA_GROUP * nt), tq, n, 0)
    if with_ctx:
        cb = n_lat // n_ctx
        o = _flash(gq, gk, gv, shape,
                   lambda g, r: (cb, g * GQA_GROUP + r), lambda g: g,
                   lambda g, r: (cb, g * GQA_GROUP + r),
                   (GQA_KV, GQA_GROUP), n_ctx, n_ctx, cb, prev=o)
    return o


def _finish_kernel(lam_ref, x_ref, mod_ref, gates_ref, rg_ref, dg_ref, gg_ref, ro_ref, do_ref,
                   go_ref, sub_ref, wb_ref, wo_ref, fin_ref, o_ref, *, n_lat, tm, d, lambda_init,
                   final_norm):
    lp = lam_ref[...]
    lam = (jnp.exp(jnp.sum(lp[0:1] * lp[1:2], axis=-1, keepdims=True))
           - jnp.exp(jnp.sum(lp[2:3] * lp[3:4], axis=-1, keepdims=True)) + lambda_init)

    def head_norm(o):
        return o * lax.rsqrt(jnp.mean(o * o, axis=-1, keepdims=True) + EPS)

    branches = []
    for h in range(HEADS):
        sl = slice(h * LANES, (h + 1) * LANES)
        r = head_norm(ro_ref[:, sl]) * _silu(rg_ref[:, sl].astype(F32))
        dd = head_norm(do_ref[0, :, sl] - lam * do_ref[1, :, sl]) * sub_ref[...] * (1.0 - lambda_init)
        dd = dd * _silu(dg_ref[:, sl].astype(F32))
        g = go_ref[:, sl].astype(F32) * _silu(gg_ref[:, sl].astype(F32))
        branches.append((r.astype(BF16), dd.astype(BF16), g.astype(BF16)))

    merged = jnp.zeros((tm, d), F32)
    for b in range(3):
        br = jnp.concatenate([branches[h][b] for h in range(HEADS)], axis=1)
        y = jnp.dot(br, wb_ref[b], preferred_element_type=F32)
        merged = merged + _sigmoid(gates_ref[:, b * d:(b + 1) * d].astype(F32)) * y
    out = jnp.dot(merged.astype(BF16), wo_ref[...], preferred_element_type=F32)

    rows = pl.program_id(0) * tm + lax.broadcasted_iota(jnp.int32, (tm, 1), 0)
    gate = _row_mod(mod_ref, 2 * d, 3 * d, rows >= n_lat)
    x = x_ref[...] + gate * out
    if final_norm:
        x = x * lax.rsqrt(jnp.mean(x * x, axis=-1, keepdims=True) + EPS) * fin_ref[...]
    o_ref[...] = x


def _finish(xx, z, mod, lam_params, ro, do, go, subln, wb, wo, fin_gain, n_lat, lambda_init, last):
    n, d = xx.shape
    tm = 256
    n_rows = n_lat if last else n
    bw = HEADS * LANES

    def rows(width, off=0):
        return pl.BlockSpec((tm, width), lambda i, b=off // width: (i, b))

    const2 = lambda i: (0, 0)
    kern = functools.partial(_finish_kernel, n_lat=n_lat, tm=tm, d=d, lambda_init=lambda_init,
                             final_norm=last)
    return pl.pallas_call(
        kern,
        grid=(n_rows // tm,),
        in_specs=[
            pl.BlockSpec((4, KEY_W), const2),
            rows(d),
            pl.BlockSpec((8, 3 * d), const2),
            rows(3 * d, Z_GATES), rows(bw, Z_RG), rows(bw, Z_DG), rows(bw, Z_GG),
            rows(bw),
            pl.BlockSpec((2, tm, bw), lambda i: (0, i, 0)),
            rows(bw),
            pl.BlockSpec((1, LANES), const2),
            pl.BlockSpec((3, bw, d), lambda i: (0, 0, 0), pipeline_mode=pl.Buffered(1)),
            pl.BlockSpec((d, d), const2, pipeline_mode=pl.Buffered(1)),
            pl.BlockSpec((1, d), const2),
        ],
        out_specs=rows(d),
        out_shape=jax.ShapeDtypeStruct((n_rows, d), F32),
        compiler_params=_params("arbitrary"),
    )(lam_params, xx, mod, z, z, z, z, ro, do, go, subln.reshape(1, LANES), wb, wo,
      fin_gain.reshape(1, d))


def kernel(x, c, ctx, c_ctx, norm_gain, w_ada, b_ada, w_in, ret_log_rate, diff_lambda,
           diff_subln_gain, gqa_q_gain, gqa_k_gain, w_branch, w_out, final_norm_gain):
    _, n_lat, d = x.shape
    n_ctx = ctx.shape[1]
    depth = w_in.shape[0]
    assert x.shape[0] == 1 and d == 2048 and w_in.shape[2] == Z_COLS
    assert n_lat % n_ctx == 0 and n_ctx % CHUNK == 0 and n_lat % GRID_W == 0

    xx = jnp.concatenate([x[0], ctx[0]], axis=0)
    c8 = jnp.concatenate([c, c_ctx[None], jnp.zeros((6, d), F32)], axis=0)
    mods = _ada_all(c8, w_ada, b_ada)
    tabs = _rope_tables(n_lat, n_ctx)
    w_in16 = w_in.astype(BF16)
    wb16 = w_branch.astype(BF16)
    wo16 = w_out.astype(BF16)

    for l in range(depth):
        last = l == depth - 1
        lambda_init = 0.8 - 0.6 * math.exp(-0.3 * l)
        z = _in_proj(xx, mods[l], norm_gain[l], w_in16[l], n_lat)
        rq, rk, dq, dk, dv, gq, gk, gv = _prep(z, tabs, gqa_q_gain[l], gqa_k_gain[l])
        ro = _retention(rq, rk, z, ret_log_rate[l], n_lat)
        do = _diff_attention(dq, dk, dv, n_lat, not last)
        go = _gqa_attention(gq, gk, gv, n_lat, not last)
        xx = _finish(xx, z, mods[l], diff_lambda[l], ro, do, go, diff_subln_gain[l], wb16[l],
                     wo16[l], final_norm_gain, n_lat, lambda_init, last)
    return xx[None]
```

```python
import functools
import math

import jax
import jax.numpy as jnp
from jax import lax
from jax.experimental import pallas as pl
from jax.experimental.pallas import tpu as pltpu

F32 = jnp.float32
BF16 = jnp.bfloat16

EPS = 1e-6
ROPE_THETA = 10000.0
GRID_W = 64
LANES = 128
VMEM_LIMIT = 56 * 1024 * 1024

HEADS = 8
HEAD_W = 128
KEY_W = 64
GQA_KV = 2
GQA_GROUP = 4
CHUNK = 128

Z_GATES, Z_RQ, Z_RK, Z_RV, Z_RG = 0, 6144, 6656, 7168, 8192
Z_DQ, Z_DK, Z_DV, Z_DG = 9216, 10240, 11264, 12288
Z_GQ, Z_GG, Z_GK, Z_GV = 13312, 14336, 15360, 15616
Z_COLS = 15872
W_TILE = 512
NORM_ROWS = 128


def _pick(n, candidates):
    for c in candidates:
        if n % c == 0:
            return c
    raise ValueError(f"no tile in {candidates} divides {n}")


def _params(*sem):
    return pltpu.CompilerParams(dimension_semantics=sem, vmem_limit_bytes=VMEM_LIMIT)


def _sigmoid(x):
    return 1.0 / (1.0 + jnp.exp(-x))


def _silu(x):
    return x * _sigmoid(x)


def _ada_kernel(c_ref, w_ref, b_ref, o_ref):
    s = _silu(c_ref[...])
    o_ref[...] = jnp.dot(s, w_ref[...], preferred_element_type=F32,
                         precision=lax.Precision.HIGHEST) + b_ref[...]


def _ada_all(c8, w_ada, b_ada):
    depth, d, d3 = w_ada.shape
    tn = 1024
    return pl.pallas_call(
        _ada_kernel,
        grid=(depth, d3 // tn),
        in_specs=[
            pl.BlockSpec((8, d), lambda l, j: (0, 0)),
            pl.BlockSpec((None, d, tn), lambda l, j: (l, 0, j)),
            pl.BlockSpec((None, 1, tn), lambda l, j: (l, 0, j)),
        ],
        out_specs=pl.BlockSpec((None, 8, tn), lambda l, j: (l, 0, j)),
        out_shape=jax.ShapeDtypeStruct((depth, 8, d3), F32),
        compiler_params=_params("arbitrary", "arbitrary"),
    )(c8, w_ada, b_ada.reshape(depth, 1, d3))


def _row_mod(mod_ref, lo, hi, is_ctx):
    return jnp.where(is_ctx, mod_ref[1:2, lo:hi], mod_ref[0:1, lo:hi])


def _inproj_kernel(x_ref, mod_ref, g_ref, w_ref, z_ref, h_ref, *, n_lat, tm, d):
    i = pl.program_id(0)

    @pl.when(pl.program_id(1) == 0)
    def _():
        def norm_rows(r, carry):
            sl = pl.ds(pl.multiple_of(r * NORM_ROWS, NORM_ROWS), NORM_ROWS)
            x = x_ref[sl, :]
            y = x * lax.rsqrt(jnp.mean(x * x, axis=-1, keepdims=True) + EPS) * g_ref[...]
            rows = i * tm + r * NORM_ROWS + lax.broadcasted_iota(jnp.int32, (NORM_ROWS, 1), 0)
            is_ctx = rows >= n_lat
            shift = _row_mod(mod_ref, 0, d, is_ctx)
            scale = _row_mod(mod_ref, d, 2 * d, is_ctx)
            h_ref[sl, :] = (y * (1.0 + scale) + shift).astype(BF16)
            return carry

        lax.fori_loop(0, tm // NORM_ROWS, norm_rows, 0)

    z_ref[...] = jnp.dot(h_ref[...], w_ref[...], preferred_element_type=F32).astype(BF16)


def _w_block(j):
    return jnp.where(j < 12, j + 19, jnp.where(j < 28, j - 12, jnp.where(j < 30, j - 11, 16)))


def _in_proj(xx, mod, gain, w, n_lat):
    n, d = xx.shape
    tm = _pick(n, (1280, 640, 256))
    kern = functools.partial(_inproj_kernel, n_lat=n_lat, tm=tm, d=d)
    return pl.pallas_call(
        kern,
        grid=(n // tm, Z_COLS // W_TILE),
        in_specs=[
            pl.BlockSpec((tm, d), lambda i, j: (i, 0)),
            pl.BlockSpec((8, 3 * d), lambda i, j: (0, 0)),
            pl.BlockSpec((1, d), lambda i, j: (0, 0)),
            pl.BlockSpec((d, W_TILE), lambda i, j: (0, _w_block(j))),
        ],
        out_specs=pl.BlockSpec((tm, W_TILE), lambda i, j: (i, j)),
        out_shape=jax.ShapeDtypeStruct((n, Z_COLS), BF16),
        scratch_shapes=[pltpu.VMEM((tm, d), BF16)],
        compiler_params=_params("arbitrary", "arbitrary"),
    )(xx, mod, gain.reshape(1, d), w)


def _swap_halves(x, half):
    if 2 * half == LANES:
        return pltpu.roll(x, half, 1)
    lane = lax.broadcasted_iota(jnp.int32, x.shape, 1)
    return jnp.where(lane % (2 * half) < half,
                     pltpu.roll(x, LANES - half, 1), pltpu.roll(x, half, 1))


def _rope(x, c, s, half):
    return x * c + _swap_halves(x, half) * s


def _prep_kernel(rq_ref, rk_ref, dq_ref, dk_ref, dv_ref, gq_ref, gk_ref, gv_ref,
                 sc_ref, ss_ref, ac_ref, as_ref, bc_ref, bs_ref, qg_ref, kg_ref,
                 orq_ref, ork_ref, odq_ref, odk_ref, odv_ref, ogq_ref, ogk_ref, ogv_ref):
    tp = rq_ref.shape[0]
    lane = lax.broadcasted_iota(jnp.int32, (tp, LANES), 1)
    low = lane < KEY_W
    sc, ss = sc_ref[...], ss_ref[...]
    ac, as_ = ac_ref[...], as_ref[...]
    bc, bs = bc_ref[...], bs_ref[...]
    ones = jnp.ones((tp, LANES), BF16)

    k_scale = KEY_W ** -0.5
    for p in range(HEADS // 2):
        sl = slice(p * LANES, (p + 1) * LANES)
        for src, dst, mul in ((rq_ref, orq_ref, 1.0), (rk_ref, ork_ref, k_scale)):
            y = _rope(src[:, sl].astype(F32), sc, ss, KEY_W // 2) * mul
            dst[:, (2 * p) * LANES:(2 * p + 1) * LANES] = jnp.where(low, y, 0.0).astype(BF16)
            dst[:, (2 * p + 1) * LANES:(2 * p + 2) * LANES] = jnp.where(
                low, pltpu.roll(y, KEY_W, 1), 0.0).astype(BF16)

    d_scale = KEY_W ** -0.5
    for h in range(HEADS):
        sl = slice(h * LANES, (h + 1) * LANES)
        q = _rope(dq_ref[:, sl].astype(F32), ac, as_, KEY_W // 2) * d_scale
        odq_ref[0, :, sl] = jnp.where(low, q, 0.0).astype(BF16)
        odq_ref[1, :, sl] = jnp.where(low, 0.0, q).astype(BF16)
        odk_ref[:, sl] = _rope(dk_ref[:, sl].astype(F32), ac, as_, KEY_W // 2).astype(BF16)
        odv_ref[:, (2 * h) * LANES:(2 * h + 1) * LANES] = dv_ref[:, sl]
        odv_ref[:, (2 * h + 1) * LANES:(2 * h + 2) * LANES] = ones

    g_scale = HEAD_W ** -0.5

    def normed(x, gain):
        return x * lax.rsqrt(jnp.mean(x * x, axis=-1, keepdims=True) + EPS) * gain

    for h in range(HEADS):
        sl = slice(h * LANES, (h + 1) * LANES)
        q = _rope(normed(gq_ref[:, sl].astype(F32), qg_ref[...]), bc, bs, HEAD_W // 2)
        ogq_ref[:, sl] = (q * g_scale).astype(BF16)
    for h in range(GQA_KV):
        sl = slice(h * LANES, (h + 1) * LANES)
        k = _rope(normed(gk_ref[:, sl].astype(F32), kg_ref[...]), bc, bs, HEAD_W // 2)
        ogk_ref[:, sl] = k.astype(BF16)
        ogv_ref[:, (2 * h) * LANES:(2 * h + 1) * LANES] = gv_ref[:, sl]
        ogv_ref[:, (2 * h + 1) * LANES:(2 * h + 2) * LANES] = ones


def _prep(z, tabs, q_gain, k_gain):
    n = z.shape[0]
    tp = _pick(n, (640, 256))

    def zspec(off, width):
        return pl.BlockSpec((tp, width), lambda i, b=off // width: (i, b))

    tab = pl.BlockSpec((tp, LANES), lambda i: (i, 0))
    vec = pl.BlockSpec((1, LANES), lambda i: (0, 0))

    def ospec(width):
        return pl.BlockSpec((tp, width), lambda i: (i, 0))

    return pl.pallas_call(
        _prep_kernel,
        grid=(n // tp,),
        in_specs=[zspec(Z_RQ, 512), zspec(Z_RK, 512), zspec(Z_DQ, 1024), zspec(Z_DK, 1024),
                  zspec(Z_DV, 1024), zspec(Z_GQ, 1024), zspec(Z_GK, 256), zspec(Z_GV, 256),
                  tab, tab, tab, tab, tab, tab, vec, vec],
        out_specs=[ospec(1024), ospec(1024),
                   pl.BlockSpec((2, tp, 1024), lambda i: (0, i, 0)),
                   ospec(1024), ospec(2048), ospec(1024), ospec(256), ospec(512)],
        out_shape=[jax.ShapeDtypeStruct((n, 1024), BF16), jax.ShapeDtypeStruct((n, 1024), BF16),
                   jax.ShapeDtypeStruct((2, n, 1024), BF16), jax.ShapeDtypeStruct((n, 1024), BF16),
                   jax.ShapeDtypeStruct((n, 2048), BF16), jax.ShapeDtypeStruct((n, 1024), BF16),
                   jax.ShapeDtypeStruct((n, 256), BF16), jax.ShapeDtypeStruct((n, 512), BF16)],
        compiler_params=_params("arbitrary"),
    )(z, z, z, z, z, z, z, z, *tabs, q_gain.reshape(1, LANES), k_gain.reshape(1, LANES))


def _rope_tables(n_lat, n_ctx):
    def pattern(cos, sin):
        reps = LANES // (2 * cos.shape[1])
        c = jnp.tile(jnp.concatenate([cos, cos], axis=1), (1, reps))
        s = jnp.tile(jnp.concatenate([-sin, sin], axis=1), (1, reps))
        c = jnp.concatenate([c, jnp.ones((n_ctx, LANES), F32)], axis=0)
        s = jnp.concatenate([s, jnp.zeros((n_ctx, LANES), F32)], axis=0)
        return c, s

    def axial(head_dim):
        n_rows = n_lat // GRID_W
        rows = jnp.repeat(jnp.arange(n_rows), GRID_W).astype(F32)
        cols = jnp.tile(jnp.arange(GRID_W), n_rows).astype(F32)
        n_freq = head_dim // 4
        freqs = ROPE_THETA ** (-jnp.arange(n_freq, dtype=F32) / n_freq)
        ang = jnp.concatenate([rows[:, None] * freqs, cols[:, None] * freqs], axis=-1)
        return jnp.cos(ang), jnp.sin(ang)

    freqs = 1.0 / (ROPE_THETA ** jnp.linspace(0.0, 1.0, KEY_W // 2, dtype=F32))
    ang = jnp.arange(n_lat, dtype=F32)[:, None] * freqs
    return (*pattern(jnp.cos(ang), jnp.sin(ang)), *pattern(*axial(KEY_W)), *pattern(*axial(HEAD_W)))


def _log_decay(lr_ref, direction, h, shape):
    return -jnp.exp(jnp.full(shape, lr_ref[direction, h], F32))


def _ret_sum_kernel(lr_ref, k_ref, v_ref, kv_ref):
    h = pl.program_id(0)
    j = lax.broadcasted_iota(jnp.int32, (CHUNK, LANES), 0).astype(F32)
    w_f = jnp.exp(_log_decay(lr_ref, 0, h, (CHUNK, LANES)) * (CHUNK - 1 - j))
    w_b = jnp.exp(_log_decay(lr_ref, 1, h, (CHUNK, LANES)) * j)
    for c in range(k_ref.shape[0] // CHUNK):
        rows = slice(c * CHUNK, (c + 1) * CHUNK)
        k = k_ref[rows, :].astype(F32)
        kk = (k * w_f + pltpu.roll(k * w_b, KEY_W, 1)).T.astype(BF16)
        kv_ref[c] = jnp.dot(kk, v_ref[rows, :], preferred_element_type=F32)


def _ret_scan_kernel(lr_ref, kv_ref, st_ref, *, n_lat_chunks):
    h = pl.program_id(0)
    nc = kv_ref.shape[0]
    shape = (KEY_W, LANES)
    g_f = jnp.exp(_log_decay(lr_ref, 0, h, shape) * CHUNK)
    g_b = jnp.exp(_log_decay(lr_ref, 1, h, shape) * CHUNK)

    def fwd(c, s):
        st_ref[c, 0:KEY_W, :] = s.astype(BF16)
        return g_f * s + kv_ref[c, 0:KEY_W, :]

    def bwd(t, s):
        c = nc - 1 - t
        st_ref[c, KEY_W:, :] = s.astype(BF16)
        return g_b * s + kv_ref[c, KEY_W:, :]

    zero = jnp.zeros(shape, F32)
    s = lax.fori_loop(n_lat_chunks, nc, fwd, zero)
    lax.fori_loop(0, n_lat_chunks, fwd, s)
    lax.fori_loop(0, nc, bwd, zero)


def _ret_out_kernel(lr_ref, q_ref, k_ref, v_ref, st_ref, o_ref):
    h = pl.program_id(0)
    i = lax.broadcasted_iota(jnp.int32, (CHUNK, CHUNK), 0)
    j = lax.broadcasted_iota(jnp.int32, (CHUNK, CHUNK), 1)
    rel = (i - j).astype(F32)
    lg_f = _log_decay(lr_ref, 0, h, (CHUNK, CHUNK))
    lg_b = _log_decay(lr_ref, 1, h, (CHUNK, CHUNK))
    decay = jnp.where(i >= j, jnp.exp(lg_f * jnp.maximum(rel, 0.0)),
                      jnp.exp(lg_b * jnp.maximum(-rel, 0.0)))
    pos = i.astype(F32)
    cross_f = jnp.exp(lg_f * (pos + 1.0))
    cross_b = jnp.exp(lg_b * (CHUNK - pos))
    for c in range(q_ref.shape[0] // CHUNK):
        rows = slice(c * CHUNK, (c + 1) * CHUNK)
        q = q_ref[rows, :]
        att = lax.dot_general(q, k_ref[rows, :], (((1,), (1,)), ((), ())),
                              preferred_element_type=F32) * decay
        qf = q.astype(F32)
        qs = (qf * cross_f + pltpu.roll(qf * cross_b, KEY_W, 1)).astype(BF16)
        lhs = jnp.concatenate([att.astype(BF16), qs], axis=1)
        rhs = jnp.concatenate([v_ref[rows, :], st_ref[c]], axis=0)
        o_ref[rows, :] = jnp.dot(lhs, rhs, preferred_element_type=F32)


def _retention(rq, rk, z, log_rate, n_lat):
    n = z.shape[0]
    nc = n // CHUNK
    tr = _pick(n, (1280, 640, 256))
    cpt = tr // CHUNK
    smem = pl.BlockSpec(memory_space=pltpu.SMEM)
    head_rows = pl.BlockSpec((tr, LANES), lambda h, i: (i, h))
    v_rows = pl.BlockSpec((tr, LANES), lambda h, i: (i, Z_RV // LANES + h))
    chunk_mats = pl.BlockSpec((None, cpt, CHUNK, LANES), lambda h, i: (h, i, 0, 0))

    kv = pl.pallas_call(
        _ret_sum_kernel,
        grid=(HEADS, n // tr),
        in_specs=[smem, head_rows, v_rows],
        out_specs=chunk_mats,
        out_shape=jax.ShapeDtypeStruct((HEADS, nc, CHUNK, LANES), F32),
        compiler_params=_params("arbitrary", "arbitrary"),
    )(log_rate, rk, z)

    all_chunks = pl.BlockSpec((None, nc, CHUNK, LANES), lambda h: (h, 0, 0, 0))
    st = pl.pallas_call(
        functools.partial(_ret_scan_kernel, n_lat_chunks=n_lat // CHUNK),
        grid=(HEADS,),
        in_specs=[smem, all_chunks],
        out_specs=all_chunks,
        out_shape=jax.ShapeDtypeStruct((HEADS, nc, CHUNK, LANES), BF16),
        compiler_params=_params("arbitrary"),
    )(log_rate, kv)

    return pl.pallas_call(
        _ret_out_kernel,
        grid=(HEADS, n // tr),
        in_specs=[smem, head_rows, head_rows, v_rows, chunk_mats],
        out_specs=head_rows,
        out_shape=jax.ShapeDtypeStruct((n, HEADS * LANES), F32),
        compiler_params=_params("arbitrary", "arbitrary"),
    )(log_rate, rq, rk, z, st)


def _flash_kernel(q_ref, k_ref, v_ref, *rest, tk):
    o_ref, s0_ref, s1_ref, m_ref, acc_ref = rest[-5:]
    nk = k_ref.shape[0] // tk
    q = q_ref[...]

    def key_rows(c):
        return pl.ds(pl.multiple_of(c * tk, tk), tk)

    def scores(c, dst):
        dst[...] = lax.dot_general(q, k_ref[key_rows(c), :], (((1,), (1,)), ((), ())),
                                   preferred_element_type=F32)

    def consume(c, src):
        s = src[...]
        m = m_ref[...]
        m_new = jnp.maximum(m, jnp.max(s, axis=-1, keepdims=True))
        p = jnp.exp(s - m_new)
        acc_ref[...] = jnp.exp(m - m_new) * acc_ref[...] + jnp.dot(
            p.astype(BF16), v_ref[key_rows(c), :], preferred_element_type=F32)
        m_ref[...] = m_new

    m_ref[...] = jnp.full(m_ref.shape, -jnp.inf, F32)
    acc_ref[...] = jnp.zeros(acc_ref.shape, F32)
    scores(0, s0_ref)

    def pair(i, carry):
        c = 2 * i
        scores(c + 1, s1_ref)
        consume(c, s0_ref)
        scores(c + 2, s0_ref)
        consume(c + 1, s1_ref)
        return carry

    lax.fori_loop(0, (nk - 1) // 2, pair, 0)
    if nk % 2 == 0:
        scores(nk - 1, s1_ref)
        consume(nk - 2, s0_ref)
        consume(nk - 1, s1_ref)
    else:
        consume(nk - 1, s0_ref)
    acc = acc_ref[...]
    o_ref[...] = (acc[:, :LANES] / acc[:, LANES:]).astype(o_ref.dtype)


def _flash(q, k, v, out_shape, q_map, kv_map, o_map, grid, tq, n_keys, key_block, prev=None):
    tk = _pick(n_keys, (1280, 640, 256))
    q_block = (None,) * (q.ndim - 2) + (tq, LANES)
    o_block = (None,) * (len(out_shape.shape) - 2) + (tq, LANES)
    in_specs = [
        pl.BlockSpec(q_block, q_map),
        pl.BlockSpec((n_keys, LANES), lambda g, r: (key_block, kv_map(g)),
                     pipeline_mode=pl.Buffered(1)),
        pl.BlockSpec((n_keys, 2 * LANES), lambda g, r: (key_block, kv_map(g)),
                     pipeline_mode=pl.Buffered(1)),
    ]
    args = [q, k, v]
    aliases = {}
    if prev is not None:
        in_specs.append(pl.BlockSpec(memory_space=pl.ANY))
        args.append(prev)
        aliases = {3: 0}
    return pl.pallas_call(
        functools.partial(_flash_kernel, tk=tk),
        grid=grid,
        in_specs=in_specs,
        out_specs=pl.BlockSpec(o_block, o_map),
        out_shape=out_shape,
        scratch_shapes=[pltpu.VMEM((tq, tk), F32), pltpu.VMEM((tq, tk), F32),
                        pltpu.VMEM((tq, 1), F32), pltpu.VMEM((tq, 2 * LANES), F32)],
        input_output_aliases=aliases,
        compiler_params=_params("arbitrary", "arbitrary"),
    )(*args)


def _diff_attention(dq, dk, dv, n_lat, with_ctx):
    n = dk.shape[0]
    n_ctx = n - n_lat
    tq = _pick(n_lat, (1024, 512, 256))
    nt = n_lat // tq
    shape = jax.ShapeDtypeStruct((2, n, HEADS * LANES), F32)
    o = _flash(dq, dk, dv, shape,
               lambda g, r: (r // nt, r % nt, g), lambda g: g, lambda g, r: (r // nt, r % nt, g),
               (HEADS, 2 * nt), tq, n, 0)
    if with_ctx:
        cb = n_lat // n_ctx
        o = _flash(dq, dk, dv, shape,
                   lambda g, r: (r, cb, g), lambda g: g, lambda g, r: (r, cb, g),
                   (HEADS, 2), n_ctx, n_ctx, cb, prev=o)
    return o


def _gqa_attention(gq, gk, gv, n_lat, with_ctx):
    n = gk.shape[0]
    n_ctx = n - n_lat
    tq = _pick(n_lat, (1024, 512, 256))
    nt = n_lat // tq
    shape = jax.ShapeDtypeStruct((n, HEADS * LANES), BF16)
    o = _flash(gq, gk, gv, shape,
               lambda g, r: (r % nt, g * GQA_GROUP + r // nt), lambda g: g,
               lambda g, r: (r % nt, g * GQA_GROUP + r // nt),
               (GQA_KV, GQA_GROUP * nt), tq, n, 0)
    if with_ctx:
        cb = n_lat // n_ctx
        o = _flash(gq, gk, gv, shape,
                   lambda g, r: (cb, g * GQA_GROUP + r), lambda g: g,
                   lambda g, r: (cb, g * GQA_GROUP + r),
                   (GQA_KV, GQA_GROUP), n_ctx, n_ctx, cb, prev=o)
    return o


def _finish_kernel(lam_ref, x_ref, mod_ref, gates_ref, rg_ref, dg_ref, gg_ref, ro_ref, do_ref,
                   go_ref, sub_ref, wb_ref, wo_ref, fin_ref, o_ref, *, n_lat, tm, d, lambda_init,
                   final_norm):
    lp = lam_ref[...]
    lam = (jnp.exp(jnp.sum(lp[0:1] * lp[1:2], axis=-1, keepdims=True))
           - jnp.exp(jnp.sum(lp[2:3] * lp[3:4], axis=-1, keepdims=True)) + lambda_init)

    def head_norm(o):
        return o * lax.rsqrt(jnp.mean(o * o, axis=-1, keepdims=True) + EPS)

    branches = []
    for h in range(HEADS):
        sl = slice(h * LANES, (h + 1) * LANES)
        r = head_norm(ro_ref[:, sl]) * _silu(rg_ref[:, sl].astype(F32))
        dd = head_norm(do_ref[0, :, sl] - lam * do_ref[1, :, sl]) * sub_ref[...] * (1.0 - lambda_init)
        dd = dd * _silu(dg_ref[:, sl].astype(F32))
        g = go_ref[:, sl].astype(F32) * _silu(gg_ref[:, sl].astype(F32))
        branches.append((r.astype(BF16), dd.astype(BF16), g.astype(BF16)))

    merged = jnp.zeros((tm, d), F32)
    for b in range(3):
        br = jnp.concatenate([branches[h][b] for h in range(HEADS)], axis=1)
        y = jnp.dot(br, wb_ref[b], preferred_element_type=F32)
        merged = merged + _sigmoid(gates_ref[:, b * d:(b + 1) * d].astype(F32)) * y
    out = jnp.dot(merged.astype(BF16), wo_ref[...], preferred_element_type=F32)

    rows = pl.program_id(0) * tm + lax.broadcasted_iota(jnp.int32, (tm, 1), 0)
    gate = _row_mod(mod_ref, 2 * d, 3 * d, rows >= n_lat)
    x = x_ref[...] + gate * out
    if final_norm:
        x = x * lax.rsqrt(jnp.mean(x * x, axis=-1, keepdims=True) + EPS) * fin_ref[...]
    o_ref[...] = x


def _finish(xx, z, mod, lam_params, ro, do, go, subln, wb, wo, fin_gain, n_lat, lambda_init, last):
    n, d = xx.shape
    tm = 256
    n_rows = n_lat if last else n
    bw = HEADS * LANES

    def rows(width, off=0):
        return pl.BlockSpec((tm, width), lambda i, b=off // width: (i, b))

    const2 = lambda i: (0, 0)
    kern = functools.partial(_finish_kernel, n_lat=n_lat, tm=tm, d=d, lambda_init=lambda_init,
                             final_norm=last)
    return pl.pallas_call(
        kern,
        grid=(n_rows // tm,),
        in_specs=[
            pl.BlockSpec((4, KEY_W), const2),
            rows(d),
            pl.BlockSpec((8, 3 * d), const2),
            rows(3 * d, Z_GATES), rows(bw, Z_RG), rows(bw, Z_DG), rows(bw, Z_GG),
            rows(bw),
            pl.BlockSpec((2, tm, bw), lambda i: (0, i, 0)),
            rows(bw),
            pl.BlockSpec((1, LANES), const2),
            pl.BlockSpec((3, bw, d), lambda i: (0, 0, 0), pipeline_mode=pl.Buffered(1)),
            pl.BlockSpec((d, d), const2, pipeline_mode=pl.Buffered(1)),
            pl.BlockSpec((1, d), const2),
        ],
        out_specs=rows(d),
        out_shape=jax.ShapeDtypeStruct((n_rows, d), F32),
        compiler_params=_params("arbitrary"),
    )(lam_params, xx, mod, z, z, z, z, ro, do, go, subln.reshape(1, LANES), wb, wo,
      fin_gain.reshape(1, d))


def kernel(x, c, ctx, c_ctx, norm_gain, w_ada, b_ada, w_in, ret_log_rate, diff_lambda,
           diff_subln_gain, gqa_q_gain, gqa_k_gain, w_branch, w_out, final_norm_gain):
    _, n_lat, d = x.shape
    n_ctx = ctx.shape[1]
    depth = w_in.shape[0]
    assert x.shape[0] == 1 and d == 2048 and w_in.shape[2] == Z_COLS
    assert n_lat % n_ctx == 0 and n_ctx % CHUNK == 0 and n_lat % GRID_W == 0

    xx = jnp.concatenate([x[0], ctx[0]], axis=0)
    c8 = jnp.concatenate([c, c_ctx[None], jnp.zeros((6, d), F32)], axis=0)
    mods = _ada_all(c8, w_ada, b_ada)
    tabs = _rope_tables(n_lat, n_ctx)
    w_in16 = w_in.astype(BF16)
    wb16 = w_branch.astype(BF16)
    wo16 = w_out.astype(BF16)

    for l in range(depth):
        last = l == depth - 1
        lambda_init = 0.8 - 0.6 * math.exp(-0.3 * l)
        z = _in_proj(xx, mods[l], norm_gain[l], w_in16[l], n_lat)
        rq, rk, dq, dk, dv, gq, gk, gv = _prep(z, tabs, gqa_q_gain[l], gqa_k_gain[l])
        ro = _retention(rq, rk, z, ret_log_rate[l], n_lat)
        do = _diff_attention(dq, dk, dv, n_lat, not last)
        go = _gqa_attention(gq, gk, gv, n_lat, not last)
        xx = _finish(xx, z, mods[l], diff_lambda[l], ro, do, go, diff_subln_gain[l], wb16[l],
                     wo16[l], final_norm_gain, n_lat, lambda_init, last)
    return xx[None]
```

```python
import functools
import math

import jax
import jax.numpy as jnp
from jax import lax
from jax.experimental import pallas as pl
from jax.experimental.pallas import tpu as pltpu

F32 = jnp.float32
BF16 = jnp.bfloat16

EPS = 1e-6
ROPE_THETA = 10000.0
GRID_W = 64
LANES = 128
VMEM_LIMIT = 56 * 1024 * 1024

HEADS = 8
HEAD_W = 128
KEY_W = 64
GQA_KV = 2
GQA_GROUP = 4
CHUNK = 128

Z_GATES, Z_RQ, Z_RK, Z_RV, Z_RG = 0, 6144, 6656, 7168, 8192
Z_DQ, Z_DK, Z_DV, Z_DG = 9216, 10240, 11264, 12288
Z_GQ, Z_GG, Z_GK, Z_GV = 13312, 14336, 15360, 15616
Z_COLS = 15872
W_TILE = 512
NORM_ROWS = 128
LOG2_E = math.log2(math.e)


def _pick(n, candidates):
    for c in candidates:
        if n % c == 0:
            return c
    raise ValueError(f"no tile in {candidates} divides {n}")


def _params(*sem):
    return pltpu.CompilerParams(dimension_semantics=sem, vmem_limit_bytes=VMEM_LIMIT)


def _sigmoid(x):
    return 1.0 / (1.0 + jnp.exp(-x))


def _silu(x):
    return x * _sigmoid(x)


def _ada_kernel(c_ref, w_ref, b_ref, o_ref):
    s = _silu(c_ref[...])
    o_ref[...] = jnp.dot(s, w_ref[...], preferred_element_type=F32,
                         precision=lax.Precision.HIGHEST) + b_ref[...]


def _ada_all(c8, w_ada, b_ada):
    depth, d, d3 = w_ada.shape
    tn = 1024
    return pl.pallas_call(
        _ada_kernel,
        grid=(depth, d3 // tn),
        in_specs=[
            pl.BlockSpec((8, d), lambda l, j: (0, 0)),
            pl.BlockSpec((None, d, tn), lambda l, j: (l, 0, j)),
            pl.BlockSpec((None, 1, tn), lambda l, j: (l, 0, j)),
        ],
        out_specs=pl.BlockSpec((None, 8, tn), lambda l, j: (l, 0, j)),
        out_shape=jax.ShapeDtypeStruct((depth, 8, d3), F32),
        compiler_params=_params("arbitrary", "arbitrary"),
    )(c8, w_ada, b_ada.reshape(depth, 1, d3))


def _row_mod(mod_ref, lo, hi, is_ctx):
    return jnp.where(is_ctx, mod_ref[1:2, lo:hi], mod_ref[0:1, lo:hi])


def _inproj_kernel(x_ref, mod_ref, g_ref, w_ref, z_ref, h_ref, *, n_lat, tm, d):
    i = pl.program_id(0)

    @pl.when(pl.program_id(1) == 0)
    def _():
        def norm_rows(r, carry):
            sl = pl.ds(pl.multiple_of(r * NORM_ROWS, NORM_ROWS), NORM_ROWS)
            x = x_ref[sl, :]
            y = x * lax.rsqrt(jnp.mean(x * x, axis=-1, keepdims=True) + EPS) * g_ref[...]
            rows = i * tm + r * NORM_ROWS + lax.broadcasted_iota(jnp.int32, (NORM_ROWS, 1), 0)
            is_ctx = rows >= n_lat
            shift = _row_mod(mod_ref, 0, d, is_ctx)
            scale = _row_mod(mod_ref, d, 2 * d, is_ctx)
            h_ref[sl, :] = (y * (1.0 + scale) + shift).astype(BF16)
            return carry

        lax.fori_loop(0, tm // NORM_ROWS, norm_rows, 0)

    z_ref[...] = jnp.dot(h_ref[...], w_ref[...], preferred_element_type=F32).astype(BF16)


def _w_block(j):
    return jnp.where(j < 12, j + 19, jnp.where(j < 28, j - 12, jnp.where(j < 30, j - 11, 16)))


def _in_proj(xx, mod, gain, w, n_lat):
    n, d = xx.shape
    tm = _pick(n, (1280, 640, 256))
    kern = functools.partial(_inproj_kernel, n_lat=n_lat, tm=tm, d=d)
    return pl.pallas_call(
        kern,
        grid=(n // tm, Z_COLS // W_TILE),
        in_specs=[
            pl.BlockSpec((tm, d), lambda i, j: (i, 0)),
            pl.BlockSpec((8, 3 * d), lambda i, j: (0, 0)),
            pl.BlockSpec((1, d), lambda i, j: (0, 0)),
            pl.BlockSpec((d, W_TILE), lambda i, j: (0, _w_block(j))),
        ],
        out_specs=pl.BlockSpec((tm, W_TILE), lambda i, j: (i, j)),
        out_shape=jax.ShapeDtypeStruct((n, Z_COLS), BF16),
        scratch_shapes=[pltpu.VMEM((tm, d), BF16)],
        compiler_params=_params("arbitrary", "arbitrary"),
    )(xx, mod, gain.reshape(1, d), w)


def _swap_halves(x, half):
    if 2 * half == LANES:
        return pltpu.roll(x, half, 1)
    lane = lax.broadcasted_iota(jnp.int32, x.shape, 1)
    return jnp.where(lane % (2 * half) < half,
                     pltpu.roll(x, LANES - half, 1), pltpu.roll(x, half, 1))


def _rope(x, c, s, half):
    return x * c + _swap_halves(x, half) * s


def _prep_kernel(rq_ref, rk_ref, dq_ref, dk_ref, dv_ref, gq_ref, gk_ref, gv_ref,
                 sc_ref, ss_ref, ac_ref, as_ref, bc_ref, bs_ref, qg_ref, kg_ref,
                 orq_ref, ork_ref, odq_ref, odk_ref, odv_ref, ogq_ref, ogk_ref, ogv_ref):
    tp = rq_ref.shape[0]
    lane = lax.broadcasted_iota(jnp.int32, (tp, LANES), 1)
    low = lane < KEY_W
    sc, ss = sc_ref[...], ss_ref[...]
    ac, as_ = ac_ref[...], as_ref[...]
    bc, bs = bc_ref[...], bs_ref[...]
    ones = jnp.ones((tp, LANES), BF16)

    k_scale = KEY_W ** -0.5
    for p in range(HEADS // 2):
        sl = slice(p * LANES, (p + 1) * LANES)
        for src, dst, mul in ((rq_ref, orq_ref, 1.0), (rk_ref, ork_ref, k_scale)):
            y = _rope(src[:, sl].astype(F32), sc, ss, KEY_W // 2) * mul
            dst[:, (2 * p) * LANES:(2 * p + 1) * LANES] = jnp.where(low, y, 0.0).astype(BF16)
            dst[:, (2 * p + 1) * LANES:(2 * p + 2) * LANES] = jnp.where(
                low, pltpu.roll(y, KEY_W, 1), 0.0).astype(BF16)

    d_scale = KEY_W ** -0.5 * LOG2_E
    for h in range(HEADS):
        sl = slice(h * LANES, (h + 1) * LANES)
        q = _rope(dq_ref[:, sl].astype(F32), ac, as_, KEY_W // 2) * d_scale
        odq_ref[0, :, sl] = jnp.where(low, q, 0.0).astype(BF16)
        odq_ref[1, :, sl] = jnp.where(low, 0.0, q).astype(BF16)
        odk_ref[:, sl] = _rope(dk_ref[:, sl].astype(F32), ac, as_, KEY_W // 2).astype(BF16)
        odv_ref[:, (2 * h) * LANES:(2 * h + 1) * LANES] = dv_ref[:, sl]
        odv_ref[:, (2 * h + 1) * LANES:(2 * h + 2) * LANES] = ones

    g_scale = HEAD_W ** -0.5 * LOG2_E

    def normed(x, gain):
        return x * lax.rsqrt(jnp.mean(x * x, axis=-1, keepdims=True) + EPS) * gain

    for h in range(HEADS):
        sl = slice(h * LANES, (h + 1) * LANES)
        q = _rope(normed(gq_ref[:, sl].astype(F32), qg_ref[...]), bc, bs, HEAD_W // 2)
        ogq_ref[:, sl] = (q * g_scale).astype(BF16)
    for h in range(GQA_KV):
        sl = slice(h * LANES, (h + 1) * LANES)
        k = _rope(normed(gk_ref[:, sl].astype(F32), kg_ref[...]), bc, bs, HEAD_W // 2)
        ogk_ref[:, sl] = k.astype(BF16)
        ogv_ref[:, (2 * h) * LANES:(2 * h + 1) * LANES] = gv_ref[:, sl]
        ogv_ref[:, (2 * h + 1) * LANES:(2 * h + 2) * LANES] = ones


def _prep(z, tabs, q_gain, k_gain):
    n = z.shape[0]
    tp = _pick(n, (640, 256))

    def zspec(off, width):
        return pl.BlockSpec((tp, width), lambda i, b=off // width: (i, b))

    tab = pl.BlockSpec((tp, LANES), lambda i: (i, 0))
    vec = pl.BlockSpec((1, LANES), lambda i: (0, 0))

    def ospec(width):
        return pl.BlockSpec((tp, width), lambda i: (i, 0))

    return pl.pallas_call(
        _prep_kernel,
        grid=(n // tp,),
        in_specs=[zspec(Z_RQ, 512), zspec(Z_RK, 512), zspec(Z_DQ, 1024), zspec(Z_DK, 1024),
                  zspec(Z_DV, 1024), zspec(Z_GQ, 1024), zspec(Z_GK, 256), zspec(Z_GV, 256),
                  tab, tab, tab, tab, tab, tab, vec, vec],
        out_specs=[ospec(1024), ospec(1024),
                   pl.BlockSpec((2, tp, 1024), lambda i: (0, i, 0)),
                   ospec(1024), ospec(2048), ospec(1024), ospec(256), ospec(512)],
        out_shape=[jax.ShapeDtypeStruct((n, 1024), BF16), jax.ShapeDtypeStruct((n, 1024), BF16),
                   jax.ShapeDtypeStruct((2, n, 1024), BF16), jax.ShapeDtypeStruct((n, 1024), BF16),
                   jax.ShapeDtypeStruct((n, 2048), BF16), jax.ShapeDtypeStruct((n, 1024), BF16),
                   jax.ShapeDtypeStruct((n, 256), BF16), jax.ShapeDtypeStruct((n, 512), BF16)],
        compiler_params=_params("arbitrary"),
    )(z, z, z, z, z, z, z, z, *tabs, q_gain.reshape(1, LANES), k_gain.reshape(1, LANES))


def _rope_tables(n_lat, n_ctx):
    def pattern(cos, sin):
        reps = LANES // (2 * cos.shape[1])
        c = jnp.tile(jnp.concatenate([cos, cos], axis=1), (1, reps))
        s = jnp.tile(jnp.concatenate([-sin, sin], axis=1), (1, reps))
        c = jnp.concatenate([c, jnp.ones((n_ctx, LANES), F32)], axis=0)
        s = jnp.concatenate([s, jnp.zeros((n_ctx, LANES), F32)], axis=0)
        return c, s

    def axial(head_dim):
        n_rows = n_lat // GRID_W
        rows = jnp.repeat(jnp.arange(n_rows), GRID_W).astype(F32)
        cols = jnp.tile(jnp.arange(GRID_W), n_rows).astype(F32)
        n_freq = head_dim // 4
        freqs = ROPE_THETA ** (-jnp.arange(n_freq, dtype=F32) / n_freq)
        ang = jnp.concatenate([rows[:, None] * freqs, cols[:, None] * freqs], axis=-1)
        return jnp.cos(ang), jnp.sin(ang)

    freqs = 1.0 / (ROPE_THETA ** jnp.linspace(0.0, 1.0, KEY_W // 2, dtype=F32))
    ang = jnp.arange(n_lat, dtype=F32)[:, None] * freqs
    return (*pattern(jnp.cos(ang), jnp.sin(ang)), *pattern(*axial(KEY_W)), *pattern(*axial(HEAD_W)))


def _log_decay(lr_ref, direction, h, shape):
    return -jnp.exp(jnp.full(shape, lr_ref[direction, h], F32))


def _ret_sum_kernel(lr_ref, k_ref, v_ref, kv_ref):
    h = pl.program_id(0)
    j = lax.broadcasted_iota(jnp.int32, (CHUNK, LANES), 0).astype(F32)
    w_f = jnp.exp(_log_decay(lr_ref, 0, h, (CHUNK, LANES)) * (CHUNK - 1 - j))
    w_b = jnp.exp(_log_decay(lr_ref, 1, h, (CHUNK, LANES)) * j)
    for c in range(k_ref.shape[0] // CHUNK):
        rows = slice(c * CHUNK, (c + 1) * CHUNK)
        k = k_ref[rows, :].astype(F32)
        kk = (k * w_f + pltpu.roll(k * w_b, KEY_W, 1)).T.astype(BF16)
        kv_ref[c] = jnp.dot(kk, v_ref[rows, :], preferred_element_type=F32)


def _ret_scan_kernel(lr_ref, kv_ref, st_ref, *, n_lat_chunks):
    h = pl.program_id(0)
    nc = kv_ref.shape[0]
    shape = (KEY_W, LANES)
    g_f = jnp.exp(_log_decay(lr_ref, 0, h, shape) * CHUNK)
    g_b = jnp.exp(_log_decay(lr_ref, 1, h, shape) * CHUNK)

    def fwd(c, s):
        st_ref[c, 0:KEY_W, :] = s.astype(BF16)
        return g_f * s + kv_ref[c, 0:KEY_W, :]

    def bwd(t, s):
        c = nc - 1 - t
        st_ref[c, KEY_W:, :] = s.astype(BF16)
        return g_b * s + kv_ref[c, KEY_W:, :]

    zero = jnp.zeros(shape, F32)
    s = lax.fori_loop(n_lat_chunks, nc, fwd, zero)
    lax.fori_loop(0, n_lat_chunks, fwd, s)
    lax.fori_loop(0, nc, bwd, zero)


def _ret_out_kernel(lr_ref, q_ref, k_ref, v_ref, st_ref, o_ref):
    h = pl.program_id(0)
    i = lax.broadcasted_iota(jnp.int32, (CHUNK, CHUNK), 0)
    j = lax.broadcasted_iota(jnp.int32, (CHUNK, CHUNK), 1)
    rel = (i - j).astype(F32)
    lg_f = _log_decay(lr_ref, 0, h, (CHUNK, CHUNK))
    lg_b = _log_decay(lr_ref, 1, h, (CHUNK, CHUNK))
    decay = jnp.where(i >= j, jnp.exp(lg_f * jnp.maximum(rel, 0.0)),
                      jnp.exp(lg_b * jnp.maximum(-rel, 0.0)))
    pos = i.astype(F32)
    cross_f = jnp.exp(lg_f * (pos + 1.0))
    cross_b = jnp.exp(lg_b * (CHUNK - pos))
    for c in range(q_ref.shape[0] // CHUNK):
        rows = slice(c * CHUNK, (c + 1) * CHUNK)
        q = q_ref[rows, :]
        att = lax.dot_general(q, k_ref[rows, :], (((1,), (1,)), ((), ())),
                              preferred_element_type=F32) * decay
        qf = q.astype(F32)
        qs = (qf * cross_f + pltpu.roll(qf * cross_b, KEY_W, 1)).astype(BF16)
        lhs = jnp.concatenate([att.astype(BF16), qs], axis=1)
        rhs = jnp.concatenate([v_ref[rows, :], st_ref[c]], axis=0)
        o_ref[rows, :] = jnp.dot(lhs, rhs, preferred_element_type=F32)


def _retention(rq, rk, z, log_rate, n_lat):
    n = z.shape[0]
    nc = n // CHUNK
    tr = _pick(n, (1280, 640, 256))
    cpt = tr // CHUNK
    smem = pl.BlockSpec(memory_space=pltpu.SMEM)
    head_rows = pl.BlockSpec((tr, LANES), lambda h, i: (i, h))
    v_rows = pl.BlockSpec((tr, LANES), lambda h, i: (i, Z_RV // LANES + h))
    chunk_mats = pl.BlockSpec((None, cpt, CHUNK, LANES), lambda h, i: (h, i, 0, 0))

    kv = pl.pallas_call(
        _ret_sum_kernel,
        grid=(HEADS, n // tr),
        in_specs=[smem, head_rows, v_rows],
        out_specs=chunk_mats,
        out_shape=jax.ShapeDtypeStruct((HEADS, nc, CHUNK, LANES), F32),
        compiler_params=_params("arbitrary", "arbitrary"),
    )(log_rate, rk, z)

    all_chunks = pl.BlockSpec((None, nc, CHUNK, LANES), lambda h: (h, 0, 0, 0))
    st = pl.pallas_call(
        functools.partial(_ret_scan_kernel, n_lat_chunks=n_lat // CHUNK),
        grid=(HEADS,),
        in_specs=[smem, all_chunks],
        out_specs=all_chunks,
        out_shape=jax.ShapeDtypeStruct((HEADS, nc, CHUNK, LANES), BF16),
        compiler_params=_params("arbitrary"),
    )(log_rate, kv)

    return pl.pallas_call(
        _ret_out_kernel,
        grid=(HEADS, n // tr),
        in_specs=[smem, head_rows, head_rows, v_rows, chunk_mats],
        out_specs=head_rows,
        out_shape=jax.ShapeDtypeStruct((n, HEADS * LANES), F32),
        compiler_params=_params("arbitrary", "arbitrary"),
    )(log_rate, rq, rk, z, st)


def _flash_kernel(q_ref, k_ref, v_ref, *rest, tq, tk, unroll):
    o_ref, s0_ref, s1_ref, m_ref, acc_ref = rest[-5:]
    bufs = (s0_ref, s1_ref)
    nk = k_ref.shape[0] // tk
    n_steps = (q_ref.shape[0] // tq) * nk

    def q_rows(qt):
        return pl.ds(pl.multiple_of(qt * tq, tq), tq)

    def key_rows(c):
        return pl.ds(pl.multiple_of(c * tk, tk), tk)

    def scores(qt, c, dst):
        dst[...] = lax.dot_general(q_ref[q_rows(qt), :], k_ref[key_rows(c), :],
                                   (((1,), (1,)), ((), ())), preferred_element_type=F32)

    def consume(qt, c, src):
        s = src[...]
        m = jnp.where(c == 0, -jnp.inf, m_ref[...])
        m_new = jnp.maximum(m, jnp.max(s, axis=-1, keepdims=True))
        p = jnp.exp2(s - m_new)
        acc = jnp.exp2(m - m_new) * acc_ref[...] + jnp.dot(
            p.astype(BF16), v_ref[key_rows(c), :], preferred_element_type=F32)
        acc_ref[...] = acc
        m_ref[...] = m_new
        o_ref[q_rows(qt), :] = (acc[:, :LANES] / acc[:, LANES:]).astype(o_ref.dtype)

    def step(parity, qt, c):
        wrap = c == nk - 1
        qt_next = jnp.where(wrap, qt + 1, qt)
        c_next = jnp.where(wrap, 0, c + 1)
        scores(qt_next, c_next, bufs[1 - parity])
        consume(qt, c, bufs[parity])
        return qt_next, c_next

    def group(_, carry):
        qt, c = carry
        for u in range(unroll):
            qt, c = step(u % 2, qt, c)
        return qt, c

    m_ref[...] = jnp.full(m_ref.shape, -jnp.inf, F32)
    acc_ref[...] = jnp.zeros(acc_ref.shape, F32)
    zero = jnp.int32(0)
    scores(zero, zero, s0_ref)
    qt, c = lax.fori_loop(0, (n_steps - 1) // unroll, group, (zero, zero))
    for u in range((n_steps - 1) % unroll):
        qt, c = step(u % 2, qt, c)
    consume(qt, c, bufs[(n_steps - 1) % 2])


FLASH_UNROLL = 4


def _flash(q, k, v, out_shape, q_map, kv_map, o_map, grid, tq, q_tiles, n_keys, key_block,
           prev=None):
    tk = _pick(n_keys, (1280, 640, 256))
    q_block = (None,) * (q.ndim - 2) + (tq * q_tiles, LANES)
    o_block = (None,) * (len(out_shape.shape) - 2) + (tq * q_tiles, LANES)
    in_specs = [
        pl.BlockSpec(q_block, q_map),
        pl.BlockSpec((n_keys, LANES), lambda g, r: (key_block, kv_map(g)),
                     pipeline_mode=pl.Buffered(1)),
        pl.BlockSpec((n_keys, 2 * LANES), lambda g, r: (key_block, kv_map(g)),
                     pipeline_mode=pl.Buffered(1)),
    ]
    args = [q, k, v]
    aliases = {}
    if prev is not None:
        in_specs.append(pl.BlockSpec(memory_space=pl.ANY))
        args.append(prev)
        aliases = {3: 0}
    return pl.pallas_call(
        functools.partial(_flash_kernel, tq=tq, tk=tk, unroll=FLASH_UNROLL),
        grid=grid,
        in_specs=in_specs,
        out_specs=pl.BlockSpec(o_block, o_map),
        out_shape=out_shape,
        scratch_shapes=[pltpu.VMEM((tq, tk), F32), pltpu.VMEM((tq, tk), F32),
                        pltpu.VMEM((tq, 1), F32), pltpu.VMEM((tq, 2 * LANES), F32)],
        input_output_aliases=aliases,
        compiler_params=_params("arbitrary", "arbitrary"),
    )(*args)


def _query_tiling(n_lat):
    tq = _pick(n_lat, (1024, 512, 256))
    q_tiles = _pick(n_lat // tq, (4, 2, 1))
    return tq, q_tiles, n_lat // (tq * q_tiles)


def _diff_attention(dq, dk, dv, n_lat, with_ctx):
    n = dk.shape[0]
    n_ctx = n - n_lat
    tq, q_tiles, nb = _query_tiling(n_lat)
    shape = jax.ShapeDtypeStruct((2, n, HEADS * LANES), F32)
    o = _flash(dq, dk, dv, shape,
               lambda g, r: (r // nb, r % nb, g), lambda g: g, lambda g, r: (r // nb, r % nb, g),
               (HEADS, 2 * nb), tq, q_tiles, n, 0)
    if with_ctx:
        cb = n_lat // n_ctx
        o = _flash(dq, dk, dv, shape,
                   lambda g, r: (r, cb, g), lambda g: g, lambda g, r: (r, cb, g),
                   (HEADS, 2), n_ctx, 1, n_ctx, cb, prev=o)
    return o


def _gqa_attention(gq, gk, gv, n_lat, with_ctx):
    n = gk.shape[0]
    n_ctx = n - n_lat
    tq, q_tiles, nb = _query_tiling(n_lat)
    shape = jax.ShapeDtypeStruct((n, HEADS * LANES), BF16)
    o = _flash(gq, gk, gv, shape,
               lambda g, r: (r % nb, g * GQA_GROUP + r // nb), lambda g: g,
               lambda g, r: (r % nb, g * GQA_GROUP + r // nb),
               (GQA_KV, GQA_GROUP * nb), tq, q_tiles, n, 0)
    if with_ctx:
        cb = n_lat // n_ctx
        o = _flash(gq, gk, gv, shape,
                   lambda g, r: (cb, g * GQA_GROUP + r), lambda g: g,
                   lambda g, r: (cb, g * GQA_GROUP + r),
                   (GQA_KV, GQA_GROUP), n_ctx, 1, n_ctx, cb, prev=o)
    return o


def _finish_kernel(lam_ref, x_ref, mod_ref, gates_ref, rg_ref, dg_ref, gg_ref, ro_ref, do_ref,
                   go_ref, sub_ref, wb_ref, wo_ref, fin_ref, o_ref, *, n_lat, tm, d, lambda_init,
                   final_norm):
    lp = lam_ref[...]
    lam = (jnp.exp(jnp.sum(lp[0:1] * lp[1:2], axis=-1, keepdims=True))
           - jnp.exp(jnp.sum(lp[2:3] * lp[3:4], axis=-1, keepdims=True)) + lambda_init)

    def head_norm(o):
        return o * lax.rsqrt(jnp.mean(o * o, axis=-1, keepdims=True) + EPS)

    branches = []
    for h in range(HEADS):
        sl = slice(h * LANES, (h + 1) * LANES)
        r = head_norm(ro_ref[:, sl]) * _silu(rg_ref[:, sl].astype(F32))
        dd = head_norm(do_ref[0, :, sl] - lam * do_ref[1, :, sl]) * sub_ref[...] * (1.0 - lambda_init)
        dd = dd * _silu(dg_ref[:, sl].astype(F32))
        g = go_ref[:, sl].astype(F32) * _silu(gg_ref[:, sl].astype(F32))
        branches.append((r.astype(BF16), dd.astype(BF16), g.astype(BF16)))

    merged = jnp.zeros((tm, d), F32)
    for b in range(3):
        br = jnp.concatenate([branches[h][b] for h in range(HEADS)], axis=1)
        y = jnp.dot(br, wb_ref[b], preferred_element_type=F32)
        merged = merged + _sigmoid(gates_ref[:, b * d:(b + 1) * d].astype(F32)) * y
    out = jnp.dot(merged.astype(BF16), wo_ref[...], preferred_element_type=F32)

    rows = pl.program_id(0) * tm + lax.broadcasted_iota(jnp.int32, (tm, 1), 0)
    gate = _row_mod(mod_ref, 2 * d, 3 * d, rows >= n_lat)
    x = x_ref[...] + gate * out
    if final_norm:
        x = x * lax.rsqrt(jnp.mean(x * x, axis=-1, keepdims=True) + EPS) * fin_ref[...]
    o_ref[...] = x


def _finish(xx, z, mod, lam_params, ro, do, go, subln, wb, wo, fin_gain, n_lat, lambda_init, last):
    n, d = xx.shape
    tm = 256
    n_rows = n_lat if last else n
    bw = HEADS * LANES

    def rows(width, off=0):
        return pl.BlockSpec((tm, width), lambda i, b=off // width: (i, b))

    const2 = lambda i: (0, 0)
    kern = functools.partial(_finish_kernel, n_lat=n_lat, tm=tm, d=d, lambda_init=lambda_init,
                             final_norm=last)
    return pl.pallas_call(
        kern,
        grid=(n_rows // tm,),
        in_specs=[
            pl.BlockSpec((4, KEY_W), const2),
            rows(d),
            pl.BlockSpec((8, 3 * d), const2),
            rows(3 * d, Z_GATES), rows(bw, Z_RG), rows(bw, Z_DG), rows(bw, Z_GG),
            rows(bw),
            pl.BlockSpec((2, tm, bw), lambda i: (0, i, 0)),
            rows(bw),
            pl.BlockSpec((1, LANES), const2),
            pl.BlockSpec((3, bw, d), lambda i: (0, 0, 0), pipeline_mode=pl.Buffered(1)),
            pl.BlockSpec((d, d), const2, pipeline_mode=pl.Buffered(1)),
            pl.BlockSpec((1, d), const2),
        ],
        out_specs=rows(d),
        out_shape=jax.ShapeDtypeStruct((n_rows, d), F32),
        compiler_params=_params("arbitrary"),
    )(lam_params, xx, mod, z, z, z, z, ro, do, go, subln.reshape(1, LANES), wb, wo,
      fin_gain.reshape(1, d))


def kernel(x, c, ctx, c_ctx, norm_gain, w_ada, b_ada, w_in, ret_log_rate, diff_lambda,
           diff_subln_gain, gqa_q_gain, gqa_k_gain, w_branch, w_out, final_norm_gain):
    _, n_lat, d = x.shape
    n_ctx = ctx.shape[1]
    depth = w_in.shape[0]
    assert x.shape[0] == 1 and d == 2048 and w_in.shape[2] == Z_COLS
    assert n_lat % n_ctx == 0 and n_ctx % CHUNK == 0 and n_lat % GRID_W == 0

    xx = jnp.concatenate([x[0], ctx[0]], axis=0)
    c8 = jnp.concatenate([c, c_ctx[None], jnp.zeros((6, d), F32)], axis=0)
    mods = _ada_all(c8, w_ada, b_ada)
    tabs = _rope_tables(n_lat, n_ctx)
    w_in16 = w_in.astype(BF16)
    wb16 = w_branch.astype(BF16)
    wo16 = w_out.astype(BF16)

    for l in range(depth):
        last = l == depth - 1
        lambda_init = 0.8 - 0.6 * math.exp(-0.3 * l)
        z = _in_proj(xx, mods[l], norm_gain[l], w_in16[l], n_lat)
        rq, rk, dq, dk, dv, gq, gk, gv = _prep(z, tabs, gqa_q_gain[l], gqa_k_gain[l])
        ro = _retention(rq, rk, z, ret_log_rate[l], n_lat)
        do = _diff_attention(dq, dk, dv, n_lat, not last)
        go = _gqa_attention(gq, gk, gv, n_lat, not last)
        xx = _finish(xx, z, mods[l], diff_lambda[l], ro, do, go, diff_subln_gain[l], wb16[l],
                     wo16[l], final_norm_gain, n_lat, lambda_init, last)
    return xx[None]
```

```python
import functools
import math

import jax
import jax.numpy as jnp
from jax import lax
from jax.experimental import pallas as pl
from jax.experimental.pallas import tpu as pltpu

F32 = jnp.float32
BF16 = jnp.bfloat16

EPS = 1e-6
ROPE_THETA = 10000.0
GRID_W = 64
LANES = 128
VMEM_LIMIT = 56 * 1024 * 1024

HEADS = 8
HEAD_W = 128
KEY_W = 64
GQA_KV = 2
GQA_GROUP = 4
CHUNK = 128

Z_GATES, Z_RQ, Z_RK, Z_RV, Z_RG = 0, 6144, 6656, 7168, 8192
Z_DQ, Z_DK, Z_DV, Z_DG = 9216, 10240, 11264, 12288
Z_GQ, Z_GG, Z_GK, Z_GV = 13312, 14336, 15360, 15616
Z_COLS = 15872
W_TILE = 512
NORM_ROWS = 128
LOG2_E = math.log2(math.e)


def _pick(n, candidates):
    for c in candidates:
        if n % c == 0:
            return c
    raise ValueError(f"no tile in {candidates} divides {n}")


def _params(*sem):
    return pltpu.CompilerParams(dimension_semantics=sem, vmem_limit_bytes=VMEM_LIMIT)


def _sigmoid(x):
    return 0.5 * jnp.tanh(0.5 * x) + 0.5


def _silu(x):
    return x * _sigmoid(x)


def _ada_kernel(c_ref, w_ref, b_ref, o_ref):
    s = _silu(c_ref[...])
    o_ref[...] = jnp.dot(s, w_ref[...], preferred_element_type=F32,
                         precision=lax.Precision.HIGHEST) + b_ref[...]


def _ada_all(c8, w_ada, b_ada):
    depth, d, d3 = w_ada.shape
    tn = 1024
    return pl.pallas_call(
        _ada_kernel,
        grid=(depth, d3 // tn),
        in_specs=[
            pl.BlockSpec((8, d), lambda l, j: (0, 0)),
            pl.BlockSpec((None, d, tn), lambda l, j: (l, 0, j)),
            pl.BlockSpec((None, 1, tn), lambda l, j: (l, 0, j)),
        ],
        out_specs=pl.BlockSpec((None, 8, tn), lambda l, j: (l, 0, j)),
        out_shape=jax.ShapeDtypeStruct((depth, 8, d3), F32),
        compiler_params=_params("arbitrary", "arbitrary"),
    )(c8, w_ada, b_ada.reshape(depth, 1, d3))


def _row_mod(mod_ref, lo, hi, is_ctx):
    return jnp.where(is_ctx, mod_ref[1:2, lo:hi], mod_ref[0:1, lo:hi])


def _inproj_kernel(x_ref, mod_ref, g_ref, w_ref, z_ref, h_ref, *, n_lat, tm, d):
    i = pl.program_id(0)

    @pl.when(pl.program_id(1) == 0)
    def _():
        def norm_rows(r, carry):
            sl = pl.ds(pl.multiple_of(r * NORM_ROWS, NORM_ROWS), NORM_ROWS)
            x = x_ref[sl, :]
            y = x * lax.rsqrt(jnp.mean(x * x, axis=-1, keepdims=True) + EPS) * g_ref[...]
            rows = i * tm + r * NORM_ROWS + lax.broadcasted_iota(jnp.int32, (NORM_ROWS, 1), 0)
            is_ctx = rows >= n_lat
            shift = _row_mod(mod_ref, 0, d, is_ctx)
            scale = _row_mod(mod_ref, d, 2 * d, is_ctx)
            h_ref[sl, :] = (y * (1.0 + scale) + shift).astype(BF16)
            return carry

        lax.fori_loop(0, tm // NORM_ROWS, norm_rows, 0)

    z_ref[...] = jnp.dot(h_ref[...], w_ref[...], preferred_element_type=F32).astype(BF16)


def _w_block(j):
    return jnp.where(j < 12, j + 19, jnp.where(j < 28, j - 12, jnp.where(j < 30, j - 11, 16)))


def _in_proj(xx, mods, gain, w, layer, n_lat):
    n, d = xx.shape
    tm = _pick(n, (1280, 640, 256))
    kern = functools.partial(_inproj_kernel, n_lat=n_lat, tm=tm, d=d)
    return pl.pallas_call(
        kern,
        grid=(n // tm, Z_COLS // W_TILE),
        in_specs=[
            pl.BlockSpec((tm, d), lambda i, j: (i, 0)),
            pl.BlockSpec((None, 8, 3 * d), lambda i, j: (layer, 0, 0)),
            pl.BlockSpec((1, d), lambda i, j: (0, 0)),
            pl.BlockSpec((None, d, W_TILE), lambda i, j: (layer, 0, _w_block(j))),
        ],
        out_specs=pl.BlockSpec((tm, W_TILE), lambda i, j: (i, j)),
        out_shape=jax.ShapeDtypeStruct((n, Z_COLS), BF16),
        scratch_shapes=[pltpu.VMEM((tm, d), BF16)],
        compiler_params=_params("arbitrary", "arbitrary"),
    )(xx, mods, gain.reshape(1, d), w)


def _swap_halves(x, half):
    if 2 * half == LANES:
        return pltpu.roll(x, half, 1)
    lane = lax.broadcasted_iota(jnp.int32, x.shape, 1)
    return jnp.where(lane % (2 * half) < half,
                     pltpu.roll(x, LANES - half, 1), pltpu.roll(x, half, 1))


def _rope(x, c, s, half):
    return x * c + _swap_halves(x, half) * s


def _prep_kernel(rq_ref, rk_ref, dq_ref, dk_ref, dv_ref, gq_ref, gk_ref, gv_ref,
                 sc_ref, ss_ref, ac_ref, as_ref, bc_ref, bs_ref, qg_ref, kg_ref,
                 orq_ref, ork_ref, odq_ref, odk_ref, odv_ref, ogq_ref, ogk_ref, ogv_ref):
    tp = rq_ref.shape[0]
    lane = lax.broadcasted_iota(jnp.int32, (tp, LANES), 1)
    low = lane < KEY_W
    sc, ss = sc_ref[...], ss_ref[...]
    ac, as_ = ac_ref[...], as_ref[...]
    bc, bs = bc_ref[...], bs_ref[...]
    ones = jnp.ones((tp, LANES), BF16)

    k_scale = KEY_W ** -0.5
    for p in range(HEADS // 2):
        sl = slice(p * LANES, (p + 1) * LANES)
        for src, dst, mul in ((rq_ref, orq_ref, 1.0), (rk_ref, ork_ref, k_scale)):
            y = _rope(src[:, sl].astype(F32), sc, ss, KEY_W // 2) * mul
            dst[:, (2 * p) * LANES:(2 * p + 1) * LANES] = jnp.where(low, y, 0.0).astype(BF16)
            dst[:, (2 * p + 1) * LANES:(2 * p + 2) * LANES] = jnp.where(
                low, pltpu.roll(y, KEY_W, 1), 0.0).astype(BF16)

    d_scale = KEY_W ** -0.5 * LOG2_E
    for h in range(HEADS):
        sl = slice(h * LANES, (h + 1) * LANES)
        q = _rope(dq_ref[:, sl].astype(F32), ac, as_, KEY_W // 2) * d_scale
        odq_ref[0, :, sl] = jnp.where(low, q, 0.0).astype(BF16)
        odq_ref[1, :, sl] = jnp.where(low, 0.0, q).astype(BF16)
        odk_ref[:, sl] = _rope(dk_ref[:, sl].astype(F32), ac, as_, KEY_W // 2).astype(BF16)
        odv_ref[:, (2 * h) * LANES:(2 * h + 1) * LANES] = dv_ref[:, sl]
        odv_ref[:, (2 * h + 1) * LANES:(2 * h + 2) * LANES] = ones

    g_scale = HEAD_W ** -0.5 * LOG2_E

    def normed(x, gain):
        return x * lax.rsqrt(jnp.mean(x * x, axis=-1, keepdims=True) + EPS) * gain

    for h in range(HEADS):
        sl = slice(h * LANES, (h + 1) * LANES)
        q = _rope(normed(gq_ref[:, sl].astype(F32), qg_ref[...]), bc, bs, HEAD_W // 2)
        ogq_ref[:, sl] = (q * g_scale).astype(BF16)
    for h in range(GQA_KV):
        sl = slice(h * LANES, (h + 1) * LANES)
        k = _rope(normed(gk_ref[:, sl].astype(F32), kg_ref[...]), bc, bs, HEAD_W // 2)
        ogk_ref[:, sl] = k.astype(BF16)
        ogv_ref[:, (2 * h) * LANES:(2 * h + 1) * LANES] = gv_ref[:, sl]
        ogv_ref[:, (2 * h + 1) * LANES:(2 * h + 2) * LANES] = ones


def _prep(z, tabs, q_gain, k_gain):
    n = z.shape[0]
    tp = _pick(n, (640, 256))

    def zspec(off, width):
        return pl.BlockSpec((tp, width), lambda i, b=off // width: (i, b))

    tab = pl.BlockSpec((tp, LANES), lambda i: (i, 0))
    vec = pl.BlockSpec((1, LANES), lambda i: (0, 0))

    def ospec(width):
        return pl.BlockSpec((tp, width), lambda i: (i, 0))

    return pl.pallas_call(
        _prep_kernel,
        grid=(n // tp,),
        in_specs=[zspec(Z_RQ, 512), zspec(Z_RK, 512), zspec(Z_DQ, 1024), zspec(Z_DK, 1024),
                  zspec(Z_DV, 1024), zspec(Z_GQ, 1024), zspec(Z_GK, 256), zspec(Z_GV, 256),
                  tab, tab, tab, tab, tab, tab, vec, vec],
        out_specs=[ospec(1024), ospec(1024),
                   pl.BlockSpec((2, tp, 1024), lambda i: (0, i, 0)),
                   ospec(1024), ospec(2048), ospec(1024), ospec(256), ospec(512)],
        out_shape=[jax.ShapeDtypeStruct((n, 1024), BF16), jax.ShapeDtypeStruct((n, 1024), BF16),
                   jax.ShapeDtypeStruct((2, n, 1024), BF16), jax.ShapeDtypeStruct((n, 1024), BF16),
                   jax.ShapeDtypeStruct((n, 2048), BF16), jax.ShapeDtypeStruct((n, 1024), BF16),
                   jax.ShapeDtypeStruct((n, 256), BF16), jax.ShapeDtypeStruct((n, 512), BF16)],
        compiler_params=_params("arbitrary"),
    )(z, z, z, z, z, z, z, z, *tabs, q_gain.reshape(1, LANES), k_gain.reshape(1, LANES))


def _rope_tables(n_lat, n_ctx):
    def pattern(cos, sin):
        reps = LANES // (2 * cos.shape[1])
        c = jnp.tile(jnp.concatenate([cos, cos], axis=1), (1, reps))
        s = jnp.tile(jnp.concatenate([-sin, sin], axis=1), (1, reps))
        c = jnp.concatenate([c, jnp.ones((n_ctx, LANES), F32)], axis=0)
        s = jnp.concatenate([s, jnp.zeros((n_ctx, LANES), F32)], axis=0)
        return c, s

    def axial(head_dim):
        n_rows = n_lat // GRID_W
        rows = jnp.repeat(jnp.arange(n_rows), GRID_W).astype(F32)
        cols = jnp.tile(jnp.arange(GRID_W), n_rows).astype(F32)
        n_freq = head_dim // 4
        freqs = ROPE_THETA ** (-jnp.arange(n_freq, dtype=F32) / n_freq)
        ang = jnp.concatenate([rows[:, None] * freqs, cols[:, None] * freqs], axis=-1)
        return jnp.cos(ang), jnp.sin(ang)

    freqs = 1.0 / (ROPE_THETA ** jnp.linspace(0.0, 1.0, KEY_W // 2, dtype=F32))
    ang = jnp.arange(n_lat, dtype=F32)[:, None] * freqs
    return (*pattern(jnp.cos(ang), jnp.sin(ang)), *pattern(*axial(KEY_W)), *pattern(*axial(HEAD_W)))


def _log_decay(lr_ref, direction, h, shape):
    return -jnp.exp(jnp.full(shape, lr_ref[direction, h], F32))


def _ret_sum_kernel(lr_ref, k_ref, v_ref, kv_ref):
    h = pl.program_id(0)
    j = lax.broadcasted_iota(jnp.int32, (CHUNK, LANES), 0).astype(F32)
    w_f = jnp.exp(_log_decay(lr_ref, 0, h, (CHUNK, LANES)) * (CHUNK - 1 - j))
    w_b = jnp.exp(_log_decay(lr_ref, 1, h, (CHUNK, LANES)) * j)
    for c in range(k_ref.shape[0] // CHUNK):
        rows = slice(c * CHUNK, (c + 1) * CHUNK)
        k = k_ref[rows, :].astype(F32)
        kk = (k * w_f + pltpu.roll(k * w_b, KEY_W, 1)).T.astype(BF16)
        kv_ref[c] = jnp.dot(kk, v_ref[rows, :], preferred_element_type=F32)


def _ret_scan_kernel(lr_ref, kv_ref, st_ref, *, n_lat_chunks):
    h = pl.program_id(0)
    nc = kv_ref.shape[0]
    shape = (KEY_W, LANES)
    g_f = jnp.exp(_log_decay(lr_ref, 0, h, shape) * CHUNK)
    g_b = jnp.exp(_log_decay(lr_ref, 1, h, shape) * CHUNK)

    def fwd(c, s):
        st_ref[c, 0:KEY_W, :] = s.astype(BF16)
        return g_f * s + kv_ref[c, 0:KEY_W, :]

    def bwd(t, s):
        c = nc - 1 - t
        st_ref[c, KEY_W:, :] = s.astype(BF16)
        return g_b * s + kv_ref[c, KEY_W:, :]

    zero = jnp.zeros(shape, F32)
    s = lax.fori_loop(n_lat_chunks, nc, fwd, zero)
    lax.fori_loop(0, n_lat_chunks, fwd, s)
    lax.fori_loop(0, nc, bwd, zero)


def _ret_out_kernel(lr_ref, q_ref, k_ref, v_ref, st_ref, o_ref):
    h = pl.program_id(0)
    i = lax.broadcasted_iota(jnp.int32, (CHUNK, CHUNK), 0)
    j = lax.broadcasted_iota(jnp.int32, (CHUNK, CHUNK), 1)
    rel = (i - j).astype(F32)
    lg_f = _log_decay(lr_ref, 0, h, (CHUNK, CHUNK))
    lg_b = _log_decay(lr_ref, 1, h, (CHUNK, CHUNK))
    decay = jnp.where(i >= j, jnp.exp(lg_f * jnp.maximum(rel, 0.0)),
                      jnp.exp(lg_b * jnp.maximum(-rel, 0.0)))
    pos = i.astype(F32)
    cross_f = jnp.exp(lg_f * (pos + 1.0))
    cross_b = jnp.exp(lg_b * (CHUNK - pos))
    for c in range(q_ref.shape[0] // CHUNK):
        rows = slice(c * CHUNK, (c + 1) * CHUNK)
        q = q_ref[rows, :]
        att = lax.dot_general(q, k_ref[rows, :], (((1,), (1,)), ((), ())),
                              preferred_element_type=F32) * decay
        qf = q.astype(F32)
        qs = (qf * cross_f + pltpu.roll(qf * cross_b, KEY_W, 1)).astype(BF16)
        lhs = jnp.concatenate([att.astype(BF16), qs], axis=1)
        rhs = jnp.concatenate([v_ref[rows, :], st_ref[c]], axis=0)
        o_ref[rows, :] = jnp.dot(lhs, rhs, preferred_element_type=F32)


def _retention(rq, rk, z, log_rate, n_lat):
    n = z.shape[0]
    nc = n // CHUNK
    tr = _pick(n, (1280, 640, 256))
    cpt = tr // CHUNK
    smem = pl.BlockSpec(memory_space=pltpu.SMEM)
    head_rows = pl.BlockSpec((tr, LANES), lambda h, i: (i, h))
    v_rows = pl.BlockSpec((tr, LANES), lambda h, i: (i, Z_RV // LANES + h))
    chunk_mats = pl.BlockSpec((None, cpt, CHUNK, LANES), lambda h, i: (h, i, 0, 0))

    kv = pl.pallas_call(
        _ret_sum_kernel,
        grid=(HEADS, n // tr),
        in_specs=[smem, head_rows, v_rows],
        out_specs=chunk_mats,
        out_shape=jax.ShapeDtypeStruct((HEADS, nc, CHUNK, LANES), F32),
        compiler_params=_params("arbitrary", "arbitrary"),
    )(log_rate, rk, z)

    all_chunks = pl.BlockSpec((None, nc, CHUNK, LANES), lambda h: (h, 0, 0, 0))
    st = pl.pallas_call(
        functools.partial(_ret_scan_kernel, n_lat_chunks=n_lat // CHUNK),
        grid=(HEADS,),
        in_specs=[smem, all_chunks],
        out_specs=all_chunks,
        out_shape=jax.ShapeDtypeStruct((HEADS, nc, CHUNK, LANES), BF16),
        compiler_params=_params("arbitrary"),
    )(log_rate, kv)

    return pl.pallas_call(
        _ret_out_kernel,
        grid=(HEADS, n // tr),
        in_specs=[smem, head_rows, head_rows, v_rows, chunk_mats],
        out_specs=head_rows,
        out_shape=jax.ShapeDtypeStruct((n, HEADS * LANES), F32),
        compiler_params=_params("arbitrary", "arbitrary"),
    )(log_rate, rq, rk, z, st)


def _flash_kernel(q_ref, k_ref, v_ref, *rest, tq, tk, unroll):
    o_ref, s0_ref, s1_ref, m_ref, acc_ref = rest[-5:]
    bufs = (s0_ref, s1_ref)
    nk = k_ref.shape[0] // tk
    n_steps = (q_ref.shape[0] // tq) * nk

    def q_rows(qt):
        return pl.ds(pl.multiple_of(qt * tq, tq), tq)

    def key_rows(c):
        return pl.ds(pl.multiple_of(c * tk, tk), tk)

    def scores(qt, c, dst):
        dst[...] = lax.dot_general(q_ref[q_rows(qt), :], k_ref[key_rows(c), :],
                                   (((1,), (1,)), ((), ())), preferred_element_type=F32)

    def consume(qt, c, src):
        s = src[...]
        m = jnp.where(c == 0, -jnp.inf, m_ref[...])
        m_new = jnp.maximum(m, jnp.max(s, axis=-1, keepdims=True))
        p = jnp.exp2(s - m_new)
        acc = jnp.exp2(m - m_new) * acc_ref[...] + jnp.dot(
            p.astype(BF16), v_ref[key_rows(c), :], preferred_element_type=F32)
        acc_ref[...] = acc
        m_ref[...] = m_new
        o_ref[q_rows(qt), :] = (acc[:, :LANES] / acc[:, LANES:]).astype(o_ref.dtype)

    def step(parity, qt, c):
        wrap = c == nk - 1
        qt_next = jnp.where(wrap, qt + 1, qt)
        c_next = jnp.where(wrap, 0, c + 1)
        scores(qt_next, c_next, bufs[1 - parity])
        consume(qt, c, bufs[parity])
        return qt_next, c_next

    def group(_, carry):
        qt, c = carry
        for u in range(unroll):
            qt, c = step(u % 2, qt, c)
        return qt, c

    m_ref[...] = jnp.full(m_ref.shape, -jnp.inf, F32)
    acc_ref[...] = jnp.zeros(acc_ref.shape, F32)
    zero = jnp.int32(0)
    scores(zero, zero, s0_ref)
    qt, c = lax.fori_loop(0, (n_steps - 1) // unroll, group, (zero, zero))
    for u in range((n_steps - 1) % unroll):
        qt, c = step(u % 2, qt, c)
    consume(qt, c, bufs[(n_steps - 1) % 2])


FLASH_UNROLL = 6


def _flash(q, k, v, out_shape, q_map, kv_map, o_map, grid, tq, q_tiles, n_keys, key_block,
           prev=None):
    tk = _pick(n_keys, (1280, 640, 256))
    q_block = (None,) * (q.ndim - 2) + (tq * q_tiles, LANES)
    o_block = (None,) * (len(out_shape.shape) - 2) + (tq * q_tiles, LANES)
    in_specs = [
        pl.BlockSpec(q_block, q_map),
        pl.BlockSpec((n_keys, LANES), lambda g, r: (key_block, kv_map(g)),
                     pipeline_mode=pl.Buffered(1)),
        pl.BlockSpec((n_keys, 2 * LANES), lambda g, r: (key_block, kv_map(g)),
                     pipeline_mode=pl.Buffered(1)),
    ]
    args = [q, k, v]
    aliases = {}
    if prev is not None:
        in_specs.append(pl.BlockSpec(memory_space=pl.ANY))
        args.append(prev)
        aliases = {3: 0}
    return pl.pallas_call(
        functools.partial(_flash_kernel, tq=tq, tk=tk, unroll=FLASH_UNROLL),
        grid=grid,
        in_specs=in_specs,
        out_specs=pl.BlockSpec(o_block, o_map),
        out_shape=out_shape,
        scratch_shapes=[pltpu.VMEM((tq, tk), F32), pltpu.VMEM((tq, tk), F32),
                        pltpu.VMEM((tq, 1), F32), pltpu.VMEM((tq, 2 * LANES), F32)],
        input_output_aliases=aliases,
        compiler_params=_params("arbitrary", "arbitrary"),
    )(*args)


def _query_tiling(n_lat):
    tq = _pick(n_lat, (1024, 512, 256))
    q_tiles = _pick(n_lat // tq, (8, 4, 2, 1))
    return tq, q_tiles, n_lat // (tq * q_tiles)


def _diff_attention(dq, dk, dv, n_lat, with_ctx):
    n = dk.shape[0]
    n_ctx = n - n_lat
    tq, q_tiles, nb = _query_tiling(n_lat)
    shape = jax.ShapeDtypeStruct((2, n, HEADS * LANES), F32)
    o = _flash(dq, dk, dv, shape,
               lambda g, r: (r // nb, r % nb, g), lambda g: g, lambda g, r: (r // nb, r % nb, g),
               (HEADS, 2 * nb), tq, q_tiles, n, 0)
    if with_ctx:
        cb = n_lat // n_ctx
        o = _flash(dq, dk, dv, shape,
                   lambda g, r: (r, cb, g), lambda g: g, lambda g, r: (r, cb, g),
                   (HEADS, 2), n_ctx, 1, n_ctx, cb, prev=o)
    return o


def _gqa_attention(gq, gk, gv, n_lat, with_ctx):
    n = gk.shape[0]
    n_ctx = n - n_lat
    tq, q_tiles, nb = _query_tiling(n_lat)
    shape = jax.ShapeDtypeStruct((n, HEADS * LANES), BF16)
    o = _flash(gq, gk, gv, shape,
               lambda g, r: (r % nb, g * GQA_GROUP + r // nb), lambda g: g,
               lambda g, r: (r % nb, g * GQA_GROUP + r // nb),
               (GQA_KV, GQA_GROUP * nb), tq, q_tiles, n, 0)
    if with_ctx:
        cb = n_lat // n_ctx
        o = _flash(gq, gk, gv, shape,
                   lambda g, r: (cb, g * GQA_GROUP + r), lambda g: g,
                   lambda g, r: (cb, g * GQA_GROUP + r),
                   (GQA_KV, GQA_GROUP), n_ctx, 1, n_ctx, cb, prev=o)
    return o


def _finish_kernel(lam_ref, x_ref, mod_ref, gates_ref, rg_ref, dg_ref, gg_ref, ro_ref, do_ref,
                   go_ref, sub_ref, wb_ref, wo_ref, fin_ref, o_ref, *, n_lat, tm, d, lambda_init,
                   final_norm):
    lp = lam_ref[...]
    lam = (jnp.exp(jnp.sum(lp[0:1] * lp[1:2], axis=-1, keepdims=True))
           - jnp.exp(jnp.sum(lp[2:3] * lp[3:4], axis=-1, keepdims=True)) + lambda_init)

    def head_norm(o):
        return o * lax.rsqrt(jnp.mean(o * o, axis=-1, keepdims=True) + EPS)

    branches = []
    for h in range(HEADS):
        sl = slice(h * LANES, (h + 1) * LANES)
        r = head_norm(ro_ref[:, sl]) * _silu(rg_ref[:, sl].astype(F32))
        dd = head_norm(do_ref[0, :, sl] - lam * do_ref[1, :, sl]) * sub_ref[...] * (1.0 - lambda_init)
        dd = dd * _silu(dg_ref[:, sl].astype(F32))
        g = go_ref[:, sl].astype(F32) * _silu(gg_ref[:, sl].astype(F32))
        branches.append((r.astype(BF16), dd.astype(BF16), g.astype(BF16)))

    merged = jnp.zeros((tm, d), F32)
    for b in range(3):
        br = jnp.concatenate([branches[h][b] for h in range(HEADS)], axis=1)
        y = jnp.dot(br, wb_ref[b], preferred_element_type=F32)
        merged = merged + _sigmoid(gates_ref[:, b * d:(b + 1) * d].astype(F32)) * y
    out = jnp.dot(merged.astype(BF16), wo_ref[...], preferred_element_type=F32)

    rows = pl.program_id(0) * tm + lax.broadcasted_iota(jnp.int32, (tm, 1), 0)
    gate = _row_mod(mod_ref, 2 * d, 3 * d, rows >= n_lat)
    x = x_ref[...] + gate * out
    if final_norm:
        x = x * lax.rsqrt(jnp.mean(x * x, axis=-1, keepdims=True) + EPS) * fin_ref[...]
    o_ref[...] = x


def _finish(xx, z, mods, lam_params, ro, do, go, subln, wb, wo, fin_gain, layer, n_lat,
            lambda_init, last):
    n, d = xx.shape
    tm = 256
    n_rows = n_lat if last else n
    bw = HEADS * LANES

    def rows(width, off=0):
        return pl.BlockSpec((tm, width), lambda i, b=off // width: (i, b))

    const2 = lambda i: (0, 0)
    kern = functools.partial(_finish_kernel, n_lat=n_lat, tm=tm, d=d, lambda_init=lambda_init,
                             final_norm=last)
    return pl.pallas_call(
        kern,
        grid=(n_rows // tm,),
        in_specs=[
            pl.BlockSpec((4, KEY_W), const2),
            rows(d),
            pl.BlockSpec((None, 8, 3 * d), lambda i: (layer, 0, 0)),
            rows(3 * d, Z_GATES), rows(bw, Z_RG), rows(bw, Z_DG), rows(bw, Z_GG),
            rows(bw),
            pl.BlockSpec((2, tm, bw), lambda i: (0, i, 0)),
            rows(bw),
            pl.BlockSpec((1, LANES), const2),
            pl.BlockSpec((None, 3, bw, d), lambda i: (layer, 0, 0, 0),
                         pipeline_mode=pl.Buffered(1)),
            pl.BlockSpec((None, d, d), lambda i: (layer, 0, 0), pipeline_mode=pl.Buffered(1)),
            pl.BlockSpec((1, d), const2),
        ],
        out_specs=rows(d),
        out_shape=jax.ShapeDtypeStruct((n_rows, d), F32),
        compiler_params=_params("arbitrary"),
    )(lam_params, xx, mods, z, z, z, z, ro, do, go, subln.reshape(1, LANES), wb, wo,
      fin_gain.reshape(1, d))


def kernel(x, c, ctx, c_ctx, norm_gain, w_ada, b_ada, w_in, ret_log_rate, diff_lambda,
           diff_subln_gain, gqa_q_gain, gqa_k_gain, w_branch, w_out, final_norm_gain):
    _, n_lat, d = x.shape
    n_ctx = ctx.shape[1]
    depth = w_in.shape[0]
    assert x.shape[0] == 1 and d == 2048 and w_in.shape[2] == Z_COLS
    assert n_lat % n_ctx == 0 and n_ctx % CHUNK == 0 and n_lat % GRID_W == 0

    xx = jnp.concatenate([x[0], ctx[0]], axis=0)
    c8 = jnp.concatenate([c, c_ctx[None], jnp.zeros((6, d), F32)], axis=0)
    mods = _ada_all(c8, w_ada, b_ada)
    tabs = _rope_tables(n_lat, n_ctx)
    w_in16 = w_in.astype(BF16)
    wb16 = w_branch.astype(BF16)
    wo16 = w_out.astype(BF16)

    for l in range(depth):
        last = l == depth - 1
        lambda_init = 0.8 - 0.6 * math.exp(-0.3 * l)
        z = _in_proj(xx, mods, norm_gain[l], w_in16, l, n_lat)
        rq, rk, dq, dk, dv, gq, gk, gv = _prep(z, tabs, gqa_q_gain[l], gqa_k_gain[l])
        ro = _retention(rq, rk, z, ret_log_rate[l], n_lat)
        do = _diff_attention(dq, dk, dv, n_lat, not last)
        go = _gqa_attention(gq, gk, gv, n_lat, not last)
        xx = _finish(xx, z, mods, diff_lambda[l], ro, do, go, diff_subln_gain[l], wb16, wo16,
                     final_norm_gain, l, n_lat, lambda_init, last)
    return xx[None]
```

```python
import functools
import math

import jax
import jax.numpy as jnp
from jax import lax
from jax.experimental import pallas as pl
from jax.experimental.pallas import tpu as pltpu

F32 = jnp.float32
BF16 = jnp.bfloat16

EPS = 1e-6
ROPE_THETA = 10000.0
GRID_W = 64
LANES = 128
VMEM_LIMIT = 56 * 1024 * 1024

HEADS = 8
HEAD_W = 128
KEY_W = 64
GQA_KV = 2
GQA_GROUP = 4
CHUNK = 128

Z_GATES, Z_RQ, Z_RK, Z_RV, Z_RG = 0, 6144, 6656, 7168, 8192
Z_DQ, Z_DK, Z_DV, Z_DG = 9216, 10240, 11264, 12288
Z_GQ, Z_GG, Z_GK, Z_GV = 13312, 14336, 15360, 15616
Z_COLS = 15872
W_TILE = 512
NORM_ROWS = 128
LOG2_E = math.log2(math.e)


def _pick(n, candidates):
    for c in candidates:
        if n % c == 0:
            return c
    raise ValueError(f"no tile in {candidates} divides {n}")


def _params(*sem):
    return pltpu.CompilerParams(dimension_semantics=sem, vmem_limit_bytes=VMEM_LIMIT)


def _sigmoid(x):
    return 0.5 * jnp.tanh(0.5 * x) + 0.5


def _silu(x):
    return x * _sigmoid(x)


def _ada_kernel(c_ref, w_ref, b_ref, o_ref):
    s = _silu(c_ref[...])
    o_ref[...] = jnp.dot(s, w_ref[...], preferred_element_type=F32,
                         precision=lax.Precision.HIGHEST) + b_ref[...]


def _ada_all(c8, w_ada, b_ada):
    depth, d, d3 = w_ada.shape
    tn = 1024
    return pl.pallas_call(
        _ada_kernel,
        grid=(depth, d3 // tn),
        in_specs=[
            pl.BlockSpec((8, d), lambda l, j: (0, 0)),
            pl.BlockSpec((None, d, tn), lambda l, j: (l, 0, j)),
            pl.BlockSpec((None, 1, tn), lambda l, j: (l, 0, j)),
        ],
        out_specs=pl.BlockSpec((None, 8, tn), lambda l, j: (l, 0, j)),
        out_shape=jax.ShapeDtypeStruct((depth, 8, d3), F32),
        compiler_params=_params("arbitrary", "arbitrary"),
    )(c8, w_ada, b_ada.reshape(depth, 1, d3))


def _row_mod(mod_ref, lo, hi, is_ctx):
    return jnp.where(is_ctx, mod_ref[1:2, lo:hi], mod_ref[0:1, lo:hi])


def _modulated_norm(x, gain, mod_ref, first_row, n_lat, d):
    y = x * lax.rsqrt(jnp.mean(x * x, axis=-1, keepdims=True) + EPS) * gain
    rows = first_row + lax.broadcasted_iota(jnp.int32, (x.shape[0], 1), 0)
    is_ctx = rows >= n_lat
    shift = _row_mod(mod_ref, 0, d, is_ctx)
    scale = _row_mod(mod_ref, d, 2 * d, is_ctx)
    return (y * (1.0 + scale) + shift).astype(BF16)


def _norm_kernel(x_ref, mod_ref, g_ref, h_ref, *, n_lat, tm, d):
    h_ref[...] = _modulated_norm(x_ref[...], g_ref[...], mod_ref, pl.program_id(0) * tm, n_lat, d)


def _norm_mod(xx, mods, gain, layer, n_lat):
    n, d = xx.shape
    tm = NORM_ROWS
    return pl.pallas_call(
        functools.partial(_norm_kernel, n_lat=n_lat, tm=tm, d=d),
        grid=(n // tm,),
        in_specs=[
            pl.BlockSpec((tm, d), lambda i: (i, 0)),
            pl.BlockSpec((None, 8, 3 * d), lambda i: (layer, 0, 0)),
            pl.BlockSpec((1, d), lambda i: (0, 0)),
        ],
        out_specs=pl.BlockSpec((tm, d), lambda i: (i, 0)),
        out_shape=jax.ShapeDtypeStruct((n, d), BF16),
        compiler_params=_params("arbitrary"),
    )(xx, mods, gain.reshape(1, d))


def _inproj_kernel(h_ref, w_ref, z_ref):
    z_ref[...] = jnp.dot(h_ref[...], w_ref[...], preferred_element_type=F32).astype(BF16)


def _w_block(j):
    return jnp.where(j < 12, j + 19, jnp.where(j < 28, j - 12, jnp.where(j < 30, j - 11, 16)))


def _in_proj(h, w, layer):
    n, d = h.shape
    tm = _pick(n, (3328, 1280, 640, 256))
    return pl.pallas_call(
        _inproj_kernel,
        grid=(n // tm, Z_COLS // W_TILE),
        in_specs=[
            pl.BlockSpec((tm, d), lambda i, j: (i, 0)),
            pl.BlockSpec((None, d, W_TILE), lambda i, j: (layer, 0, _w_block(j))),
        ],
        out_specs=pl.BlockSpec((tm, W_TILE), lambda i, j: (i, j)),
        out_shape=jax.ShapeDtypeStruct((n, Z_COLS), BF16),
        compiler_params=_params("arbitrary", "arbitrary"),
    )(h, w)


def _swap_halves(x, half):
    if 2 * half == LANES:
        return pltpu.roll(x, half, 1)
    lane = lax.broadcasted_iota(jnp.int32, x.shape, 1)
    return jnp.where(lane % (2 * half) < half,
                     pltpu.roll(x, LANES - half, 1), pltpu.roll(x, half, 1))


def _rope(x, c, s, half):
    return x * c + _swap_halves(x, half) * s


def _prep_kernel(rq_ref, rk_ref, dq_ref, dk_ref, dv_ref, gq_ref, gk_ref, gv_ref,
                 sc_ref, ss_ref, ac_ref, as_ref, bc_ref, bs_ref, qg_ref, kg_ref,
                 orq_ref, ork_ref, odq_ref, odk_ref, odv_ref, ogq_ref, ogk_ref, ogv_ref):
    tp = rq_ref.shape[0]
    lane = lax.broadcasted_iota(jnp.int32, (tp, LANES), 1)
    low = lane < KEY_W
    sc, ss = sc_ref[...], ss_ref[...]
    ac, as_ = ac_ref[...], as_ref[...]
    bc, bs = bc_ref[...], bs_ref[...]
    ones = jnp.ones((tp, LANES), BF16)

    k_scale = KEY_W ** -0.5
    for p in range(HEADS // 2):
        sl = slice(p * LANES, (p + 1) * LANES)
        for src, dst, mul in ((rq_ref, orq_ref, 1.0), (rk_ref, ork_ref, k_scale)):
            y = _rope(src[:, sl].astype(F32), sc, ss, KEY_W // 2) * mul
            dst[:, (2 * p) * LANES:(2 * p + 1) * LANES] = jnp.where(low, y, 0.0).astype(BF16)
            dst[:, (2 * p + 1) * LANES:(2 * p + 2) * LANES] = jnp.where(
                low, pltpu.roll(y, KEY_W, 1), 0.0).astype(BF16)

    d_scale = KEY_W ** -0.5 * LOG2_E
    for h in range(HEADS):
        sl = slice(h * LANES, (h + 1) * LANES)
        q = _rope(dq_ref[:, sl].astype(F32), ac, as_, KEY_W // 2) * d_scale
        odq_ref[0, :, sl] = jnp.where(low, q, 0.0).astype(BF16)
        odq_ref[1, :, sl] = jnp.where(low, 0.0, q).astype(BF16)
        odk_ref[:, sl] = _rope(dk_ref[:, sl].astype(F32), ac, as_, KEY_W // 2).astype(BF16)
        odv_ref[:, (2 * h) * LANES:(2 * h + 1) * LANES] = dv_ref[:, sl]
        odv_ref[:, (2 * h + 1) * LANES:(2 * h + 2) * LANES] = ones

    g_scale = HEAD_W ** -0.5 * LOG2_E

    def normed(x, gain):
        return x * lax.rsqrt(jnp.mean(x * x, axis=-1, keepdims=True) + EPS) * gain

    for h in range(HEADS):
        sl = slice(h * LANES, (h + 1) * LANES)
        q = _rope(normed(gq_ref[:, sl].astype(F32), qg_ref[...]), bc, bs, HEAD_W // 2)
        ogq_ref[:, sl] = (q * g_scale).astype(BF16)
    for h in range(GQA_KV):
        sl = slice(h * LANES, (h + 1) * LANES)
        k = _rope(normed(gk_ref[:, sl].astype(F32), kg_ref[...]), bc, bs, HEAD_W // 2)
        ogk_ref[:, sl] = k.astype(BF16)
        ogv_ref[:, (2 * h) * LANES:(2 * h + 1) * LANES] = gv_ref[:, sl]
        ogv_ref[:, (2 * h + 1) * LANES:(2 * h + 2) * LANES] = ones


def _prep(z, tabs, q_gain, k_gain):
    n = z.shape[0]
    tp = _pick(n, (640, 256))

    def zspec(off, width):
        return pl.BlockSpec((tp, width), lambda i, b=off // width: (i, b))

    tab = pl.BlockSpec((tp, LANES), lambda i: (i, 0))
    vec = pl.BlockSpec((1, LANES), lambda i: (0, 0))

    def ospec(width):
        return pl.BlockSpec((tp, width), lambda i: (i, 0))

    return pl.pallas_call(
        _prep_kernel,
        grid=(n // tp,),
        in_specs=[zspec(Z_RQ, 512), zspec(Z_RK, 512), zspec(Z_DQ, 1024), zspec(Z_DK, 1024),
                  zspec(Z_DV, 1024), zspec(Z_GQ, 1024), zspec(Z_GK, 256), zspec(Z_GV, 256),
                  tab, tab, tab, tab, tab, tab, vec, vec],
        out_specs=[ospec(1024), ospec(1024),
                   pl.BlockSpec((2, tp, 1024), lambda i: (0, i, 0)),
                   ospec(1024), ospec(2048), ospec(1024), ospec(256), ospec(512)],
        out_shape=[jax.ShapeDtypeStruct((n, 1024), BF16), jax.ShapeDtypeStruct((n, 1024), BF16),
                   jax.ShapeDtypeStruct((2, n, 1024), BF16), jax.ShapeDtypeStruct((n, 1024), BF16),
                   jax.ShapeDtypeStruct((n, 2048), BF16), jax.ShapeDtypeStruct((n, 1024), BF16),
                   jax.ShapeDtypeStruct((n, 256), BF16), jax.ShapeDtypeStruct((n, 512), BF16)],
        compiler_params=_params("arbitrary"),
    )(z, z, z, z, z, z, z, z, *tabs, q_gain.reshape(1, LANES), k_gain.reshape(1, LANES))


def _rope_tables(n_lat, n_ctx):
    def pattern(cos, sin):
        reps = LANES // (2 * cos.shape[1])
        c = jnp.tile(jnp.concatenate([cos, cos], axis=1), (1, reps))
        s = jnp.tile(jnp.concatenate([-sin, sin], axis=1), (1, reps))
        c = jnp.concatenate([c, jnp.ones((n_ctx, LANES), F32)], axis=0)
        s = jnp.concatenate([s, jnp.zeros((n_ctx, LANES), F32)], axis=0)
        return c, s

    def axial(head_dim):
        n_rows = n_lat // GRID_W
        rows = jnp.repeat(jnp.arange(n_rows), GRID_W).astype(F32)
        cols = jnp.tile(jnp.arange(GRID_W), n_rows).astype(F32)
        n_freq = head_dim // 4
        freqs = ROPE_THETA ** (-jnp.arange(n_freq, dtype=F32) / n_freq)
        ang = jnp.concatenate([rows[:, None] * freqs, cols[:, None] * freqs], axis=-1)
        return jnp.cos(ang), jnp.sin(ang)

    freqs = 1.0 / (ROPE_THETA ** jnp.linspace(0.0, 1.0, KEY_W // 2, dtype=F32))
    ang = jnp.arange(n_lat, dtype=F32)[:, None] * freqs
    return (*pattern(jnp.cos(ang), jnp.sin(ang)), *pattern(*axial(KEY_W)), *pattern(*axial(HEAD_W)))


def _log_decay(lr_ref, direction, h, shape):
    return -jnp.exp(jnp.full(shape, lr_ref[direction, h], F32))


def _ret_sum_kernel(lr_ref, k_ref, v_ref, kv_ref):
    h = pl.program_id(0)
    j = lax.broadcasted_iota(jnp.int32, (CHUNK, LANES), 0).astype(F32)
    w_f = jnp.exp(_log_decay(lr_ref, 0, h, (CHUNK, LANES)) * (CHUNK - 1 - j))
    w_b = jnp.exp(_log_decay(lr_ref, 1, h, (CHUNK, LANES)) * j)
    for c in range(k_ref.shape[0] // CHUNK):
        rows = slice(c * CHUNK, (c + 1) * CHUNK)
        k = k_ref[rows, :].astype(F32)
        kk = (k * w_f + pltpu.roll(k * w_b, KEY_W, 1)).T.astype(BF16)
        kv_ref[c] = jnp.dot(kk, v_ref[rows, :], preferred_element_type=F32)


def _ret_scan_kernel(lr_ref, kv_ref, st_ref, *, n_lat_chunks):
    h = pl.program_id(0)
    nc = kv_ref.shape[0]
    shape = (KEY_W, LANES)
    g_f = jnp.exp(_log_decay(lr_ref, 0, h, shape) * CHUNK)
    g_b = jnp.exp(_log_decay(lr_ref, 1, h, shape) * CHUNK)

    def fwd(c, s):
        st_ref[c, 0:KEY_W, :] = s.astype(BF16)
        return g_f * s + kv_ref[c, 0:KEY_W, :]

    def bwd(t, s):
        c = nc - 1 - t
        st_ref[c, KEY_W:, :] = s.astype(BF16)
        return g_b * s + kv_ref[c, KEY_W:, :]

    zero = jnp.zeros(shape, F32)
    s = lax.fori_loop(n_lat_chunks, nc, fwd, zero)
    lax.fori_loop(0, n_lat_chunks, fwd, s)
    lax.fori_loop(0, nc, bwd, zero)


def _ret_out_kernel(lr_ref, q_ref, k_ref, v_ref, st_ref, o_ref):
    h = pl.program_id(0)
    i = lax.broadcasted_iota(jnp.int32, (CHUNK, CHUNK), 0)
    j = lax.broadcasted_iota(jnp.int32, (CHUNK, CHUNK), 1)
    rel = (i - j).astype(F32)
    lg_f = _log_decay(lr_ref, 0, h, (CHUNK, CHUNK))
    lg_b = _log_decay(lr_ref, 1, h, (CHUNK, CHUNK))
    decay = jnp.where(i >= j, jnp.exp(lg_f * jnp.maximum(rel, 0.0)),
                      jnp.exp(lg_b * jnp.maximum(-rel, 0.0)))
    pos = i.astype(F32)
    cross_f = jnp.exp(lg_f * (pos + 1.0))
    cross_b = jnp.exp(lg_b * (CHUNK - pos))
    for c in range(q_ref.shape[0] // CHUNK):
        rows = slice(c * CHUNK, (c + 1) * CHUNK)
        q = q_ref[rows, :]
        att = lax.dot_general(q, k_ref[rows, :], (((1,), (1,)), ((), ())),
                              preferred_element_type=F32) * decay
        qf = q.astype(F32)
        qs = (qf * cross_f + pltpu.roll(qf * cross_b, KEY_W, 1)).astype(BF16)
        lhs = jnp.concatenate([att.astype(BF16), qs], axis=1)
        rhs = jnp.concatenate([v_ref[rows, :], st_ref[c]], axis=0)
        o_ref[rows, :] = jnp.dot(lhs, rhs, preferred_element_type=F32)


def _retention(rq, rk, z, log_rate, n_lat):
    n = z.shape[0]
    nc = n // CHUNK
    tr = _pick(n, (1280, 640, 256))
    cpt = tr // CHUNK
    smem = pl.BlockSpec(memory_space=pltpu.SMEM)
    head_rows = pl.BlockSpec((tr, LANES), lambda h, i: (i, h))
    v_rows = pl.BlockSpec((tr, LANES), lambda h, i: (i, Z_RV // LANES + h))
    chunk_mats = pl.BlockSpec((None, cpt, CHUNK, LANES), lambda h, i: (h, i, 0, 0))

    kv = pl.pallas_call(
        _ret_sum_kernel,
        grid=(HEADS, n // tr),
        in_specs=[smem, head_rows, v_rows],
        out_specs=chunk_mats,
        out_shape=jax.ShapeDtypeStruct((HEADS, nc, CHUNK, LANES), F32),
        compiler_params=_params("arbitrary", "arbitrary"),
    )(log_rate, rk, z)

    all_chunks = pl.BlockSpec((None, nc, CHUNK, LANES), lambda h: (h, 0, 0, 0))
    st = pl.pallas_call(
        functools.partial(_ret_scan_kernel, n_lat_chunks=n_lat // CHUNK),
        grid=(HEADS,),
        in_specs=[smem, all_chunks],
        out_specs=all_chunks,
        out_shape=jax.ShapeDtypeStruct((HEADS, nc, CHUNK, LANES), BF16),
        compiler_params=_params("arbitrary"),
    )(log_rate, kv)

    return pl.pallas_call(
        _ret_out_kernel,
        grid=(HEADS, n // tr),
        in_specs=[smem, head_rows, head_rows, v_rows, chunk_mats],
        out_specs=head_rows,
        out_shape=jax.ShapeDtypeStruct((n, HEADS * LANES), F32),
        compiler_params=_params("arbitrary", "arbitrary"),
    )(log_rate, rq, rk, z, st)


def _flash_kernel(q_ref, k_ref, v_ref, *rest, tq, tk, unroll):
    o_ref, s0_ref, s1_ref, m_ref, acc_ref = rest[-5:]
    bufs = (s0_ref, s1_ref)
    nk = k_ref.shape[0] // tk
    n_steps = (q_ref.shape[0] // tq) * nk

    def q_rows(qt):
        return pl.ds(pl.multiple_of(qt * tq, tq), tq)

    def key_rows(c):
        return pl.ds(pl.multiple_of(c * tk, tk), tk)

    def scores(qt, c, dst):
        dst[...] = lax.dot_general(q_ref[q_rows(qt), :], k_ref[key_rows(c), :],
                                   (((1,), (1,)), ((), ())), preferred_element_type=F32)

    def consume(qt, c, src):
        s = src[...]
        m = jnp.where(c == 0, -jnp.inf, m_ref[...])
        m_new = jnp.maximum(m, jnp.max(s, axis=-1, keepdims=True))
        p = jnp.exp2(s - m_new)
        acc = jnp.exp2(m - m_new) * acc_ref[...] + jnp.dot(
            p.astype(BF16), v_ref[key_rows(c), :], preferred_element_type=F32)
        acc_ref[...] = acc
        m_ref[...] = m_new
        o_ref[q_rows(qt), :] = (acc[:, :LANES] / acc[:, LANES:]).astype(o_ref.dtype)

    def step(parity, qt, c):
        wrap = c == nk - 1
        qt_next = jnp.where(wrap, qt + 1, qt)
        c_next = jnp.where(wrap, 0, c + 1)
        scores(qt_next, c_next, bufs[1 - parity])
        consume(qt, c, bufs[parity])
        return qt_next, c_next

    def group(_, carry):
        qt, c = carry
        for u in range(unroll):
            qt, c = step(u % 2, qt, c)
        return qt, c

    m_ref[...] = jnp.full(m_ref.shape, -jnp.inf, F32)
    acc_ref[...] = jnp.zeros(acc_ref.shape, F32)
    zero = jnp.int32(0)
    scores(zero, zero, s0_ref)
    qt, c = lax.fori_loop(0, (n_steps - 1) // unroll, group, (zero, zero))
    for u in range((n_steps - 1) % unroll):
        qt, c = step(u % 2, qt, c)
    consume(qt, c, bufs[(n_steps - 1) % 2])


FLASH_UNROLL = 6


def _flash(q, k, v, out_shape, q_map, kv_map, o_map, grid, tq, q_tiles, n_keys, key_block,
           prev=None):
    tk = _pick(n_keys, (1280, 640, 256))
    q_block = (None,) * (q.ndim - 2) + (tq * q_tiles, LANES)
    o_block = (None,) * (len(out_shape.shape) - 2) + (tq * q_tiles, LANES)
    in_specs = [
        pl.BlockSpec(q_block, q_map),
        pl.BlockSpec((n_keys, LANES), lambda g, r: (key_block, kv_map(g)),
                     pipeline_mode=pl.Buffered(1)),
        pl.BlockSpec((n_keys, 2 * LANES), lambda g, r: (key_block, kv_map(g)),
                     pipeline_mode=pl.Buffered(1)),
    ]
    args = [q, k, v]
    aliases = {}
    if prev is not None:
        in_specs.append(pl.BlockSpec(memory_space=pl.ANY))
        args.append(prev)
        aliases = {3: 0}
    return pl.pallas_call(
        functools.partial(_flash_kernel, tq=tq, tk=tk, unroll=FLASH_UNROLL),
        grid=grid,
        in_specs=in_specs,
        out_specs=pl.BlockSpec(o_block, o_map),
        out_shape=out_shape,
        scratch_shapes=[pltpu.VMEM((tq, tk), F32), pltpu.VMEM((tq, tk), F32),
                        pltpu.VMEM((tq, 1), F32), pltpu.VMEM((tq, 2 * LANES), F32)],
        input_output_aliases=aliases,
        compiler_params=_params("arbitrary", "arbitrary"),
    )(*args)


def _query_tiling(n_lat):
    tq = _pick(n_lat, (1024, 512, 256))
    q_tiles = _pick(n_lat // tq, (8, 4, 2, 1))
    return tq, q_tiles, n_lat // (tq * q_tiles)


def _diff_attention(dq, dk, dv, n_lat, with_ctx):
    n = dk.shape[0]
    n_ctx = n - n_lat
    tq, q_tiles, nb = _query_tiling(n_lat)
    shape = jax.ShapeDtypeStruct((2, n, HEADS * LANES), F32)
    o = _flash(dq, dk, dv, shape,
               lambda g, r: (r // nb, r % nb, g), lambda g: g, lambda g, r: (r // nb, r % nb, g),
               (HEADS, 2 * nb), tq, q_tiles, n, 0)
    if with_ctx:
        cb = n_lat // n_ctx
        o = _flash(dq, dk, dv, shape,
                   lambda g, r: (r, cb, g), lambda g: g, lambda g, r: (r, cb, g),
                   (HEADS, 2), n_ctx, 1, n_ctx, cb, prev=o)
    return o


def _gqa_attention(gq, gk, gv, n_lat, with_ctx):
    n = gk.shape[0]
    n_ctx = n - n_lat
    tq, q_tiles, nb = _query_tiling(n_lat)
    shape = jax.ShapeDtypeStruct((n, HEADS * LANES), BF16)
    o = _flash(gq, gk, gv, shape,
               lambda g, r: (r % nb, g * GQA_GROUP + r // nb), lambda g: g,
               lambda g, r: (r % nb, g * GQA_GROUP + r // nb),
               (GQA_KV, GQA_GROUP * nb), tq, q_tiles, n, 0)
    if with_ctx:
        cb = n_lat // n_ctx
        o = _flash(gq, gk, gv, shape,
                   lambda g, r: (cb, g * GQA_GROUP + r), lambda g: g,
                   lambda g, r: (cb, g * GQA_GROUP + r),
                   (GQA_KV, GQA_GROUP), n_ctx, 1, n_ctx, cb, prev=o)
    return o


def _finish_kernel(lam_ref, x_ref, mod_ref, gates_ref, rg_ref, dg_ref, gg_ref, ro_ref, do_ref,
                   go_ref, sub_ref, wb_ref, wo_ref, ng_ref, *rest, n_lat, tm, d, lambda_init,
                   final_norm):
    lp = lam_ref[...]
    lam = (jnp.exp(jnp.sum(lp[0:1] * lp[1:2], axis=-1, keepdims=True))
           - jnp.exp(jnp.sum(lp[2:3] * lp[3:4], axis=-1, keepdims=True)) + lambda_init)

    def head_norm(o):
        return o * lax.rsqrt(jnp.mean(o * o, axis=-1, keepdims=True) + EPS)

    branches = []
    for h in range(HEADS):
        sl = slice(h * LANES, (h + 1) * LANES)
        r = head_norm(ro_ref[:, sl]) * _silu(rg_ref[:, sl].astype(F32))
        dd = head_norm(do_ref[0, :, sl] - lam * do_ref[1, :, sl]) * sub_ref[...] * (1.0 - lambda_init)
        dd = dd * _silu(dg_ref[:, sl].astype(F32))
        g = go_ref[:, sl].astype(F32) * _silu(gg_ref[:, sl].astype(F32))
        branches.append((r.astype(BF16), dd.astype(BF16), g.astype(BF16)))

    merged = jnp.zeros((tm, d), F32)
    for b in range(3):
        br = jnp.concatenate([branches[h][b] for h in range(HEADS)], axis=1)
        y = jnp.dot(br, wb_ref[b], preferred_element_type=F32)
        merged = merged + _sigmoid(gates_ref[:, b * d:(b + 1) * d].astype(F32)) * y
    out = jnp.dot(merged.astype(BF16), wo_ref[...], preferred_element_type=F32)

    first_row = pl.program_id(0) * tm
    rows = first_row + lax.broadcasted_iota(jnp.int32, (tm, 1), 0)
    gate = _row_mod(mod_ref, 2 * d, 3 * d, rows >= n_lat)
    x = x_ref[...] + gate * out
    if final_norm:
        (o_ref,) = rest
        o_ref[...] = x * lax.rsqrt(jnp.mean(x * x, axis=-1, keepdims=True) + EPS) * ng_ref[...]
    else:
        next_mod_ref, o_ref, h_ref = rest
        o_ref[...] = x
        h_ref[...] = _modulated_norm(x, ng_ref[...], next_mod_ref, first_row, n_lat, d)


def _finish(xx, z, mods, lam_params, ro, do, go, subln, wb, wo, next_gain, layer, n_lat,
            lambda_init, last):
    n, d = xx.shape
    tm = 256
    n_rows = n_lat if last else n
    bw = HEADS * LANES

    def rows(width, off=0):
        return pl.BlockSpec((tm, width), lambda i, b=off // width: (i, b))

    const2 = lambda i: (0, 0)
    kern = functools.partial(_finish_kernel, n_lat=n_lat, tm=tm, d=d, lambda_init=lambda_init,
                             final_norm=last)
    if last:
        extra_specs, extra_args = [], []
        out_specs = rows(d)
        out_shape = jax.ShapeDtypeStruct((n_rows, d), F32)
    else:
        extra_specs = [pl.BlockSpec((None, 8, 3 * d), lambda i: (layer + 1, 0, 0))]
        extra_args = [mods]
        out_specs = [rows(d), rows(d)]
        out_shape = [jax.ShapeDtypeStruct((n_rows, d), F32), jax.ShapeDtypeStruct((n_rows, d), BF16)]
    return pl.pallas_call(
        kern,
        grid=(n_rows // tm,),
        in_specs=[
            pl.BlockSpec((4, KEY_W), const2),
            rows(d),
            pl.BlockSpec((None, 8, 3 * d), lambda i: (layer, 0, 0)),
            rows(3 * d, Z_GATES), rows(bw, Z_RG), rows(bw, Z_DG), rows(bw, Z_GG),
            rows(bw),
            pl.BlockSpec((2, tm, bw), lambda i: (0, i, 0)),
            rows(bw),
            pl.BlockSpec((1, LANES), const2),
            pl.BlockSpec((None, 3, bw, d), lambda i: (layer, 0, 0, 0),
                         pipeline_mode=pl.Buffered(1)),
            pl.BlockSpec((None, d, d), lambda i: (layer, 0, 0), pipeline_mode=pl.Buffered(1)),
            pl.BlockSpec((1, d), const2),
        ] + extra_specs,
        out_specs=out_specs,
        out_shape=out_shape,
        compiler_params=_params("arbitrary"),
    )(lam_params, xx, mods, z, z, z, z, ro, do, go, subln.reshape(1, LANES), wb, wo,
      next_gain.reshape(1, d), *extra_args)


def kernel(x, c, ctx, c_ctx, norm_gain, w_ada, b_ada, w_in, ret_log_rate, diff_lambda,
           diff_subln_gain, gqa_q_gain, gqa_k_gain, w_branch, w_out, final_norm_gain):
    _, n_lat, d = x.shape
    n_ctx = ctx.shape[1]
    depth = w_in.shape[0]
    assert x.shape[0] == 1 and d == 2048 and w_in.shape[2] == Z_COLS
    assert n_lat % n_ctx == 0 and n_ctx % CHUNK == 0 and n_lat % GRID_W == 0

    xx = jnp.concatenate([x[0], ctx[0]], axis=0)
    c8 = jnp.concatenate([c, c_ctx[None], jnp.zeros((6, d), F32)], axis=0)
    mods = _ada_all(c8, w_ada, b_ada)
    tabs = _rope_tables(n_lat, n_ctx)
    w_in16 = w_in.astype(BF16)
    wb16 = w_branch.astype(BF16)
    wo16 = w_out.astype(BF16)

    h = _norm_mod(xx, mods, norm_gain[0], 0, n_lat)
    for l in range(depth):
        last = l == depth - 1
        lambda_init = 0.8 - 0.6 * math.exp(-0.3 * l)
        z = _in_proj(h, w_in16, l)
        rq, rk, dq, dk, dv, gq, gk, gv = _prep(z, tabs, gqa_q_gain[l], gqa_k_gain[l])
        ro = _retention(rq, rk, z, ret_log_rate[l], n_lat)
        do = _diff_attention(dq, dk, dv, n_lat, not last)
        go = _gqa_attention(gq, gk, gv, n_lat, not last)
        next_gain = final_norm_gain if last else norm_gain[l + 1]
        out = _finish(xx, z, mods, diff_lambda[l], ro, do, go, diff_subln_gain[l], wb16, wo16,
                      next_gain, l, n_lat, lambda_init, last)
        if last:
            return out[None]
        xx, h = out
```

```python
import functools
import math

import jax
import jax.numpy as jnp
from jax import lax
from jax.experimental import pallas as pl
from jax.experimental.pallas import tpu as pltpu

F32 = jnp.float32
BF16 = jnp.bfloat16

EPS = 1e-6
ROPE_THETA = 10000.0
GRID_W = 64
LANES = 128
VMEM_LIMIT = 56 * 1024 * 1024

HEADS = 8
HEAD_W = 128
KEY_W = 64
GQA_KV = 2
GQA_GROUP = 4
CHUNK = 128

Z_GATES, Z_RQ, Z_RK, Z_RV, Z_RG = 0, 6144, 6656, 7168, 8192
Z_DQ, Z_DK, Z_DV, Z_DG = 9216, 10240, 11264, 12288
Z_GQ, Z_GG, Z_GK, Z_GV = 13312, 14336, 15360, 15616
Z_COLS = 15872
W_TILE = 512
NORM_ROWS = 128
LOG2_E = math.log2(math.e)


def _pick(n, candidates):
    for c in candidates:
        if n % c == 0:
            return c
    raise ValueError(f"no tile in {candidates} divides {n}")


def _params(*sem):
    return pltpu.CompilerParams(dimension_semantics=sem, vmem_limit_bytes=VMEM_LIMIT)


def _sigmoid(x):
    return 0.5 * jnp.tanh(0.5 * x) + 0.5


def _silu(x):
    return x * _sigmoid(x)


def _ada_kernel(c_ref, w_ref, b_ref, o_ref):
    s = _silu(c_ref[...])
    o_ref[...] = jnp.dot(s, w_ref[...], preferred_element_type=F32,
                         precision=lax.Precision.HIGHEST) + b_ref[...]


def _ada_all(c8, w_ada, b_ada):
    depth, d, d3 = w_ada.shape
    tn = 1024
    return pl.pallas_call(
        _ada_kernel,
        grid=(depth, d3 // tn),
        in_specs=[
            pl.BlockSpec((8, d), lambda l, j: (0, 0)),
            pl.BlockSpec((None, d, tn), lambda l, j: (l, 0, j)),
            pl.BlockSpec((None, 1, tn), lambda l, j: (l, 0, j)),
        ],
        out_specs=pl.BlockSpec((None, 8, tn), lambda l, j: (l, 0, j)),
        out_shape=jax.ShapeDtypeStruct((depth, 8, d3), F32),
        compiler_params=_params("arbitrary", "arbitrary"),
    )(c8, w_ada, b_ada.reshape(depth, 1, d3))


def _row_mod(mod_ref, lo, hi, is_ctx):
    return jnp.where(is_ctx, mod_ref[1:2, lo:hi], mod_ref[0:1, lo:hi])


def _modulated_norm(x, gain, mod_ref, first_row, n_lat, d):
    y = x * lax.rsqrt(jnp.mean(x * x, axis=-1, keepdims=True) + EPS) * gain
    rows = first_row + lax.broadcasted_iota(jnp.int32, (x.shape[0], 1), 0)
    is_ctx = rows >= n_lat
    shift = _row_mod(mod_ref, 0, d, is_ctx)
    scale = _row_mod(mod_ref, d, 2 * d, is_ctx)
    return (y * (1.0 + scale) + shift).astype(BF16)


def _norm_kernel(x_ref, mod_ref, g_ref, h_ref, *, n_lat, tm, d):
    h_ref[...] = _modulated_norm(x_ref[...], g_ref[...], mod_ref, pl.program_id(0) * tm, n_lat, d)


def _norm_mod(xx, mods, gain, layer, n_lat):
    n, d = xx.shape
    tm = NORM_ROWS
    return pl.pallas_call(
        functools.partial(_norm_kernel, n_lat=n_lat, tm=tm, d=d),
        grid=(n // tm,),
        in_specs=[
            pl.BlockSpec((tm, d), lambda i: (i, 0)),
            pl.BlockSpec((None, 8, 3 * d), lambda i: (layer, 0, 0)),
            pl.BlockSpec((1, d), lambda i: (0, 0)),
        ],
        out_specs=pl.BlockSpec((tm, d), lambda i: (i, 0)),
        out_shape=jax.ShapeDtypeStruct((n, d), BF16),
        compiler_params=_params("arbitrary"),
    )(xx, mods, gain.reshape(1, d))


def _inproj_kernel(h_ref, w_ref, z_ref):
    z_ref[...] = jnp.dot(h_ref[...], w_ref[...].astype(BF16),
                         preferred_element_type=F32).astype(BF16)


def _w_block(j):
    return jnp.where(j < 12, j + 19, jnp.where(j < 28, j - 12, jnp.where(j < 30, j - 11, 16)))


def _in_proj(h, w, layer):
    n, d = h.shape
    tm = _pick(n, (3328, 1280, 640, 256))
    return pl.pallas_call(
        _inproj_kernel,
        grid=(n // tm, Z_COLS // W_TILE),
        in_specs=[
            pl.BlockSpec((tm, d), lambda i, j: (i, 0)),
            pl.BlockSpec((None, d, W_TILE), lambda i, j: (layer, 0, _w_block(j))),
        ],
        out_specs=pl.BlockSpec((tm, W_TILE), lambda i, j: (i, j)),
        out_shape=jax.ShapeDtypeStruct((n, Z_COLS), BF16),
        compiler_params=_params("arbitrary", "arbitrary"),
    )(h, w)


def _swap_halves(x, half):
    if 2 * half == LANES:
        return pltpu.roll(x, half, 1)
    lane = lax.broadcasted_iota(jnp.int32, x.shape, 1)
    return jnp.where(lane % (2 * half) < half,
                     pltpu.roll(x, LANES - half, 1), pltpu.roll(x, half, 1))


def _rope(x, c, s, half):
    return x * c + _swap_halves(x, half) * s


def _prep_kernel(rq_ref, rk_ref, dq_ref, dk_ref, dv_ref, gq_ref, gk_ref, gv_ref,
                 sc_ref, ss_ref, ac_ref, as_ref, bc_ref, bs_ref, qg_ref, kg_ref,
                 orq_ref, ork_ref, odq_ref, odk_ref, odv_ref, ogq_ref, ogk_ref, ogv_ref):
    tp = rq_ref.shape[0]
    lane = lax.broadcasted_iota(jnp.int32, (tp, LANES), 1)
    low = lane < KEY_W
    sc, ss = sc_ref[...], ss_ref[...]
    ac, as_ = ac_ref[...], as_ref[...]
    bc, bs = bc_ref[...], bs_ref[...]
    ones = jnp.ones((tp, LANES), BF16)

    k_scale = KEY_W ** -0.5
    for p in range(HEADS // 2):
        sl = slice(p * LANES, (p + 1) * LANES)
        for src, dst, mul in ((rq_ref, orq_ref, 1.0), (rk_ref, ork_ref, k_scale)):
            y = _rope(src[:, sl].astype(F32), sc, ss, KEY_W // 2) * mul
            dst[:, (2 * p) * LANES:(2 * p + 1) * LANES] = jnp.where(low, y, 0.0).astype(BF16)
            dst[:, (2 * p + 1) * LANES:(2 * p + 2) * LANES] = jnp.where(
                low, pltpu.roll(y, KEY_W, 1), 0.0).astype(BF16)

    d_scale = KEY_W ** -0.5 * LOG2_E
    for h in range(HEADS):
        sl = slice(h * LANES, (h + 1) * LANES)
        q = _rope(dq_ref[:, sl].astype(F32), ac, as_, KEY_W // 2) * d_scale
        odq_ref[0, :, sl] = jnp.where(low, q, 0.0).astype(BF16)
        odq_ref[1, :, sl] = jnp.where(low, 0.0, q).astype(BF16)
        odk_ref[:, sl] = _rope(dk_ref[:, sl].astype(F32), ac, as_, KEY_W // 2).astype(BF16)
        odv_ref[:, (2 * h) * LANES:(2 * h + 1) * LANES] = dv_ref[:, sl]
        odv_ref[:, (2 * h + 1) * LANES:(2 * h + 2) * LANES] = ones

    g_scale = HEAD_W ** -0.5 * LOG2_E

    def normed(x, gain):
        return x * lax.rsqrt(jnp.mean(x * x, axis=-1, keepdims=True) + EPS) * gain

    for h in range(HEADS):
        sl = slice(h * LANES, (h + 1) * LANES)
        q = _rope(normed(gq_ref[:, sl].astype(F32), qg_ref[...]), bc, bs, HEAD_W // 2)
        ogq_ref[:, sl] = (q * g_scale).astype(BF16)
    for h in range(GQA_KV):
        sl = slice(h * LANES, (h + 1) * LANES)
        k = _rope(normed(gk_ref[:, sl].astype(F32), kg_ref[...]), bc, bs, HEAD_W // 2)
        ogk_ref[:, sl] = k.astype(BF16)
        ogv_ref[:, (2 * h) * LANES:(2 * h + 1) * LANES] = gv_ref[:, sl]
        ogv_ref[:, (2 * h + 1) * LANES:(2 * h + 2) * LANES] = ones


def _prep(z, tabs, q_gain, k_gain):
    n = z.shape[0]
    tp = _pick(n, (640, 256))

    def zspec(off, width):
        return pl.BlockSpec((tp, width), lambda i, b=off // width: (i, b))

    tab = pl.BlockSpec((tp, LANES), lambda i: (i, 0))
    vec = pl.BlockSpec((1, LANES), lambda i: (0, 0))

    def ospec(width):
        return pl.BlockSpec((tp, width), lambda i: (i, 0))

    return pl.pallas_call(
        _prep_kernel,
        grid=(n // tp,),
        in_specs=[zspec(Z_RQ, 512), zspec(Z_RK, 512), zspec(Z_DQ, 1024), zspec(Z_DK, 1024),
                  zspec(Z_DV, 1024), zspec(Z_GQ, 1024), zspec(Z_GK, 256), zspec(Z_GV, 256),
                  tab, tab, tab, tab, tab, tab, vec, vec],
        out_specs=[ospec(1024), ospec(1024),
                   pl.BlockSpec((2, tp, 1024), lambda i: (0, i, 0)),
                   ospec(1024), ospec(2048), ospec(1024), ospec(256), ospec(512)],
        out_shape=[jax.ShapeDtypeStruct((n, 1024), BF16), jax.ShapeDtypeStruct((n, 1024), BF16),
                   jax.ShapeDtypeStruct((2, n, 1024), BF16), jax.ShapeDtypeStruct((n, 1024), BF16),
                   jax.ShapeDtypeStruct((n, 2048), BF16), jax.ShapeDtypeStruct((n, 1024), BF16),
                   jax.ShapeDtypeStruct((n, 256), BF16), jax.ShapeDtypeStruct((n, 512), BF16)],
        compiler_params=_params("arbitrary"),
    )(z, z, z, z, z, z, z, z, *tabs, q_gain.reshape(1, LANES), k_gain.reshape(1, LANES))


def _rope_tables(n_lat, n_ctx):
    def pattern(cos, sin):
        reps = LANES // (2 * cos.shape[1])
        c = jnp.tile(jnp.concatenate([cos, cos], axis=1), (1, reps))
        s = jnp.tile(jnp.concatenate([-sin, sin], axis=1), (1, reps))
        c = jnp.concatenate([c, jnp.ones((n_ctx, LANES), F32)], axis=0)
        s = jnp.concatenate([s, jnp.zeros((n_ctx, LANES), F32)], axis=0)
        return c, s

    def axial(head_dim):
        n_rows = n_lat // GRID_W
        rows = jnp.repeat(jnp.arange(n_rows), GRID_W).astype(F32)
        cols = jnp.tile(jnp.arange(GRID_W), n_rows).astype(F32)
        n_freq = head_dim // 4
        freqs = ROPE_THETA ** (-jnp.arange(n_freq, dtype=F32) / n_freq)
        ang = jnp.concatenate([rows[:, None] * freqs, cols[:, None] * freqs], axis=-1)
        return jnp.cos(ang), jnp.sin(ang)

    freqs = 1.0 / (ROPE_THETA ** jnp.linspace(0.0, 1.0, KEY_W // 2, dtype=F32))
    ang = jnp.arange(n_lat, dtype=F32)[:, None] * freqs
    return (*pattern(jnp.cos(ang), jnp.sin(ang)), *pattern(*axial(KEY_W)), *pattern(*axial(HEAD_W)))


def _log_decay(lr_ref, direction, h, shape):
    return -jnp.exp(jnp.full(shape, lr_ref[direction, h], F32))


def _ret_sum_kernel(lr_ref, k_ref, v_ref, kv_ref):
    h = pl.program_id(0)
    j = lax.broadcasted_iota(jnp.int32, (CHUNK, LANES), 0).astype(F32)
    w_f = jnp.exp(_log_decay(lr_ref, 0, h, (CHUNK, LANES)) * (CHUNK - 1 - j))
    w_b = jnp.exp(_log_decay(lr_ref, 1, h, (CHUNK, LANES)) * j)
    for c in range(k_ref.shape[0] // CHUNK):
        rows = slice(c * CHUNK, (c + 1) * CHUNK)
        k = k_ref[rows, :].astype(F32)
        kk = (k * w_f + pltpu.roll(k * w_b, KEY_W, 1)).T.astype(BF16)
        kv_ref[c] = jnp.dot(kk, v_ref[rows, :], preferred_element_type=F32)


def _ret_scan_kernel(lr_ref, kv_ref, st_ref, *, n_lat_chunks):
    h = pl.program_id(0)
    nc = kv_ref.shape[0]
    shape = (KEY_W, LANES)
    g_f = jnp.exp(_log_decay(lr_ref, 0, h, shape) * CHUNK)
    g_b = jnp.exp(_log_decay(lr_ref, 1, h, shape) * CHUNK)

    def fwd(c, s):
        st_ref[c, 0:KEY_W, :] = s.astype(BF16)
        return g_f * s + kv_ref[c, 0:KEY_W, :]

    def bwd(t, s):
        c = nc - 1 - t
        st_ref[c, KEY_W:, :] = s.astype(BF16)
        return g_b * s + kv_ref[c, KEY_W:, :]

    zero = jnp.zeros(shape, F32)
    s = lax.fori_loop(n_lat_chunks, nc, fwd, zero)
    lax.fori_loop(0, n_lat_chunks, fwd, s)
    lax.fori_loop(0, nc, bwd, zero)


def _ret_out_kernel(lr_ref, q_ref, k_ref, v_ref, st_ref, o_ref):
    h = pl.program_id(0)
    i = lax.broadcasted_iota(jnp.int32, (CHUNK, CHUNK), 0)
    j = lax.broadcasted_iota(jnp.int32, (CHUNK, CHUNK), 1)
    rel = (i - j).astype(F32)
    lg_f = _log_decay(lr_ref, 0, h, (CHUNK, CHUNK))
    lg_b = _log_decay(lr_ref, 1, h, (CHUNK, CHUNK))
    decay = jnp.where(i >= j, jnp.exp(lg_f * jnp.maximum(rel, 0.0)),
                      jnp.exp(lg_b * jnp.maximum(-rel, 0.0)))
    pos = i.astype(F32)
    cross_f = jnp.exp(lg_f * (pos + 1.0))
    cross_b = jnp.exp(lg_b * (CHUNK - pos))
    for c in range(q_ref.shape[0] // CHUNK):
        rows = slice(c * CHUNK, (c + 1) * CHUNK)
        q = q_ref[rows, :]
        att = lax.dot_general(q, k_ref[rows, :], (((1,), (1,)), ((), ())),
                              preferred_element_type=F32) * decay
        qf = q.astype(F32)
        qs = (qf * cross_f + pltpu.roll(qf * cross_b, KEY_W, 1)).astype(BF16)
        lhs = jnp.concatenate([att.astype(BF16), qs], axis=1)
        rhs = jnp.concatenate([v_ref[rows, :], st_ref[c]], axis=0)
        o_ref[rows, :] = jnp.dot(lhs, rhs, preferred_element_type=F32)


def _retention(rq, rk, z, log_rate, n_lat):
    n = z.shape[0]
    nc = n // CHUNK
    tr = _pick(n, (1280, 640, 256))
    cpt = tr // CHUNK
    smem = pl.BlockSpec(memory_space=pltpu.SMEM)
    head_rows = pl.BlockSpec((tr, LANES), lambda h, i: (i, h))
    v_rows = pl.BlockSpec((tr, LANES), lambda h, i: (i, Z_RV // LANES + h))
    chunk_mats = pl.BlockSpec((None, cpt, CHUNK, LANES), lambda h, i: (h, i, 0, 0))

    kv = pl.pallas_call(
        _ret_sum_kernel,
        grid=(HEADS, n // tr),
        in_specs=[smem, head_rows, v_rows],
        out_specs=chunk_mats,
        out_shape=jax.ShapeDtypeStruct((HEADS, nc, CHUNK, LANES), F32),
        compiler_params=_params("arbitrary", "arbitrary"),
    )(log_rate, rk, z)

    all_chunks = pl.BlockSpec((None, nc, CHUNK, LANES), lambda h: (h, 0, 0, 0))
    st = pl.pallas_call(
        functools.partial(_ret_scan_kernel, n_lat_chunks=n_lat // CHUNK),
        grid=(HEADS,),
        in_specs=[smem, all_chunks],
        out_specs=all_chunks,
        out_shape=jax.ShapeDtypeStruct((HEADS, nc, CHUNK, LANES), BF16),
        compiler_params=_params("arbitrary"),
    )(log_rate, kv)

    return pl.pallas_call(
        _ret_out_kernel,
        grid=(HEADS, n // tr),
        in_specs=[smem, head_rows, head_rows, v_rows, chunk_mats],
        out_specs=head_rows,
        out_shape=jax.ShapeDtypeStruct((n, HEADS * LANES), F32),
        compiler_params=_params("arbitrary", "arbitrary"),
    )(log_rate, rq, rk, z, st)


def _flash_kernel(q_ref, k_ref, v_ref, *rest, tq, tk, unroll):
    o_ref, s0_ref, s1_ref, m_ref, acc_ref = rest[-5:]
    bufs = (s0_ref, s1_ref)
    nk = k_ref.shape[0] // tk
    n_steps = (q_ref.shape[0] // tq) * nk

    def q_rows(qt):
        return pl.ds(pl.multiple_of(qt * tq, tq), tq)

    def key_rows(c):
        return pl.ds(pl.multiple_of(c * tk, tk), tk)

    def scores(qt, c, dst):
        dst[...] = lax.dot_general(q_ref[q_rows(qt), :], k_ref[key_rows(c), :],
                                   (((1,), (1,)), ((), ())), preferred_element_type=F32)

    def consume(qt, c, src):
        s = src[...]
        m = jnp.where(c == 0, -jnp.inf, m_ref[...])
        m_new = jnp.maximum(m, jnp.max(s, axis=-1, keepdims=True))
        p = jnp.exp2(s - m_new)
        acc = jnp.exp2(m - m_new) * acc_ref[...] + jnp.dot(
            p.astype(BF16), v_ref[key_rows(c), :], preferred_element_type=F32)
        acc_ref[...] = acc
        m_ref[...] = m_new
        o_ref[q_rows(qt), :] = (acc[:, :LANES] / acc[:, LANES:]).astype(o_ref.dtype)

    def step(parity, qt, c):
        wrap = c == nk - 1
        qt_next = jnp.where(wrap, qt + 1, qt)
        c_next = jnp.where(wrap, 0, c + 1)
        scores(qt_next, c_next, bufs[1 - parity])
        consume(qt, c, bufs[parity])
        return qt_next, c_next

    def group(_, carry):
        qt, c = carry
        for u in range(unroll):
            qt, c = step(u % 2, qt, c)
        return qt, c

    m_ref[...] = jnp.full(m_ref.shape, -jnp.inf, F32)
    acc_ref[...] = jnp.zeros(acc_ref.shape, F32)
    zero = jnp.int32(0)
    scores(zero, zero, s0_ref)
    qt, c = lax.fori_loop(0, (n_steps - 1) // unroll, group, (zero, zero))
    for u in range((n_steps - 1) % unroll):
        qt, c = step(u % 2, qt, c)
    consume(qt, c, bufs[(n_steps - 1) % 2])


FLASH_UNROLL = 6


def _flash(q, k, v, out_shape, q_map, kv_map, o_map, grid, tq, q_tiles, n_keys, key_block,
           prev=None):
    tk = _pick(n_keys, (1280, 640, 256))
    q_block = (None,) * (q.ndim - 2) + (tq * q_tiles, LANES)
    o_block = (None,) * (len(out_shape.shape) - 2) + (tq * q_tiles, LANES)
    in_specs = [
        pl.BlockSpec(q_block, q_map),
        pl.BlockSpec((n_keys, LANES), lambda g, r: (key_block, kv_map(g)),
                     pipeline_mode=pl.Buffered(1)),
        pl.BlockSpec((n_keys, 2 * LANES), lambda g, r: (key_block, kv_map(g)),
                     pipeline_mode=pl.Buffered(1)),
    ]
    args = [q, k, v]
    aliases = {}
    if prev is not None:
        in_specs.append(pl.BlockSpec(memory_space=pl.ANY))
        args.append(prev)
        aliases = {3: 0}
    return pl.pallas_call(
        functools.partial(_flash_kernel, tq=tq, tk=tk, unroll=FLASH_UNROLL),
        grid=grid,
        in_specs=in_specs,
        out_specs=pl.BlockSpec(o_block, o_map),
        out_shape=out_shape,
        scratch_shapes=[pltpu.VMEM((tq, tk), F32), pltpu.VMEM((tq, tk), F32),
                        pltpu.VMEM((tq, 1), F32), pltpu.VMEM((tq, 2 * LANES), F32)],
        input_output_aliases=aliases,
        compiler_params=_params("arbitrary", "arbitrary"),
    )(*args)


def _query_tiling(n_lat):
    tq = _pick(n_lat, (1024, 512, 256))
    q_tiles = _pick(n_lat // tq, (8, 4, 2, 1))
    return tq, q_tiles, n_lat // (tq * q_tiles)


def _diff_attention(dq, dk, dv, n_lat, with_ctx):
    n = dk.shape[0]
    n_ctx = n - n_lat
    tq, q_tiles, nb = _query_tiling(n_lat)
    shape = jax.ShapeDtypeStruct((2, n, HEADS * LANES), F32)
    o = _flash(dq, dk, dv, shape,
               lambda g, r: (r // nb, r % nb, g), lambda g: g, lambda g, r: (r // nb, r % nb, g),
               (HEADS, 2 * nb), tq, q_tiles, n, 0)
    if with_ctx:
        cb = n_lat // n_ctx
        o = _flash(dq, dk, dv, shape,
                   lambda g, r: (r, cb, g), lambda g: g, lambda g, r: (r, cb, g),
                   (HEADS, 2), n_ctx, 1, n_ctx, cb, prev=o)
    return o


def _gqa_attention(gq, gk, gv, n_lat, with_ctx):
    n = gk.shape[0]
    n_ctx = n - n_lat
    tq, q_tiles, nb = _query_tiling(n_lat)
    shape = jax.ShapeDtypeStruct((n, HEADS * LANES), BF16)
    o = _flash(gq, gk, gv, shape,
               lambda g, r: (r % nb, g * GQA_GROUP + r // nb), lambda g: g,
               lambda g, r: (r % nb, g * GQA_GROUP + r // nb),
               (GQA_KV, GQA_GROUP * nb), tq, q_tiles, n, 0)
    if with_ctx:
        cb = n_lat // n_ctx
        o = _flash(gq, gk, gv, shape,
                   lambda g, r: (cb, g * GQA_GROUP + r), lambda g: g,
                   lambda g, r: (cb, g * GQA_GROUP + r),
                   (GQA_KV, GQA_GROUP), n_ctx, 1, n_ctx, cb, prev=o)
    return o


def _finish_kernel(lam_ref, x_ref, mod_ref, gates_ref, rg_ref, dg_ref, gg_ref, ro_ref, do_ref,
                   go_ref, sub_ref, wb_ref, wo_ref, ng_ref, *rest, n_lat, tm, d, lambda_init,
                   final_norm):
    lp = lam_ref[...]
    lam = (jnp.exp(jnp.sum(lp[0:1] * lp[1:2], axis=-1, keepdims=True))
           - jnp.exp(jnp.sum(lp[2:3] * lp[3:4], axis=-1, keepdims=True)) + lambda_init)

    def head_norm(o):
        return o * lax.rsqrt(jnp.mean(o * o, axis=-1, keepdims=True) + EPS)

    branches = []
    for h in range(HEADS):
        sl = slice(h * LANES, (h + 1) * LANES)
        r = head_norm(ro_ref[:, sl]) * _silu(rg_ref[:, sl].astype(F32))
        dd = head_norm(do_ref[0, :, sl] - lam * do_ref[1, :, sl]) * sub_ref[...] * (1.0 - lambda_init)
        dd = dd * _silu(dg_ref[:, sl].astype(F32))
        g = go_ref[:, sl].astype(F32) * _silu(gg_ref[:, sl].astype(F32))
        branches.append((r.astype(BF16), dd.astype(BF16), g.astype(BF16)))

    merged = jnp.zeros((tm, d), F32)
    for b in range(3):
        br = jnp.concatenate([branches[h][b] for h in range(HEADS)], axis=1)
        y = jnp.dot(br, wb_ref[b], preferred_element_type=F32)
        merged = merged + _sigmoid(gates_ref[:, b * d:(b + 1) * d].astype(F32)) * y
    out = jnp.dot(merged.astype(BF16), wo_ref[...], preferred_element_type=F32)

    first_row = pl.program_id(0) * tm
    rows = first_row + lax.broadcasted_iota(jnp.int32, (tm, 1), 0)
    gate = _row_mod(mod_ref, 2 * d, 3 * d, rows >= n_lat)
    x = x_ref[...] + gate * out
    if final_norm:
        (o_ref,) = rest
        o_ref[...] = x * lax.rsqrt(jnp.mean(x * x, axis=-1, keepdims=True) + EPS) * ng_ref[...]
    else:
        next_mod_ref, o_ref, h_ref = rest
        o_ref[...] = x
        h_ref[...] = _modulated_norm(x, ng_ref[...], next_mod_ref, first_row, n_lat, d)


def _finish(xx, z, mods, lam_params, ro, do, go, subln, wb, wo, next_gain, layer, n_lat,
            lambda_init, last):
    n, d = xx.shape
    tm = 256
    n_rows = n_lat if last else n
    bw = HEADS * LANES

    def rows(width, off=0):
        return pl.BlockSpec((tm, width), lambda i, b=off // width: (i, b))

    const2 = lambda i: (0, 0)
    kern = functools.partial(_finish_kernel, n_lat=n_lat, tm=tm, d=d, lambda_init=lambda_init,
                             final_norm=last)
    if last:
        extra_specs, extra_args = [], []
        out_specs = rows(d)
        out_shape = jax.ShapeDtypeStruct((n_rows, d), F32)
    else:
        extra_specs = [pl.BlockSpec((None, 8, 3 * d), lambda i: (layer + 1, 0, 0))]
        extra_args = [mods]
        out_specs = [rows(d), rows(d)]
        out_shape = [jax.ShapeDtypeStruct((n_rows, d), F32), jax.ShapeDtypeStruct((n_rows, d), BF16)]
    return pl.pallas_call(
        kern,
        grid=(n_rows // tm,),
        in_specs=[
            pl.BlockSpec((4, KEY_W), const2),
            rows(d),
            pl.BlockSpec((None, 8, 3 * d), lambda i: (layer, 0, 0)),
            rows(3 * d, Z_GATES), rows(bw, Z_RG), rows(bw, Z_DG), rows(bw, Z_GG),
            rows(bw),
            pl.BlockSpec((2, tm, bw), lambda i: (0, i, 0)),
            rows(bw),
            pl.BlockSpec((1, LANES), const2),
            pl.BlockSpec((None, 3, bw, d), lambda i: (layer, 0, 0, 0),
                         pipeline_mode=pl.Buffered(1)),
            pl.BlockSpec((None, d, d), lambda i: (layer, 0, 0), pipeline_mode=pl.Buffered(1)),
            pl.BlockSpec((1, d), const2),
        ] + extra_specs,
        out_specs=out_specs,
        out_shape=out_shape,
        compiler_params=_params("arbitrary"),
    )(lam_params, xx, mods, z, z, z, z, ro, do, go, subln.reshape(1, LANES), wb, wo,
      next_gain.reshape(1, d), *extra_args)


def kernel(x, c, ctx, c_ctx, norm_gain, w_ada, b_ada, w_in, ret_log_rate, diff_lambda,
           diff_subln_gain, gqa_q_gain, gqa_k_gain, w_branch, w_out, final_norm_gain):
    _, n_lat, d = x.shape
    n_ctx = ctx.shape[1]
    depth = w_in.shape[0]
    assert x.shape[0] == 1 and d == 2048 and w_in.shape[2] == Z_COLS
    assert n_lat % n_ctx == 0 and n_ctx % CHUNK == 0 and n_lat % GRID_W == 0

    xx = jnp.concatenate([x[0], ctx[0]], axis=0)
    c8 = jnp.concatenate([c, c_ctx[None], jnp.zeros((6, d), F32)], axis=0)
    mods = _ada_all(c8, w_ada, b_ada)
    tabs = _rope_tables(n_lat, n_ctx)
    wb16 = w_branch.astype(BF16)
    wo16 = w_out.astype(BF16)

    h = _norm_mod(xx, mods, norm_gain[0], 0, n_lat)
    for l in range(depth):
        last = l == depth - 1
        lambda_init = 0.8 - 0.6 * math.exp(-0.3 * l)
        z = _in_proj(h, w_in, l)
        rq, rk, dq, dk, dv, gq, gk, gv = _prep(z, tabs, gqa_q_gain[l], gqa_k_gain[l])
        ro = _retention(rq, rk, z, ret_log_rate[l], n_lat)
        do = _diff_attention(dq, dk, dv, n_lat, not last)
        go = _gqa_attention(gq, gk, gv, n_lat, not last)
        next_gain = final_norm_gain if last else norm_gain[l + 1]
        out = _finish(xx, z, mods, diff_lambda[l], ro, do, go, diff_subln_gain[l], wb16, wo16,
                      next_gain, l, n_lat, lambda_init, last)
        if last:
            return out[None]
        xx, h = out
```

```python
import functools
import math

import jax
import jax.numpy as jnp
from jax import lax
from jax.experimental import pallas as pl
from jax.experimental.pallas import tpu as pltpu

F32 = jnp.float32
BF16 = jnp.bfloat16

EPS = 1e-6
ROPE_THETA = 10000.0
GRID_W = 64
LANES = 128
VMEM_LIMIT = 56 * 1024 * 1024

HEADS = 8
HEAD_W = 128
KEY_W = 64
GQA_KV = 2
GQA_GROUP = 4
CHUNK = 128

Z_GATES, Z_RQ, Z_RK, Z_RV, Z_RG = 0, 6144, 6656, 7168, 8192
Z_DQ, Z_DK, Z_DV, Z_DG = 9216, 10240, 11264, 12288
Z_GQ, Z_GG, Z_GK, Z_GV = 13312, 14336, 15360, 15616
Z_COLS = 15872
W_TILE = 512
NORM_ROWS = 128
LOG2_E = math.log2(math.e)


def _pick(n, candidates):
    for c in candidates:
        if n % c == 0:
            return c
    raise ValueError(f"no tile in {candidates} divides {n}")


def _params(*sem):
    return pltpu.CompilerParams(dimension_semantics=sem, vmem_limit_bytes=VMEM_LIMIT)


def _sigmoid(x):
    return 0.5 * jnp.tanh(0.5 * x) + 0.5


def _silu(x):
    return x * _sigmoid(x)


def _ada_kernel(c_ref, w_ref, b_ref, o_ref):
    s = _silu(c_ref[...])
    o_ref[...] = jnp.dot(s, w_ref[...], preferred_element_type=F32,
                         precision=lax.Precision.HIGHEST) + b_ref[...]


def _ada_all(c8, w_ada, b_ada):
    depth, d, d3 = w_ada.shape
    tn = 1024
    return pl.pallas_call(
        _ada_kernel,
        grid=(depth, d3 // tn),
        in_specs=[
            pl.BlockSpec((8, d), lambda l, j: (0, 0)),
            pl.BlockSpec((None, d, tn), lambda l, j: (l, 0, j)),
            pl.BlockSpec((None, 1, tn), lambda l, j: (l, 0, j)),
        ],
        out_specs=pl.BlockSpec((None, 8, tn), lambda l, j: (l, 0, j)),
        out_shape=jax.ShapeDtypeStruct((depth, 8, d3), F32),
        compiler_params=_params("arbitrary", "arbitrary"),
    )(c8, w_ada, b_ada.reshape(depth, 1, d3))


def _row_mod(mod_ref, lo, hi, is_ctx):
    return jnp.where(is_ctx, mod_ref[1:2, lo:hi], mod_ref[0:1, lo:hi])


def _modulated_norm(x, gain, mod_ref, first_row, n_lat, d):
    y = x * lax.rsqrt(jnp.mean(x * x, axis=-1, keepdims=True) + EPS) * gain
    rows = first_row + lax.broadcasted_iota(jnp.int32, (x.shape[0], 1), 0)
    is_ctx = rows >= n_lat
    shift = _row_mod(mod_ref, 0, d, is_ctx)
    scale = _row_mod(mod_ref, d, 2 * d, is_ctx)
    return (y * (1.0 + scale) + shift).astype(BF16)


def _norm_kernel(x_ref, mod_ref, g_ref, h_ref, *, n_lat, tm, d):
    h_ref[...] = _modulated_norm(x_ref[...], g_ref[...], mod_ref, pl.program_id(0) * tm, n_lat, d)


def _norm_mod(xx, mods, gain, layer, n_lat):
    n, d = xx.shape
    tm = NORM_ROWS
    return pl.pallas_call(
        functools.partial(_norm_kernel, n_lat=n_lat, tm=tm, d=d),
        grid=(n // tm,),
        in_specs=[
            pl.BlockSpec((tm, d), lambda i: (i, 0)),
            pl.BlockSpec((None, 8, 3 * d), lambda i: (layer, 0, 0)),
            pl.BlockSpec((1, d), lambda i: (0, 0)),
        ],
        out_specs=pl.BlockSpec((tm, d), lambda i: (i, 0)),
        out_shape=jax.ShapeDtypeStruct((n, d), BF16),
        compiler_params=_params("arbitrary"),
    )(xx, mods, gain.reshape(1, d))


def _inproj_kernel(h_ref, w_ref, z_ref):
    z_ref[...] = jnp.dot(h_ref[...], w_ref[...].astype(BF16),
                         preferred_element_type=F32).astype(BF16)


def _w_block(j):
    return jnp.where(j < 12, j + 19, jnp.where(j < 28, j - 12, jnp.where(j < 30, j - 11, 16)))


def _in_proj(h, w, layer):
    n, d = h.shape
    tm = _pick(n, (3328, 1280, 640, 256))
    return pl.pallas_call(
        _inproj_kernel,
        grid=(n // tm, Z_COLS // W_TILE),
        in_specs=[
            pl.BlockSpec((tm, d), lambda i, j: (i, 0)),
            pl.BlockSpec((None, d, W_TILE), lambda i, j: (layer, 0, _w_block(j))),
        ],
        out_specs=pl.BlockSpec((tm, W_TILE), lambda i, j: (i, j)),
        out_shape=jax.ShapeDtypeStruct((n, Z_COLS), BF16),
        compiler_params=_params("arbitrary", "arbitrary"),
    )(h, w)


def _swap_halves(x, half, in_first_half):
    if 2 * half == LANES:
        return pltpu.roll(x, half, 1)
    return jnp.where(in_first_half, pltpu.roll(x, LANES - half, 1), pltpu.roll(x, half, 1))


def _rope(x, c, s, half, in_first_half=None):
    return x * c + _swap_halves(x, half, in_first_half) * s


def _prep_kernel(rq_ref, rk_ref, dq_ref, dk_ref, dv_ref, gq_ref, gk_ref, gv_ref,
                 sc_ref, ss_ref, ac_ref, as_ref, bc_ref, bs_ref, qg_ref, kg_ref,
                 orq_ref, ork_ref, odq_ref, odk_ref, odv_ref, ogq_ref, ogk_ref, ogv_ref):
    tp = rq_ref.shape[0]
    lane = lax.broadcasted_iota(jnp.int32, (tp, LANES), 1)
    low = lane < KEY_W
    pair_low = lane % KEY_W < KEY_W // 2
    sc, ss = sc_ref[...], ss_ref[...]
    ac, as_ = ac_ref[...], as_ref[...]
    bc, bs = bc_ref[...], bs_ref[...]
    ones = jnp.ones((tp, LANES), BF16)

    k_scale = KEY_W ** -0.5
    for p in range(HEADS // 2):
        sl = slice(p * LANES, (p + 1) * LANES)
        for src, dst, mul in ((rq_ref, orq_ref, 1.0), (rk_ref, ork_ref, k_scale)):
            y = _rope(src[:, sl].astype(F32), sc, ss, KEY_W // 2, pair_low) * mul
            dst[:, (2 * p) * LANES:(2 * p + 1) * LANES] = jnp.where(low, y, 0.0).astype(BF16)
            dst[:, (2 * p + 1) * LANES:(2 * p + 2) * LANES] = jnp.where(
                low, pltpu.roll(y, KEY_W, 1), 0.0).astype(BF16)

    d_scale = KEY_W ** -0.5 * LOG2_E
    for h in range(HEADS):
        sl = slice(h * LANES, (h + 1) * LANES)
        q = _rope(dq_ref[:, sl].astype(F32), ac, as_, KEY_W // 2, pair_low) * d_scale
        odq_ref[0, :, sl] = jnp.where(low, q, 0.0).astype(BF16)
        odq_ref[1, :, sl] = jnp.where(low, 0.0, q).astype(BF16)
        odk_ref[:, sl] = _rope(dk_ref[:, sl].astype(F32), ac, as_, KEY_W // 2,
                               pair_low).astype(BF16)
        odv_ref[:, (2 * h) * LANES:(2 * h + 1) * LANES] = dv_ref[:, sl]
        odv_ref[:, (2 * h + 1) * LANES:(2 * h + 2) * LANES] = ones

    g_scale = HEAD_W ** -0.5 * LOG2_E

    def normed(x, gain):
        return x * lax.rsqrt(jnp.mean(x * x, axis=-1, keepdims=True) + EPS) * gain

    for h in range(HEADS):
        sl = slice(h * LANES, (h + 1) * LANES)
        q = _rope(normed(gq_ref[:, sl].astype(F32), qg_ref[...]), bc, bs, HEAD_W // 2)
        ogq_ref[:, sl] = (q * g_scale).astype(BF16)
    for h in range(GQA_KV):
        sl = slice(h * LANES, (h + 1) * LANES)
        k = _rope(normed(gk_ref[:, sl].astype(F32), kg_ref[...]), bc, bs, HEAD_W // 2)
        ogk_ref[:, sl] = k.astype(BF16)
        ogv_ref[:, (2 * h) * LANES:(2 * h + 1) * LANES] = gv_ref[:, sl]
        ogv_ref[:, (2 * h + 1) * LANES:(2 * h + 2) * LANES] = ones


def _prep(z, tabs, q_gain, k_gain):
    n = z.shape[0]
    tp = _pick(n, (640, 256))

    def zspec(off, width):
        return pl.BlockSpec((tp, width), lambda i, b=off // width: (i, b))

    tab = pl.BlockSpec((tp, LANES), lambda i: (i, 0))
    vec = pl.BlockSpec((1, LANES), lambda i: (0, 0))

    def ospec(width):
        return pl.BlockSpec((tp, width), lambda i: (i, 0))

    return pl.pallas_call(
        _prep_kernel,
        grid=(n // tp,),
        in_specs=[zspec(Z_RQ, 512), zspec(Z_RK, 512), zspec(Z_DQ, 1024), zspec(Z_DK, 1024),
                  zspec(Z_DV, 1024), zspec(Z_GQ, 1024), zspec(Z_GK, 256), zspec(Z_GV, 256),
                  tab, tab, tab, tab, tab, tab, vec, vec],
        out_specs=[ospec(1024), ospec(1024),
                   pl.BlockSpec((2, tp, 1024), lambda i: (0, i, 0)),
                   ospec(1024), ospec(2048), ospec(1024), ospec(256), ospec(512)],
        out_shape=[jax.ShapeDtypeStruct((n, 1024), BF16), jax.ShapeDtypeStruct((n, 1024), BF16),
                   jax.ShapeDtypeStruct((2, n, 1024), BF16), jax.ShapeDtypeStruct((n, 1024), BF16),
                   jax.ShapeDtypeStruct((n, 2048), BF16), jax.ShapeDtypeStruct((n, 1024), BF16),
                   jax.ShapeDtypeStruct((n, 256), BF16), jax.ShapeDtypeStruct((n, 512), BF16)],
        compiler_params=_params("arbitrary"),
    )(z, z, z, z, z, z, z, z, *tabs, q_gain.reshape(1, LANES), k_gain.reshape(1, LANES))


def _rope_tables(n_lat, n_ctx):
    def pattern(cos, sin):
        reps = LANES // (2 * cos.shape[1])
        c = jnp.tile(jnp.concatenate([cos, cos], axis=1), (1, reps))
        s = jnp.tile(jnp.concatenate([-sin, sin], axis=1), (1, reps))
        c = jnp.concatenate([c, jnp.ones((n_ctx, LANES), F32)], axis=0)
        s = jnp.concatenate([s, jnp.zeros((n_ctx, LANES), F32)], axis=0)
        return c, s

    def axial(head_dim):
        n_rows = n_lat // GRID_W
        rows = jnp.repeat(jnp.arange(n_rows), GRID_W).astype(F32)
        cols = jnp.tile(jnp.arange(GRID_W), n_rows).astype(F32)
        n_freq = head_dim // 4
        freqs = ROPE_THETA ** (-jnp.arange(n_freq, dtype=F32) / n_freq)
        ang = jnp.concatenate([rows[:, None] * freqs, cols[:, None] * freqs], axis=-1)
        return jnp.cos(ang), jnp.sin(ang)

    freqs = 1.0 / (ROPE_THETA ** jnp.linspace(0.0, 1.0, KEY_W // 2, dtype=F32))
    ang = jnp.arange(n_lat, dtype=F32)[:, None] * freqs
    return (*pattern(jnp.cos(ang), jnp.sin(ang)), *pattern(*axial(KEY_W)), *pattern(*axial(HEAD_W)))


def _log_decay(lr_ref, direction, h, shape):
    return -jnp.exp(jnp.full(shape, lr_ref[direction, h], F32))


def _ret_sum_kernel(lr_ref, k_ref, v_ref, kv_ref):
    h = pl.program_id(0)
    j = lax.broadcasted_iota(jnp.int32, (CHUNK, LANES), 0).astype(F32)
    w_f = jnp.exp(_log_decay(lr_ref, 0, h, (CHUNK, LANES)) * (CHUNK - 1 - j))
    w_b = jnp.exp(_log_decay(lr_ref, 1, h, (CHUNK, LANES)) * j)
    for c in range(k_ref.shape[0] // CHUNK):
        rows = slice(c * CHUNK, (c + 1) * CHUNK)
        k = k_ref[rows, :].astype(F32)
        kk = (k * w_f + pltpu.roll(k * w_b, KEY_W, 1)).T.astype(BF16)
        kv_ref[c] = jnp.dot(kk, v_ref[rows, :], preferred_element_type=F32)


def _ret_scan_kernel(lr_ref, kv_ref, st_ref, *, n_lat_chunks):
    h = pl.program_id(0)
    nc = kv_ref.shape[0]
    shape = (KEY_W, LANES)
    g_f = jnp.exp(_log_decay(lr_ref, 0, h, shape) * CHUNK)
    g_b = jnp.exp(_log_decay(lr_ref, 1, h, shape) * CHUNK)

    def fwd(c, s):
        st_ref[c, 0:KEY_W, :] = s.astype(BF16)
        return g_f * s + kv_ref[c, 0:KEY_W, :]

    def bwd(t, s):
        c = nc - 1 - t
        st_ref[c, KEY_W:, :] = s.astype(BF16)
        return g_b * s + kv_ref[c, KEY_W:, :]

    zero = jnp.zeros(shape, F32)
    s = lax.fori_loop(n_lat_chunks, nc, fwd, zero)
    lax.fori_loop(0, n_lat_chunks, fwd, s)
    lax.fori_loop(0, nc, bwd, zero)


def _ret_out_kernel(lr_ref, q_ref, k_ref, v_ref, st_ref, o_ref):
    h = pl.program_id(0)
    i = lax.broadcasted_iota(jnp.int32, (CHUNK, CHUNK), 0)
    j = lax.broadcasted_iota(jnp.int32, (CHUNK, CHUNK), 1)
    rel = (i - j).astype(F32)
    lg_f = _log_decay(lr_ref, 0, h, (CHUNK, CHUNK))
    lg_b = _log_decay(lr_ref, 1, h, (CHUNK, CHUNK))
    decay = jnp.where(i >= j, jnp.exp(lg_f * jnp.maximum(rel, 0.0)),
                      jnp.exp(lg_b * jnp.maximum(-rel, 0.0)))
    pos = i.astype(F32)
    cross_f = jnp.exp(lg_f * (pos + 1.0))
    cross_b = jnp.exp(lg_b * (CHUNK - pos))
    for c in range(q_ref.shape[0] // CHUNK):
        rows = slice(c * CHUNK, (c + 1) * CHUNK)
        q = q_ref[rows, :]
        att = lax.dot_general(q, k_ref[rows, :], (((1,), (1,)), ((), ())),
                              preferred_element_type=F32) * decay
        qf = q.astype(F32)
        qs = (qf * cross_f + pltpu.roll(qf * cross_b, KEY_W, 1)).astype(BF16)
        lhs = jnp.concatenate([att.astype(BF16), qs], axis=1)
        rhs = jnp.concatenate([v_ref[rows, :], st_ref[c]], axis=0)
        o_ref[rows, :] = jnp.dot(lhs, rhs, preferred_element_type=F32)


def _retention(rq, rk, z, log_rate, n_lat):
    n = z.shape[0]
    nc = n // CHUNK
    tr = _pick(n, (3328, 1280, 640, 256))
    cpt = tr // CHUNK
    smem = pl.BlockSpec(memory_space=pltpu.SMEM)
    head_rows = pl.BlockSpec((tr, LANES), lambda h, i: (i, h))
    v_rows = pl.BlockSpec((tr, LANES), lambda h, i: (i, Z_RV // LANES + h))
    chunk_mats = pl.BlockSpec((None, cpt, CHUNK, LANES), lambda h, i: (h, i, 0, 0))

    kv = pl.pallas_call(
        _ret_sum_kernel,
        grid=(HEADS, n // tr),
        in_specs=[smem, head_rows, v_rows],
        out_specs=chunk_mats,
        out_shape=jax.ShapeDtypeStruct((HEADS, nc, CHUNK, LANES), F32),
        compiler_params=_params("arbitrary", "arbitrary"),
    )(log_rate, rk, z)

    all_chunks = pl.BlockSpec((None, nc, CHUNK, LANES), lambda h: (h, 0, 0, 0))
    st = pl.pallas_call(
        functools.partial(_ret_scan_kernel, n_lat_chunks=n_lat // CHUNK),
        grid=(HEADS,),
        in_specs=[smem, all_chunks],
        out_specs=all_chunks,
        out_shape=jax.ShapeDtypeStruct((HEADS, nc, CHUNK, LANES), BF16),
        compiler_params=_params("arbitrary"),
    )(log_rate, kv)

    return pl.pallas_call(
        _ret_out_kernel,
        grid=(HEADS, n // tr),
        in_specs=[smem, head_rows, head_rows, v_rows, chunk_mats],
        out_specs=head_rows,
        out_shape=jax.ShapeDtypeStruct((n, HEADS * LANES), F32),
        compiler_params=_params("arbitrary", "arbitrary"),
    )(log_rate, rq, rk, z, st)


def _flash_kernel(q_ref, k_ref, v_ref, *rest, tq, tk, unroll):
    o_ref, s0_ref, s1_ref, m_ref, acc_ref = rest[-5:]
    bufs = (s0_ref, s1_ref)
    nk = k_ref.shape[0] // tk
    n_steps = (q_ref.shape[0] // tq) * nk

    def q_rows(qt):
        return pl.ds(pl.multiple_of(qt * tq, tq), tq)

    def key_rows(c):
        return pl.ds(pl.multiple_of(c * tk, tk), tk)

    def scores(qt, c, dst):
        dst[...] = lax.dot_general(q_ref[q_rows(qt), :], k_ref[key_rows(c), :],
                                   (((1,), (1,)), ((), ())), preferred_element_type=F32)

    def consume(qt, c, src):
        s = src[...]
        m = jnp.where(c == 0, -jnp.inf, m_ref[...])
        m_new = jnp.maximum(m, jnp.max(s, axis=-1, keepdims=True))
        p = jnp.exp2(s - m_new)
        acc = jnp.exp2(m - m_new) * acc_ref[...] + jnp.dot(
            p.astype(BF16), v_ref[key_rows(c), :], preferred_element_type=F32)
        acc_ref[...] = acc
        m_ref[...] = m_new
        o_ref[q_rows(qt), :] = (acc[:, :LANES] / acc[:, LANES:]).astype(o_ref.dtype)

    def step(parity, qt, c):
        wrap = c == nk - 1
        qt_next = jnp.where(wrap, qt + 1, qt)
        c_next = jnp.where(wrap, 0, c + 1)
        scores(qt_next, c_next, bufs[1 - parity])
        consume(qt, c, bufs[parity])
        return qt_next, c_next

    def group(_, carry):
        qt, c = carry
        for u in range(unroll):
            qt, c = step(u % 2, qt, c)
        return qt, c

    m_ref[...] = jnp.full(m_ref.shape, -jnp.inf, F32)
    acc_ref[...] = jnp.zeros(acc_ref.shape, F32)
    zero = jnp.int32(0)
    scores(zero, zero, s0_ref)
    qt, c = lax.fori_loop(0, (n_steps - 1) // unroll, group, (zero, zero))
    for u in range((n_steps - 1) % unroll):
        qt, c = step(u % 2, qt, c)
    consume(qt, c, bufs[(n_steps - 1) % 2])


FLASH_UNROLL = 6


def _flash(q, k, v, out_shape, q_map, kv_map, o_map, grid, tq, q_tiles, n_keys, key_block,
           prev=None):
    tk = _pick(n_keys, (1280, 640, 256))
    q_block = (None,) * (q.ndim - 2) + (tq * q_tiles, LANES)
    o_block = (None,) * (len(out_shape.shape) - 2) + (tq * q_tiles, LANES)
    in_specs = [
        pl.BlockSpec(q_block, q_map),
        pl.BlockSpec((n_keys, LANES), lambda g, r: (key_block, kv_map(g)),
                     pipeline_mode=pl.Buffered(1)),
        pl.BlockSpec((n_keys, 2 * LANES), lambda g, r: (key_block, kv_map(g)),
                     pipeline_mode=pl.Buffered(1)),
    ]
    args = [q, k, v]
    aliases = {}
    if prev is not None:
        in_specs.append(pl.BlockSpec(memory_space=pl.ANY))
        args.append(prev)
        aliases = {3: 0}
    return pl.pallas_call(
        functools.partial(_flash_kernel, tq=tq, tk=tk, unroll=FLASH_UNROLL),
        grid=grid,
        in_specs=in_specs,
        out_specs=pl.BlockSpec(o_block, o_map),
        out_shape=out_shape,
        scratch_shapes=[pltpu.VMEM((tq, tk), F32), pltpu.VMEM((tq, tk), F32),
                        pltpu.VMEM((tq, 1), F32), pltpu.VMEM((tq, 2 * LANES), F32)],
        input_output_aliases=aliases,
        compiler_params=_params("arbitrary", "arbitrary"),
    )(*args)


def _query_tiling(n_lat):
    tq = _pick(n_lat, (1024, 512, 256))
    q_tiles = _pick(n_lat // tq, (8, 4, 2, 1))
    return tq, q_tiles, n_lat // (tq * q_tiles)


def _diff_attention(dq, dk, dv, n_lat, with_ctx):
    n = dk.shape[0]
    n_ctx = n - n_lat
    tq, q_tiles, nb = _query_tiling(n_lat)
    shape = jax.ShapeDtypeStruct((2, n, HEADS * LANES), F32)
    o = _flash(dq, dk, dv, shape,
               lambda g, r: (r // nb, r % nb, g), lambda g: g, lambda g, r: (r // nb, r % nb, g),
               (HEADS, 2 * nb), tq, q_tiles, n, 0)
    if with_ctx:
        cb = n_lat // n_ctx
        o = _flash(dq, dk, dv, shape,
                   lambda g, r: (r, cb, g), lambda g: g, lambda g, r: (r, cb, g),
                   (HEADS, 2), n_ctx, 1, n_ctx, cb, prev=o)
    return o


def _gqa_attention(gq, gk, gv, n_lat, with_ctx):
    n = gk.shape[0]
    n_ctx = n - n_lat
    tq, q_tiles, nb = _query_tiling(n_lat)
    shape = jax.ShapeDtypeStruct((n, HEADS * LANES), BF16)
    o = _flash(gq, gk, gv, shape,
               lambda g, r: (r % nb, g * GQA_GROUP + r // nb), lambda g: g,
               lambda g, r: (r % nb, g * GQA_GROUP + r // nb),
               (GQA_KV, GQA_GROUP * nb), tq, q_tiles, n, 0)
    if with_ctx:
        cb = n_lat // n_ctx
        o = _flash(gq, gk, gv, shape,
                   lambda g, r: (cb, g * GQA_GROUP + r), lambda g: g,
                   lambda g, r: (cb, g * GQA_GROUP + r),
                   (GQA_KV, GQA_GROUP), n_ctx, 1, n_ctx, cb, prev=o)
    return o


def _finish_kernel(lam_ref, x_ref, mod_ref, gates_ref, rg_ref, dg_ref, gg_ref, ro_ref, do_ref,
                   go_ref, sub_ref, wb_ref, wo_ref, ng_ref, *rest, n_lat, tm, d, lambda_init,
                   final_norm):
    lp = lam_ref[...]
    lam = (jnp.exp(jnp.sum(lp[0:1] * lp[1:2], axis=-1, keepdims=True))
           - jnp.exp(jnp.sum(lp[2:3] * lp[3:4], axis=-1, keepdims=True)) + lambda_init)

    def head_norm(o):
        return o * lax.rsqrt(jnp.mean(o * o, axis=-1, keepdims=True) + EPS)

    branches = []
    for h in range(HEADS):
        sl = slice(h * LANES, (h + 1) * LANES)
        r = head_norm(ro_ref[:, sl]) * _silu(rg_ref[:, sl].astype(F32))
        dd = head_norm(do_ref[0, :, sl] - lam * do_ref[1, :, sl]) * sub_ref[...] * (1.0 - lambda_init)
        dd = dd * _silu(dg_ref[:, sl].astype(F32))
        g = go_ref[:, sl].astype(F32) * _silu(gg_ref[:, sl].astype(F32))
        branches.append((r.astype(BF16), dd.astype(BF16), g.astype(BF16)))

    merged = jnp.zeros((tm, d), F32)
    for b in range(3):
        br = jnp.concatenate([branches[h][b] for h in range(HEADS)], axis=1)
        y = jnp.dot(br, wb_ref[b], preferred_element_type=F32)
        merged = merged + _sigmoid(gates_ref[:, b * d:(b + 1) * d].astype(F32)) * y
    out = jnp.dot(merged.astype(BF16), wo_ref[...], preferred_element_type=F32)

    first_row = pl.program_id(0) * tm
    rows = first_row + lax.broadcasted_iota(jnp.int32, (tm, 1), 0)
    gate = _row_mod(mod_ref, 2 * d, 3 * d, rows >= n_lat)
    x = x_ref[...] + gate * out
    if final_norm:
        (o_ref,) = rest
        o_ref[...] = x * lax.rsqrt(jnp.mean(x * x, axis=-1, keepdims=True) + EPS) * ng_ref[...]
    else:
        next_mod_ref, o_ref, h_ref = rest
        o_ref[...] = x
        h_ref[...] = _modulated_norm(x, ng_ref[...], next_mod_ref, first_row, n_lat, d)


def _finish(xx, z, mods, lam_params, ro, do, go, subln, wb, wo, next_gain, layer, n_lat,
            lambda_init, last):
    n, d = xx.shape
    tm = 256
    n_rows = n_lat if last else n
    bw = HEADS * LANES

    def rows(width, off=0):
        return pl.BlockSpec((tm, width), lambda i, b=off // width: (i, b))

    const2 = lambda i: (0, 0)
    kern = functools.partial(_finish_kernel, n_lat=n_lat, tm=tm, d=d, lambda_init=lambda_init,
                             final_norm=last)
    if last:
        extra_specs, extra_args = [], []
        out_specs = rows(d)
        out_shape = jax.ShapeDtypeStruct((n_rows, d), F32)
    else:
        extra_specs = [pl.BlockSpec((None, 8, 3 * d), lambda i: (layer + 1, 0, 0))]
        extra_args = [mods]
        out_specs = [rows(d), rows(d)]
        out_shape = [jax.ShapeDtypeStruct((n_rows, d), F32), jax.ShapeDtypeStruct((n_rows, d), BF16)]
    return pl.pallas_call(
        kern,
        grid=(n_rows // tm,),
        in_specs=[
            pl.BlockSpec((4, KEY_W), const2),
            rows(d),
            pl.BlockSpec((None, 8, 3 * d), lambda i: (layer, 0, 0)),
            rows(3 * d, Z_GATES), rows(bw, Z_RG), rows(bw, Z_DG), rows(bw, Z_GG),
            rows(bw),
            pl.BlockSpec((2, tm, bw), lambda i: (0, i, 0)),
            rows(bw),
            pl.BlockSpec((1, LANES), const2),
            pl.BlockSpec((None, 3, bw, d), lambda i: (layer, 0, 0, 0),
                         pipeline_mode=pl.Buffered(1)),
            pl.BlockSpec((None, d, d), lambda i: (layer, 0, 0), pipeline_mode=pl.Buffered(1)),
            pl.BlockSpec((1, d), const2),
        ] + extra_specs,
        out_specs=out_specs,
        out_shape=out_shape,
        compiler_params=_params("arbitrary"),
    )(lam_params, xx, mods, z, z, z, z, ro, do, go, subln.reshape(1, LANES), wb, wo,
      next_gain.reshape(1, d), *extra_args)


def kernel(x, c, ctx, c_ctx, norm_gain, w_ada, b_ada, w_in, ret_log_rate, diff_lambda,
           diff_subln_gain, gqa_q_gain, gqa_k_gain, w_branch, w_out, final_norm_gain):
    _, n_lat, d = x.shape
    n_ctx = ctx.shape[1]
    depth = w_in.shape[0]
    assert x.shape[0] == 1 and d == 2048 and w_in.shape[2] == Z_COLS
    assert n_lat % n_ctx == 0 and n_ctx % CHUNK == 0 and n_lat % GRID_W == 0

    xx = jnp.concatenate([x[0], ctx[0]], axis=0)
    c8 = jnp.concatenate([c, c_ctx[None], jnp.zeros((6, d), F32)], axis=0)
    mods = _ada_all(c8, w_ada, b_ada)
    tabs = _rope_tables(n_lat, n_ctx)
    wb16 = w_branch.astype(BF16)
    wo16 = w_out.astype(BF16)

    h = _norm_mod(xx, mods, norm_gain[0], 0, n_lat)
    for l in range(depth):
        last = l == depth - 1
        lambda_init = 0.8 - 0.6 * math.exp(-0.3 * l)
        z = _in_proj(h, w_in, l)
        rq, rk, dq, dk, dv, gq, gk, gv = _prep(z, tabs, gqa_q_gain[l], gqa_k_gain[l])
        ro = _retention(rq, rk, z, ret_log_rate[l], n_lat)
        do = _diff_attention(dq, dk, dv, n_lat, not last)
        go = _gqa_attention(gq, gk, gv, n_lat, not last)
        next_gain = final_norm_gain if last else norm_gain[l + 1]
        out = _finish(xx, z, mods, diff_lambda[l], ro, do, go, diff_subln_gain[l], wb16, wo16,
                      next_gain, l, n_lat, lambda_init, last)
        if last:
            return out[None]
        xx, h = out
```

```python
import functools
import math

import jax
import jax.numpy as jnp
from jax import lax
from jax.experimental import pallas as pl
from jax.experimental.pallas import tpu as pltpu

F32 = jnp.float32
BF16 = jnp.bfloat16

EPS = 1e-6
ROPE_THETA = 10000.0
GRID_W = 64
LANES = 128
VMEM_LIMIT = 56 * 1024 * 1024

HEADS = 8
HEAD_W = 128
KEY_W = 64
GQA_KV = 2
GQA_GROUP = 4
CHUNK = 128

Z_GATES, Z_RQ, Z_RK, Z_RV, Z_RG = 0, 6144, 6656, 7168, 8192
Z_DQ, Z_DK, Z_DV, Z_DG = 9216, 10240, 11264, 12288
Z_GQ, Z_GG, Z_GK, Z_GV = 13312, 14336, 15360, 15616
Z_COLS = 15872
W_TILE = 512
NORM_ROWS = 128
LOG2_E = math.log2(math.e)


def _pick(n, candidates):
    for c in candidates:
        if n % c == 0:
            return c
    raise ValueError(f"no tile in {candidates} divides {n}")


def _params(*sem):
    return pltpu.CompilerParams(dimension_semantics=sem, vmem_limit_bytes=VMEM_LIMIT)


def _sigmoid(x):
    return 0.5 * jnp.tanh(0.5 * x) + 0.5


def _silu(x):
    return x * _sigmoid(x)


def _ada_kernel(c_ref, w_ref, b_ref, o_ref):
    s = _silu(c_ref[...])
    o_ref[...] = jnp.dot(s, w_ref[...], preferred_element_type=F32,
                         precision=lax.Precision.HIGHEST) + b_ref[...]


def _ada_all(c8, w_ada, b_ada):
    depth, d, d3 = w_ada.shape
    tn = 1024
    return pl.pallas_call(
        _ada_kernel,
        grid=(depth, d3 // tn),
        in_specs=[
            pl.BlockSpec((8, d), lambda l, j: (0, 0)),
            pl.BlockSpec((None, d, tn), lambda l, j: (l, 0, j)),
            pl.BlockSpec((None, 1, tn), lambda l, j: (l, 0, j)),
        ],
        out_specs=pl.BlockSpec((None, 8, tn), lambda l, j: (l, 0, j)),
        out_shape=jax.ShapeDtypeStruct((depth, 8, d3), F32),
        compiler_params=_params("arbitrary", "arbitrary"),
    )(c8, w_ada, b_ada.reshape(depth, 1, d3))


def _row_mod(mod_ref, lo, hi, is_ctx):
    return jnp.where(is_ctx, mod_ref[1:2, lo:hi], mod_ref[0:1, lo:hi])


def _modulated_norm(x, gain, mod_ref, first_row, n_lat, d):
    y = x * lax.rsqrt(jnp.mean(x * x, axis=-1, keepdims=True) + EPS) * gain
    rows = first_row + lax.broadcasted_iota(jnp.int32, (x.shape[0], 1), 0)
    is_ctx = rows >= n_lat
    shift = _row_mod(mod_ref, 0, d, is_ctx)
    scale = _row_mod(mod_ref, d, 2 * d, is_ctx)
    return (y * (1.0 + scale) + shift).astype(BF16)


def _norm_kernel(x_ref, mod_ref, g_ref, h_ref, *, n_lat, tm, d):
    h_ref[...] = _modulated_norm(x_ref[...], g_ref[...], mod_ref, pl.program_id(0) * tm, n_lat, d)


def _norm_mod(xx, mods, gain, layer, n_lat):
    n, d = xx.shape
    tm = NORM_ROWS
    return pl.pallas_call(
        functools.partial(_norm_kernel, n_lat=n_lat, tm=tm, d=d),
        grid=(n // tm,),
        in_specs=[
            pl.BlockSpec((tm, d), lambda i: (i, 0)),
            pl.BlockSpec((None, 8, 3 * d), lambda i: (layer, 0, 0)),
            pl.BlockSpec((1, d), lambda i: (0, 0)),
        ],
        out_specs=pl.BlockSpec((tm, d), lambda i: (i, 0)),
        out_shape=jax.ShapeDtypeStruct((n, d), BF16),
        compiler_params=_params("arbitrary"),
    )(xx, mods, gain.reshape(1, d))


def _inproj_kernel(h_ref, w_ref, z_ref):
    z_ref[...] = jnp.dot(h_ref[...], w_ref[...].astype(BF16),
                         preferred_element_type=F32).astype(BF16)


def _w_block(j):
    return jnp.where(j < 12, j + 19, jnp.where(j < 28, j - 12, jnp.where(j < 30, j - 11, 16)))


def _in_proj(h, w, layer):
    n, d = h.shape
    tm = _pick(n, (3328, 1280, 640, 256))
    return pl.pallas_call(
        _inproj_kernel,
        grid=(n // tm, Z_COLS // W_TILE),
        in_specs=[
            pl.BlockSpec((tm, d), lambda i, j: (i, 0)),
            pl.BlockSpec((None, d, W_TILE), lambda i, j: (layer, 0, _w_block(j))),
        ],
        out_specs=pl.BlockSpec((tm, W_TILE), lambda i, j: (i, j)),
        out_shape=jax.ShapeDtypeStruct((n, Z_COLS), BF16),
        compiler_params=_params("arbitrary", "arbitrary"),
    )(h, w)


def _swap_halves(x, half, in_first_half):
    if 2 * half == LANES:
        return pltpu.roll(x, half, 1)
    return jnp.where(in_first_half, pltpu.roll(x, LANES - half, 1), pltpu.roll(x, half, 1))


def _rope(x, c, s, half, in_first_half=None):
    return x * c + _swap_halves(x, half, in_first_half) * s


def _prep_kernel(rq_ref, rk_ref, dq_ref, dk_ref, dv_ref, gq_ref, gk_ref, gv_ref,
                 sc_ref, ss_ref, ac_ref, as_ref, bc_ref, bs_ref, qg_ref, kg_ref,
                 orq_ref, ork_ref, odq_ref, odk_ref, odv_ref, ogq_ref, ogk_ref, ogv_ref):
    tp = rq_ref.shape[0]
    lane = lax.broadcasted_iota(jnp.int32, (tp, LANES), 1)
    low = lane < KEY_W
    pair_low = lane % KEY_W < KEY_W // 2
    sc, ss = sc_ref[...], ss_ref[...]
    ac, as_ = ac_ref[...], as_ref[...]
    bc, bs = bc_ref[...], bs_ref[...]
    ones = jnp.ones((tp, LANES), BF16)

    k_scale = KEY_W ** -0.5
    for p in range(HEADS // 2):
        sl = slice(p * LANES, (p + 1) * LANES)
        for src, dst, mul in ((rq_ref, orq_ref, 1.0), (rk_ref, ork_ref, k_scale)):
            y = _rope(src[:, sl].astype(F32), sc, ss, KEY_W // 2, pair_low) * mul
            dst[:, (2 * p) * LANES:(2 * p + 1) * LANES] = jnp.where(low, y, 0.0).astype(BF16)
            dst[:, (2 * p + 1) * LANES:(2 * p + 2) * LANES] = jnp.where(
                low, pltpu.roll(y, KEY_W, 1), 0.0).astype(BF16)

    d_scale = KEY_W ** -0.5 * LOG2_E
    for h in range(HEADS):
        sl = slice(h * LANES, (h + 1) * LANES)
        q = _rope(dq_ref[:, sl].astype(F32), ac, as_, KEY_W // 2, pair_low) * d_scale
        odq_ref[0, :, sl] = jnp.where(low, q, 0.0).astype(BF16)
        odq_ref[1, :, sl] = jnp.where(low, 0.0, q).astype(BF16)
        odk_ref[:, sl] = _rope(dk_ref[:, sl].astype(F32), ac, as_, KEY_W // 2,
                               pair_low).astype(BF16)
        odv_ref[:, (2 * h) * LANES:(2 * h + 1) * LANES] = dv_ref[:, sl]
        odv_ref[:, (2 * h + 1) * LANES:(2 * h + 2) * LANES] = ones

    g_scale = HEAD_W ** -0.5 * LOG2_E

    def normed(x, gain):
        return x * lax.rsqrt(jnp.mean(x * x, axis=-1, keepdims=True) + EPS) * gain

    for h in range(HEADS):
        sl = slice(h * LANES, (h + 1) * LANES)
        q = _rope(normed(gq_ref[:, sl].astype(F32), qg_ref[...]), bc, bs, HEAD_W // 2)
        ogq_ref[:, sl] = (q * g_scale).astype(BF16)
    for h in range(GQA_KV):
        sl = slice(h * LANES, (h + 1) * LANES)
        k = _rope(normed(gk_ref[:, sl].astype(F32), kg_ref[...]), bc, bs, HEAD_W // 2)
        ogk_ref[:, sl] = k.astype(BF16)
        ogv_ref[:, (2 * h) * LANES:(2 * h + 1) * LANES] = gv_ref[:, sl]
        ogv_ref[:, (2 * h + 1) * LANES:(2 * h + 2) * LANES] = ones


def _prep(z, tabs, q_gain, k_gain):
    n = z.shape[0]
    tp = _pick(n, (640, 256))

    def zspec(off, width):
        return pl.BlockSpec((tp, width), lambda i, b=off // width: (i, b))

    tab = pl.BlockSpec((tp, LANES), lambda i: (i, 0))
    vec = pl.BlockSpec((1, LANES), lambda i: (0, 0))

    def ospec(width):
        return pl.BlockSpec((tp, width), lambda i: (i, 0))

    return pl.pallas_call(
        _prep_kernel,
        grid=(n // tp,),
        in_specs=[zspec(Z_RQ, 512), zspec(Z_RK, 512), zspec(Z_DQ, 1024), zspec(Z_DK, 1024),
                  zspec(Z_DV, 1024), zspec(Z_GQ, 1024), zspec(Z_GK, 256), zspec(Z_GV, 256),
                  tab, tab, tab, tab, tab, tab, vec, vec],
        out_specs=[ospec(1024), ospec(1024),
                   pl.BlockSpec((2, tp, 1024), lambda i: (0, i, 0)),
                   ospec(1024), ospec(2048), ospec(1024), ospec(256), ospec(512)],
        out_shape=[jax.ShapeDtypeStruct((n, 1024), BF16), jax.ShapeDtypeStruct((n, 1024), BF16),
                   jax.ShapeDtypeStruct((2, n, 1024), BF16), jax.ShapeDtypeStruct((n, 1024), BF16),
                   jax.ShapeDtypeStruct((n, 2048), BF16), jax.ShapeDtypeStruct((n, 1024), BF16),
                   jax.ShapeDtypeStruct((n, 256), BF16), jax.ShapeDtypeStruct((n, 512), BF16)],
        compiler_params=_params("arbitrary"),
    )(z, z, z, z, z, z, z, z, *tabs, q_gain.reshape(1, LANES), k_gain.reshape(1, LANES))


def _rope_tables(n_lat, n_ctx):
    def pattern(cos, sin):
        reps = LANES // (2 * cos.shape[1])
        c = jnp.tile(jnp.concatenate([cos, cos], axis=1), (1, reps))
        s = jnp.tile(jnp.concatenate([-sin, sin], axis=1), (1, reps))
        c = jnp.concatenate([c, jnp.ones((n_ctx, LANES), F32)], axis=0)
        s = jnp.concatenate([s, jnp.zeros((n_ctx, LANES), F32)], axis=0)
        return c, s

    def axial(head_dim):
        n_rows = n_lat // GRID_W
        rows = jnp.repeat(jnp.arange(n_rows), GRID_W).astype(F32)
        cols = jnp.tile(jnp.arange(GRID_W), n_rows).astype(F32)
        n_freq = head_dim // 4
        freqs = ROPE_THETA ** (-jnp.arange(n_freq, dtype=F32) / n_freq)
        ang = jnp.concatenate([rows[:, None] * freqs, cols[:, None] * freqs], axis=-1)
        return jnp.cos(ang), jnp.sin(ang)

    freqs = 1.0 / (ROPE_THETA ** jnp.linspace(0.0, 1.0, KEY_W // 2, dtype=F32))
    ang = jnp.arange(n_lat, dtype=F32)[:, None] * freqs
    return (*pattern(jnp.cos(ang), jnp.sin(ang)), *pattern(*axial(KEY_W)), *pattern(*axial(HEAD_W)))


def _log_decay(lr_ref, direction, h, shape):
    return -jnp.exp(jnp.full(shape, lr_ref[direction, h], F32))


def _ret_sum_kernel(lr_ref, k_ref, v_ref, kv_ref):
    h = pl.program_id(0)
    j = lax.broadcasted_iota(jnp.int32, (CHUNK, LANES), 0).astype(F32)
    w_f = jnp.exp(_log_decay(lr_ref, 0, h, (CHUNK, LANES)) * (CHUNK - 1 - j))
    w_b = jnp.exp(_log_decay(lr_ref, 1, h, (CHUNK, LANES)) * j)
    for c in range(k_ref.shape[0] // CHUNK):
        rows = slice(c * CHUNK, (c + 1) * CHUNK)
        k = k_ref[rows, :].astype(F32)
        kk = (k * w_f + pltpu.roll(k * w_b, KEY_W, 1)).T.astype(BF16)
        kv_ref[c] = jnp.dot(kk, v_ref[rows, :], preferred_element_type=F32)


def _ret_scan_kernel(lr_ref, kv_ref, st_ref, *, n_lat_chunks):
    h = pl.program_id(0)
    nc = kv_ref.shape[0]
    shape = (KEY_W, LANES)
    g_f = jnp.exp(_log_decay(lr_ref, 0, h, shape) * CHUNK)
    g_b = jnp.exp(_log_decay(lr_ref, 1, h, shape) * CHUNK)

    def fwd(c, s):
        st_ref[c, 0:KEY_W, :] = s.astype(BF16)
        return g_f * s + kv_ref[c, 0:KEY_W, :]

    def bwd(t, s):
        c = nc - 1 - t
        st_ref[c, KEY_W:, :] = s.astype(BF16)
        return g_b * s + kv_ref[c, KEY_W:, :]

    zero = jnp.zeros(shape, F32)
    s = lax.fori_loop(n_lat_chunks, nc, fwd, zero)
    lax.fori_loop(0, n_lat_chunks, fwd, s)
    lax.fori_loop(0, nc, bwd, zero)


def _ret_out_kernel(lr_ref, q_ref, k_ref, v_ref, st_ref, o_ref):
    h = pl.program_id(0)
    i = lax.broadcasted_iota(jnp.int32, (CHUNK, CHUNK), 0)
    j = lax.broadcasted_iota(jnp.int32, (CHUNK, CHUNK), 1)
    rel = (i - j).astype(F32)
    lg_f = _log_decay(lr_ref, 0, h, (CHUNK, CHUNK))
    lg_b = _log_decay(lr_ref, 1, h, (CHUNK, CHUNK))
    decay = jnp.where(i >= j, jnp.exp(lg_f * jnp.maximum(rel, 0.0)),
                      jnp.exp(lg_b * jnp.maximum(-rel, 0.0)))
    pos = i.astype(F32)
    cross_f = jnp.exp(lg_f * (pos + 1.0))
    cross_b = jnp.exp(lg_b * (CHUNK - pos))
    for c in range(q_ref.shape[0] // CHUNK):
        rows = slice(c * CHUNK, (c + 1) * CHUNK)
        q = q_ref[rows, :]
        att = lax.dot_general(q, k_ref[rows, :], (((1,), (1,)), ((), ())),
                              preferred_element_type=F32) * decay
        qf = q.astype(F32)
        qs = (qf * cross_f + pltpu.roll(qf * cross_b, KEY_W, 1)).astype(BF16)
        lhs = jnp.concatenate([att.astype(BF16), qs], axis=1)
        rhs = jnp.concatenate([v_ref[rows, :], st_ref[c]], axis=0)
        o_ref[rows, :] = jnp.dot(lhs, rhs, preferred_element_type=F32)


def _retention(rq, rk, z, log_rate, n_lat):
    n = z.shape[0]
    nc = n // CHUNK
    tr = _pick(n, (3328, 1280, 640, 256))
    cpt = tr // CHUNK
    smem = pl.BlockSpec(memory_space=pltpu.SMEM)
    head_rows = pl.BlockSpec((tr, LANES), lambda h, i: (i, h))
    v_rows = pl.BlockSpec((tr, LANES), lambda h, i: (i, Z_RV // LANES + h))
    chunk_mats = pl.BlockSpec((None, cpt, CHUNK, LANES), lambda h, i: (h, i, 0, 0))

    kv = pl.pallas_call(
        _ret_sum_kernel,
        grid=(HEADS, n // tr),
        in_specs=[smem, head_rows, v_rows],
        out_specs=chunk_mats,
        out_shape=jax.ShapeDtypeStruct((HEADS, nc, CHUNK, LANES), F32),
        compiler_params=_params("arbitrary", "arbitrary"),
    )(log_rate, rk, z)

    all_chunks = pl.BlockSpec((None, nc, CHUNK, LANES), lambda h: (h, 0, 0, 0))
    st = pl.pallas_call(
        functools.partial(_ret_scan_kernel, n_lat_chunks=n_lat // CHUNK),
        grid=(HEADS,),
        in_specs=[smem, all_chunks],
        out_specs=all_chunks,
        out_shape=jax.ShapeDtypeStruct((HEADS, nc, CHUNK, LANES), BF16),
        compiler_params=_params("arbitrary"),
    )(log_rate, kv)

    return pl.pallas_call(
        _ret_out_kernel,
        grid=(HEADS, n // tr),
        in_specs=[smem, head_rows, head_rows, v_rows, chunk_mats],
        out_specs=head_rows,
        out_shape=jax.ShapeDtypeStruct((n, HEADS * LANES), F32),
        compiler_params=_params("arbitrary", "arbitrary"),
    )(log_rate, rq, rk, z, st)


def _flash_kernel(q_ref, k_ref, v_ref, *rest, tq, tk, unroll):
    o_ref, s0_ref, s1_ref, m_ref, acc_ref = rest[-5:]
    bufs = (s0_ref, s1_ref)
    nk = k_ref.shape[0] // tk
    n_steps = (q_ref.shape[0] // tq) * nk

    def q_rows(qt):
        return pl.ds(pl.multiple_of(qt * tq, tq), tq)

    def key_rows(c):
        return pl.ds(pl.multiple_of(c * tk, tk), tk)

    def scores(qt, c, dst):
        dst[...] = lax.dot_general(q_ref[q_rows(qt), :], k_ref[key_rows(c), :],
                                   (((1,), (1,)), ((), ())), preferred_element_type=F32)

    def consume(qt, c, src):
        s = src[...]
        m = jnp.where(c == 0, -jnp.inf, m_ref[...])
        m_new = jnp.maximum(m, jnp.max(s, axis=-1, keepdims=True))
        p = jnp.exp2(s - m_new)
        acc = jnp.exp2(m - m_new) * acc_ref[...] + jnp.dot(
            p.astype(BF16), v_ref[key_rows(c), :], preferred_element_type=F32)
        acc_ref[...] = acc
        m_ref[...] = m_new
        o_ref[q_rows(qt), :] = (acc[:, :LANES] / acc[:, LANES:]).astype(o_ref.dtype)

    def step(parity, qt, c):
        wrap = c == nk - 1
        qt_next = jnp.where(wrap, qt + 1, qt)
        c_next = jnp.where(wrap, 0, c + 1)
        scores(qt_next, c_next, bufs[1 - parity])
        consume(qt, c, bufs[parity])
        return qt_next, c_next

    def group(_, carry):
        qt, c = carry
        for u in range(unroll):
            qt, c = step(u % 2, qt, c)
        return qt, c

    m_ref[...] = jnp.full(m_ref.shape, -jnp.inf, F32)
    acc_ref[...] = jnp.zeros(acc_ref.shape, F32)
    zero = jnp.int32(0)
    scores(zero, zero, s0_ref)
    qt, c = lax.fori_loop(0, (n_steps - 1) // unroll, group, (zero, zero))
    for u in range((n_steps - 1) % unroll):
        qt, c = step(u % 2, qt, c)
    consume(qt, c, bufs[(n_steps - 1) % 2])


FLASH_UNROLL = 6


def _flash(q, k, v, out_shape, q_map, kv_map, o_map, grid, tq, q_tiles, n_keys, key_block,
           prev=None):
    tk = _pick(n_keys, (1280, 640, 256))
    q_block = (None,) * (q.ndim - 2) + (tq * q_tiles, LANES)
    o_block = (None,) * (len(out_shape.shape) - 2) + (tq * q_tiles, LANES)
    in_specs = [
        pl.BlockSpec(q_block, q_map),
        pl.BlockSpec((n_keys, LANES), lambda g, r: (key_block, kv_map(g)),
                     pipeline_mode=pl.Buffered(1)),
        pl.BlockSpec((n_keys, 2 * LANES), lambda g, r: (key_block, kv_map(g)),
                     pipeline_mode=pl.Buffered(1)),
    ]
    args = [q, k, v]
    aliases = {}
    if prev is not None:
        in_specs.append(pl.BlockSpec(memory_space=pl.ANY))
        args.append(prev)
        aliases = {3: 0}
    return pl.pallas_call(
        functools.partial(_flash_kernel, tq=tq, tk=tk, unroll=FLASH_UNROLL),
        grid=grid,
        in_specs=in_specs,
        out_specs=pl.BlockSpec(o_block, o_map),
        out_shape=out_shape,
        scratch_shapes=[pltpu.VMEM((tq, tk), F32), pltpu.VMEM((tq, tk), F32),
                        pltpu.VMEM((tq, 1), F32), pltpu.VMEM((tq, 2 * LANES), F32)],
        input_output_aliases=aliases,
        compiler_params=_params("arbitrary", "arbitrary"),
    )(*args)


def _query_tiling(n_lat):
    tq = _pick(n_lat, (1024, 512, 256))
    q_tiles = _pick(n_lat // tq, (8, 4, 2, 1))
    return tq, q_tiles, n_lat // (tq * q_tiles)


def _diff_attention(dq, dk, dv, n_lat, with_ctx):
    n = dk.shape[0]
    n_ctx = n - n_lat
    tq, q_tiles, nb = _query_tiling(n_lat)
    shape = jax.ShapeDtypeStruct((2, n, HEADS * LANES), F32)
    o = _flash(dq, dk, dv, shape,
               lambda g, r: (r // nb, r % nb, g), lambda g: g, lambda g, r: (r // nb, r % nb, g),
               (HEADS, 2 * nb), tq, q_tiles, n, 0)
    if with_ctx:
        cb = n_lat // n_ctx
        o = _flash(dq, dk, dv, shape,
                   lambda g, r: (r, cb, g), lambda g: g, lambda g, r: (r, cb, g),
                   (HEADS, 2), n_ctx, 1, n_ctx, cb, prev=o)
    return o


def _gqa_attention(gq, gk, gv, n_lat, with_ctx):
    n = gk.shape[0]
    n_ctx = n - n_lat
    tq, q_tiles, nb = _query_tiling(n_lat)
    shape = jax.ShapeDtypeStruct((n, HEADS * LANES), BF16)
    o = _flash(gq, gk, gv, shape,
               lambda g, r: (r % nb, g * GQA_GROUP + r // nb), lambda g: g,
               lambda g, r: (r % nb, g * GQA_GROUP + r // nb),
               (GQA_KV, GQA_GROUP * nb), tq, q_tiles, n, 0)
    if with_ctx:
        cb = n_lat // n_ctx
        o = _flash(gq, gk, gv, shape,
                   lambda g, r: (cb, g * GQA_GROUP + r), lambda g: g,
                   lambda g, r: (cb, g * GQA_GROUP + r),
                   (GQA_KV, GQA_GROUP), n_ctx, 1, n_ctx, cb, prev=o)
    return o


def _finish_kernel(lam_ref, x_ref, mod_ref, gates_ref, rg_ref, dg_ref, gg_ref, ro_ref, do_ref,
                   go_ref, sub_ref, wb_ref, wo_ref, ng_ref, *rest, n_lat, tm, d, lambda_init,
                   final_norm):
    lp = lam_ref[...]
    lam = (jnp.exp(jnp.sum(lp[0:1] * lp[1:2], axis=-1, keepdims=True))
           - jnp.exp(jnp.sum(lp[2:3] * lp[3:4], axis=-1, keepdims=True)) + lambda_init)

    def head_norm(o):
        return o * lax.rsqrt(jnp.mean(o * o, axis=-1, keepdims=True) + EPS)

    def retention_head(sl):
        return head_norm(ro_ref[:, sl]) * _silu(rg_ref[:, sl].astype(F32))

    def diff_head(sl):
        o = head_norm(do_ref[0, :, sl] - lam * do_ref[1, :, sl]) * sub_ref[...]
        return o * (1.0 - lambda_init) * _silu(dg_ref[:, sl].astype(F32))

    def gqa_head(sl):
        return go_ref[:, sl].astype(F32) * _silu(gg_ref[:, sl].astype(F32))

    merged = jnp.zeros((tm, d), F32)
    for b, head in enumerate((retention_head, diff_head, gqa_head)):
        br = jnp.concatenate([head(slice(h * LANES, (h + 1) * LANES)).astype(BF16)
                              for h in range(HEADS)], axis=1)
        y = jnp.dot(br, wb_ref[b], preferred_element_type=F32)
        merged = merged + _sigmoid(gates_ref[:, b * d:(b + 1) * d].astype(F32)) * y
    out = jnp.dot(merged.astype(BF16), wo_ref[...], preferred_element_type=F32)

    first_row = pl.program_id(0) * tm
    rows = first_row + lax.broadcasted_iota(jnp.int32, (tm, 1), 0)
    gate = _row_mod(mod_ref, 2 * d, 3 * d, rows >= n_lat)
    x = x_ref[...] + gate * out
    if final_norm:
        (o_ref,) = rest
        o_ref[...] = x * lax.rsqrt(jnp.mean(x * x, axis=-1, keepdims=True) + EPS) * ng_ref[...]
    else:
        next_mod_ref, o_ref, h_ref = rest
        o_ref[...] = x
        h_ref[...] = _modulated_norm(x, ng_ref[...], next_mod_ref, first_row, n_lat, d)


def _finish(xx, z, mods, lam_params, ro, do, go, subln, wb, wo, next_gain, layer, n_lat,
            lambda_init, last):
    n, d = xx.shape
    tm = 256
    n_rows = n_lat if last else n
    bw = HEADS * LANES

    def rows(width, off=0):
        return pl.BlockSpec((tm, width), lambda i, b=off // width: (i, b))

    const2 = lambda i: (0, 0)
    kern = functools.partial(_finish_kernel, n_lat=n_lat, tm=tm, d=d, lambda_init=lambda_init,
                             final_norm=last)
    if last:
        extra_specs, extra_args = [], []
        out_specs = rows(d)
        out_shape = jax.ShapeDtypeStruct((n_rows, d), F32)
    else:
        extra_specs = [pl.BlockSpec((None, 8, 3 * d), lambda i: (layer + 1, 0, 0))]
        extra_args = [mods]
        out_specs = [rows(d), rows(d)]
        out_shape = [jax.ShapeDtypeStruct((n_rows, d), F32), jax.ShapeDtypeStruct((n_rows, d), BF16)]
    return pl.pallas_call(
        kern,
        grid=(n_rows // tm,),
        in_specs=[
            pl.BlockSpec((4, KEY_W), const2),
            rows(d),
            pl.BlockSpec((None, 8, 3 * d), lambda i: (layer, 0, 0)),
            rows(3 * d, Z_GATES), rows(bw, Z_RG), rows(bw, Z_DG), rows(bw, Z_GG),
            rows(bw),
            pl.BlockSpec((2, tm, bw), lambda i: (0, i, 0)),
            rows(bw),
            pl.BlockSpec((1, LANES), const2),
            pl.BlockSpec((None, 3, bw, d), lambda i: (layer, 0, 0, 0),
                         pipeline_mode=pl.Buffered(1)),
            pl.BlockSpec((None, d, d), lambda i: (layer, 0, 0), pipeline_mode=pl.Buffered(1)),
            pl.BlockSpec((1, d), const2),
        ] + extra_specs,
        out_specs=out_specs,
        out_shape=out_shape,
        compiler_params=_params("arbitrary"),
    )(lam_params, xx, mods, z, z, z, z, ro, do, go, subln.reshape(1, LANES), wb, wo,
      next_gain.reshape(1, d), *extra_args)


def kernel(x, c, ctx, c_ctx, norm_gain, w_ada, b_ada, w_in, ret_log_rate, diff_lambda,
           diff_subln_gain, gqa_q_gain, gqa_k_gain, w_branch, w_out, final_norm_gain):
    _, n_lat, d = x.shape
    n_ctx = ctx.shape[1]
    depth = w_in.shape[0]
    assert x.shape[0] == 1 and d == 2048 and w_in.shape[2] == Z_COLS
    assert n_lat % n_ctx == 0 and n_ctx % CHUNK == 0 and n_lat % GRID_W == 0

    xx = jnp.concatenate([x[0], ctx[0]], axis=0)
    c8 = jnp.concatenate([c, c_ctx[None], jnp.zeros((6, d), F32)], axis=0)
    mods = _ada_all(c8, w_ada, b_ada)
    tabs = _rope_tables(n_lat, n_ctx)
    wb16 = w_branch.astype(BF16)
    wo16 = w_out.astype(BF16)

    h = _norm_mod(xx, mods, norm_gain[0], 0, n_lat)
    for l in range(depth):
        last = l == depth - 1
        lambda_init = 0.8 - 0.6 * math.exp(-0.3 * l)
        z = _in_proj(h, w_in, l)
        rq, rk, dq, dk, dv, gq, gk, gv = _prep(z, tabs, gqa_q_gain[l], gqa_k_gain[l])
        ro = _retention(rq, rk, z, ret_log_rate[l], n_lat)
        do = _diff_attention(dq, dk, dv, n_lat, not last)
        go = _gqa_attention(gq, gk, gv, n_lat, not last)
        next_gain = final_norm_gain if last else norm_gain[l + 1]
        out = _finish(xx, z, mods, diff_lambda[l], ro, do, go, diff_subln_gain[l], wb16, wo16,
                      next_gain, l, n_lat, lambda_init, last)
        if last:
            return out[None]
        xx, h = out
```

```python
import functools
import math

import jax
import jax.numpy as jnp
from jax import lax
from jax.experimental import pallas as pl
from jax.experimental.pallas import tpu as pltpu

F32 = jnp.float32
BF16 = jnp.bfloat16

EPS = 1e-6
ROPE_THETA = 10000.0
GRID_W = 64
LANES = 128
VMEM_LIMIT = 56 * 1024 * 1024

HEADS = 8
HEAD_W = 128
KEY_W = 64
GQA_KV = 2
GQA_GROUP = 4
CHUNK = 128

Z_GATES, Z_RQ, Z_RK, Z_RV, Z_RG = 0, 6144, 6656, 7168, 8192
Z_DQ, Z_DK, Z_DV, Z_DG = 9216, 10240, 11264, 12288
Z_GQ, Z_GG, Z_GK, Z_GV = 13312, 14336, 15360, 15616
Z_COLS = 15872
W_TILE = 512
NORM_ROWS = 128
LOG2_E = math.log2(math.e)


def _pick(n, candidates):
    for c in candidates:
        if n % c == 0:
            return c
    raise ValueError(f"no tile in {candidates} divides {n}")


def _params(*sem):
    return pltpu.CompilerParams(dimension_semantics=sem, vmem_limit_bytes=VMEM_LIMIT)


def _sigmoid(x):
    return 0.5 * jnp.tanh(0.5 * x) + 0.5


def _silu(x):
    return x * _sigmoid(x)


def _ada_kernel(c_ref, w_ref, b_ref, o_ref):
    s = _silu(c_ref[...])
    o_ref[...] = jnp.dot(s, w_ref[...], preferred_element_type=F32,
                         precision=lax.Precision.HIGHEST) + b_ref[...]


def _ada_all(c8, w_ada, b_ada):
    depth, d, d3 = w_ada.shape
    tn = 1024
    return pl.pallas_call(
        _ada_kernel,
        grid=(depth, d3 // tn),
        in_specs=[
            pl.BlockSpec((8, d), lambda l, j: (0, 0)),
            pl.BlockSpec((None, d, tn), lambda l, j: (l, 0, j)),
            pl.BlockSpec((None, 1, tn), lambda l, j: (l, 0, j)),
        ],
        out_specs=pl.BlockSpec((None, 8, tn), lambda l, j: (l, 0, j)),
        out_shape=jax.ShapeDtypeStruct((depth, 8, d3), F32),
        compiler_params=_params("arbitrary", "arbitrary"),
    )(c8, w_ada, b_ada.reshape(depth, 1, d3))


def _row_mod(mod_ref, lo, hi, is_ctx):
    return jnp.where(is_ctx, mod_ref[1:2, lo:hi], mod_ref[0:1, lo:hi])


def _modulated_norm(x, gain, mod_ref, first_row, n_lat, d):
    y = x * lax.rsqrt(jnp.mean(x * x, axis=-1, keepdims=True) + EPS) * gain
    rows = first_row + lax.broadcasted_iota(jnp.int32, (x.shape[0], 1), 0)
    is_ctx = rows >= n_lat
    shift = _row_mod(mod_ref, 0, d, is_ctx)
    scale = _row_mod(mod_ref, d, 2 * d, is_ctx)
    return (y * (1.0 + scale) + shift).astype(BF16)


def _norm_kernel(x_ref, mod_ref, g_ref, h_ref, *, n_lat, tm, d):
    h_ref[...] = _modulated_norm(x_ref[...], g_ref[...], mod_ref, pl.program_id(0) * tm, n_lat, d)


def _norm_mod(xx, mods, gain, layer, n_lat):
    n, d = xx.shape
    tm = NORM_ROWS
    return pl.pallas_call(
        functools.partial(_norm_kernel, n_lat=n_lat, tm=tm, d=d),
        grid=(n // tm,),
        in_specs=[
            pl.BlockSpec((tm, d), lambda i: (i, 0)),
            pl.BlockSpec((None, 8, 3 * d), lambda i: (layer, 0, 0)),
            pl.BlockSpec((1, d), lambda i: (0, 0)),
        ],
        out_specs=pl.BlockSpec((tm, d), lambda i: (i, 0)),
        out_shape=jax.ShapeDtypeStruct((n, d), BF16),
        compiler_params=_params("arbitrary"),
    )(xx, mods, gain.reshape(1, d))


def _inproj_kernel(h_ref, w_ref, z_ref):
    z_ref[...] = jnp.dot(h_ref[...], w_ref[...].astype(BF16),
                         preferred_element_type=F32).astype(BF16)


def _w_block(j):
    return jnp.where(j < 12, j + 19, jnp.where(j < 28, j - 12, jnp.where(j < 30, j - 11, 16)))


def _in_proj(h, w, layer):
    n, d = h.shape
    tm = _pick(n, (3328, 1280, 640, 256))
    return pl.pallas_call(
        _inproj_kernel,
        grid=(n // tm, Z_COLS // W_TILE),
        in_specs=[
            pl.BlockSpec((tm, d), lambda i, j: (i, 0)),
            pl.BlockSpec((None, d, W_TILE), lambda i, j: (layer, 0, _w_block(j))),
        ],
        out_specs=pl.BlockSpec((tm, W_TILE), lambda i, j: (i, j)),
        out_shape=jax.ShapeDtypeStruct((n, Z_COLS), BF16),
        compiler_params=_params("arbitrary", "arbitrary"),
    )(h, w)


def _swap_halves(x, half, in_first_half):
    if 2 * half == LANES:
        return pltpu.roll(x, half, 1)
    return jnp.where(in_first_half, pltpu.roll(x, LANES - half, 1), pltpu.roll(x, half, 1))


def _rope(x, c, s, half, in_first_half=None):
    return x * c + _swap_halves(x, half, in_first_half) * s


def _prep_kernel(rq_ref, rk_ref, dq_ref, dk_ref, dv_ref, gq_ref, gk_ref, gv_ref,
                 sc_ref, ss_ref, ac_ref, as_ref, bc_ref, bs_ref, qg_ref, kg_ref,
                 orq_ref, ork_ref, odq_ref, odk_ref, odv_ref, ogq_ref, ogk_ref, ogv_ref):
    tp = rq_ref.shape[0]
    lane = lax.broadcasted_iota(jnp.int32, (tp, LANES), 1)
    low = lane < KEY_W
    pair_low = lane % KEY_W < KEY_W // 2
    sc, ss = sc_ref[...], ss_ref[...]
    ac, as_ = ac_ref[...], as_ref[...]
    bc, bs = bc_ref[...], bs_ref[...]
    ones = jnp.ones((tp, LANES), BF16)

    k_scale = KEY_W ** -0.5
    for p in range(HEADS // 2):
        sl = slice(p * LANES, (p + 1) * LANES)
        for src, dst, mul in ((rq_ref, orq_ref, 1.0), (rk_ref, ork_ref, k_scale)):
            y = _rope(src[:, sl].astype(F32), sc, ss, KEY_W // 2, pair_low) * mul
            dst[:, (2 * p) * LANES:(2 * p + 1) * LANES] = jnp.where(low, y, 0.0).astype(BF16)
            dst[:, (2 * p + 1) * LANES:(2 * p + 2) * LANES] = jnp.where(
                low, pltpu.roll(y, KEY_W, 1), 0.0).astype(BF16)

    d_scale = KEY_W ** -0.5 * LOG2_E
    for h in range(HEADS):
        sl = slice(h * LANES, (h + 1) * LANES)
        q = _rope(dq_ref[:, sl].astype(F32), ac, as_, KEY_W // 2, pair_low) * d_scale
        odq_ref[0, :, sl] = jnp.where(low, q, 0.0).astype(BF16)
        odq_ref[1, :, sl] = jnp.where(low, 0.0, q).astype(BF16)
        odk_ref[:, sl] = _rope(dk_ref[:, sl].astype(F32), ac, as_, KEY_W // 2,
                               pair_low).astype(BF16)
        odv_ref[:, (2 * h) * LANES:(2 * h + 1) * LANES] = dv_ref[:, sl]
        odv_ref[:, (2 * h + 1) * LANES:(2 * h + 2) * LANES] = ones

    g_scale = HEAD_W ** -0.5 * LOG2_E

    def normed(x, gain):
        return x * lax.rsqrt(jnp.mean(x * x, axis=-1, keepdims=True) + EPS) * gain

    for h in range(HEADS):
        sl = slice(h * LANES, (h + 1) * LANES)
        q = _rope(normed(gq_ref[:, sl].astype(F32), qg_ref[...]), bc, bs, HEAD_W // 2)
        ogq_ref[:, sl] = (q * g_scale).astype(BF16)
    for h in range(GQA_KV):
        sl = slice(h * LANES, (h + 1) * LANES)
        k = _rope(normed(gk_ref[:, sl].astype(F32), kg_ref[...]), bc, bs, HEAD_W // 2)
        ogk_ref[:, sl] = k.astype(BF16)
        ogv_ref[:, (2 * h) * LANES:(2 * h + 1) * LANES] = gv_ref[:, sl]
        ogv_ref[:, (2 * h + 1) * LANES:(2 * h + 2) * LANES] = ones


def _prep(z, tabs, q_gain, k_gain):
    n = z.shape[0]
    tp = _pick(n, (640, 256))

    def zspec(off, width):
        return pl.BlockSpec((tp, width), lambda i, b=off // width: (i, b))

    tab = pl.BlockSpec((tp, LANES), lambda i: (i, 0))
    vec = pl.BlockSpec((1, LANES), lambda i: (0, 0))

    def ospec(width):
        return pl.BlockSpec((tp, width), lambda i: (i, 0))

    return pl.pallas_call(
        _prep_kernel,
        grid=(n // tp,),
        in_specs=[zspec(Z_RQ, 512), zspec(Z_RK, 512), zspec(Z_DQ, 1024), zspec(Z_DK, 1024),
                  zspec(Z_DV, 1024), zspec(Z_GQ, 1024), zspec(Z_GK, 256), zspec(Z_GV, 256),
                  tab, tab, tab, tab, tab, tab, vec, vec],
        out_specs=[ospec(1024), ospec(1024),
                   pl.BlockSpec((2, tp, 1024), lambda i: (0, i, 0)),
                   ospec(1024), ospec(2048), ospec(1024), ospec(256), ospec(512)],
        out_shape=[jax.ShapeDtypeStruct((n, 1024), BF16), jax.ShapeDtypeStruct((n, 1024), BF16),
                   jax.ShapeDtypeStruct((2, n, 1024), BF16), jax.ShapeDtypeStruct((n, 1024), BF16),
                   jax.ShapeDtypeStruct((n, 2048), BF16), jax.ShapeDtypeStruct((n, 1024), BF16),
                   jax.ShapeDtypeStruct((n, 256), BF16), jax.ShapeDtypeStruct((n, 512), BF16)],
        compiler_params=_params("arbitrary"),
    )(z, z, z, z, z, z, z, z, *tabs, q_gain.reshape(1, LANES), k_gain.reshape(1, LANES))


def _rope_tables(n_lat, n_ctx):
    def pattern(cos, sin):
        reps = LANES // (2 * cos.shape[1])
        c = jnp.tile(jnp.concatenate([cos, cos], axis=1), (1, reps))
        s = jnp.tile(jnp.concatenate([-sin, sin], axis=1), (1, reps))
        c = jnp.concatenate([c, jnp.ones((n_ctx, LANES), F32)], axis=0)
        s = jnp.concatenate([s, jnp.zeros((n_ctx, LANES), F32)], axis=0)
        return c, s

    def axial(head_dim):
        n_rows = n_lat // GRID_W
        rows = jnp.repeat(jnp.arange(n_rows), GRID_W).astype(F32)
        cols = jnp.tile(jnp.arange(GRID_W), n_rows).astype(F32)
        n_freq = head_dim // 4
        freqs = ROPE_THETA ** (-jnp.arange(n_freq, dtype=F32) / n_freq)
        ang = jnp.concatenate([rows[:, None] * freqs, cols[:, None] * freqs], axis=-1)
        return jnp.cos(ang), jnp.sin(ang)

    freqs = 1.0 / (ROPE_THETA ** jnp.linspace(0.0, 1.0, KEY_W // 2, dtype=F32))
    ang = jnp.arange(n_lat, dtype=F32)[:, None] * freqs
    return (*pattern(jnp.cos(ang), jnp.sin(ang)), *pattern(*axial(KEY_W)), *pattern(*axial(HEAD_W)))


def _log_decay(lr_ref, direction, h, shape):
    return -jnp.exp(jnp.full(shape, lr_ref[direction, h], F32))


def _ret_sum_kernel(lr_ref, k_ref, v_ref, kv_ref):
    h = pl.program_id(0)
    j = lax.broadcasted_iota(jnp.int32, (CHUNK, LANES), 0).astype(F32)
    w_f = jnp.exp(_log_decay(lr_ref, 0, h, (CHUNK, LANES)) * (CHUNK - 1 - j))
    w_b = jnp.exp(_log_decay(lr_ref, 1, h, (CHUNK, LANES)) * j)
    for c in range(k_ref.shape[0] // CHUNK):
        rows = slice(c * CHUNK, (c + 1) * CHUNK)
        k = k_ref[rows, :].astype(F32)
        kk = (k * w_f + pltpu.roll(k * w_b, KEY_W, 1)).T.astype(BF16)
        kv_ref[c] = jnp.dot(kk, v_ref[rows, :], preferred_element_type=F32)


def _ret_scan_kernel(lr_ref, kv_ref, st_ref, *, n_lat_chunks):
    h = pl.program_id(0)
    nc = kv_ref.shape[0]
    shape = (KEY_W, LANES)
    g_f = jnp.exp(_log_decay(lr_ref, 0, h, shape) * CHUNK)
    g_b = jnp.exp(_log_decay(lr_ref, 1, h, shape) * CHUNK)

    def fwd(c, s):
        st_ref[c, 0:KEY_W, :] = s.astype(BF16)
        return g_f * s + kv_ref[c, 0:KEY_W, :]

    def bwd(t, s):
        c = nc - 1 - t
        st_ref[c, KEY_W:, :] = s.astype(BF16)
        return g_b * s + kv_ref[c, KEY_W:, :]

    zero = jnp.zeros(shape, F32)
    s = lax.fori_loop(n_lat_chunks, nc, fwd, zero)
    lax.fori_loop(0, n_lat_chunks, fwd, s)
    lax.fori_loop(0, nc, bwd, zero)


def _ret_out_kernel(lr_ref, q_ref, k_ref, v_ref, st_ref, o_ref):
    h = pl.program_id(0)
    i = lax.broadcasted_iota(jnp.int32, (CHUNK, CHUNK), 0)
    j = lax.broadcasted_iota(jnp.int32, (CHUNK, CHUNK), 1)
    rel = (i - j).astype(F32)
    lg_f = _log_decay(lr_ref, 0, h, (CHUNK, CHUNK))
    lg_b = _log_decay(lr_ref, 1, h, (CHUNK, CHUNK))
    decay = jnp.where(i >= j, jnp.exp(lg_f * jnp.maximum(rel, 0.0)),
                      jnp.exp(lg_b * jnp.maximum(-rel, 0.0)))
    pos = i.astype(F32)
    cross_f = jnp.exp(lg_f * (pos + 1.0))
    cross_b = jnp.exp(lg_b * (CHUNK - pos))
    for c in range(q_ref.shape[0] // CHUNK):
        rows = slice(c * CHUNK, (c + 1) * CHUNK)
        q = q_ref[rows, :]
        att = lax.dot_general(q, k_ref[rows, :], (((1,), (1,)), ((), ())),
                              preferred_element_type=F32) * decay
        qf = q.astype(F32)
        qs = (qf * cross_f + pltpu.roll(qf * cross_b, KEY_W, 1)).astype(BF16)
        lhs = jnp.concatenate([att.astype(BF16), qs], axis=1)
        rhs = jnp.concatenate([v_ref[rows, :], st_ref[c]], axis=0)
        o_ref[rows, :] = jnp.dot(lhs, rhs, preferred_element_type=F32).astype(o_ref.dtype)


def _retention(rq, rk, z, log_rate, n_lat):
    n = z.shape[0]
    nc = n // CHUNK
    tr = _pick(n, (3328, 1280, 640, 256))
    cpt = tr // CHUNK
    smem = pl.BlockSpec(memory_space=pltpu.SMEM)
    head_rows = pl.BlockSpec((tr, LANES), lambda h, i: (i, h))
    v_rows = pl.BlockSpec((tr, LANES), lambda h, i: (i, Z_RV // LANES + h))
    chunk_mats = pl.BlockSpec((None, cpt, CHUNK, LANES), lambda h, i: (h, i, 0, 0))

    kv = pl.pallas_call(
        _ret_sum_kernel,
        grid=(HEADS, n // tr),
        in_specs=[smem, head_rows, v_rows],
        out_specs=chunk_mats,
        out_shape=jax.ShapeDtypeStruct((HEADS, nc, CHUNK, LANES), F32),
        compiler_params=_params("arbitrary", "arbitrary"),
    )(log_rate, rk, z)

    all_chunks = pl.BlockSpec((None, nc, CHUNK, LANES), lambda h: (h, 0, 0, 0))
    st = pl.pallas_call(
        functools.partial(_ret_scan_kernel, n_lat_chunks=n_lat // CHUNK),
        grid=(HEADS,),
        in_specs=[smem, all_chunks],
        out_specs=all_chunks,
        out_shape=jax.ShapeDtypeStruct((HEADS, nc, CHUNK, LANES), BF16),
        compiler_params=_params("arbitrary"),
    )(log_rate, kv)

    return pl.pallas_call(
        _ret_out_kernel,
        grid=(HEADS, n // tr),
        in_specs=[smem, head_rows, head_rows, v_rows, chunk_mats],
        out_specs=head_rows,
        out_shape=jax.ShapeDtypeStruct((n, HEADS * LANES), BF16),
        compiler_params=_params("arbitrary", "arbitrary"),
    )(log_rate, rq, rk, z, st)


def _flash_kernel(q_ref, k_ref, v_ref, *rest, tq, tk, unroll):
    o_ref, s0_ref, s1_ref, m_ref, acc_ref = rest[-5:]
    bufs = (s0_ref, s1_ref)
    nk = k_ref.shape[0] // tk
    n_steps = (q_ref.shape[0] // tq) * nk

    def q_rows(qt):
        return pl.ds(pl.multiple_of(qt * tq, tq), tq)

    def key_rows(c):
        return pl.ds(pl.multiple_of(c * tk, tk), tk)

    def scores(qt, c, dst):
        dst[...] = lax.dot_general(q_ref[q_rows(qt), :], k_ref[key_rows(c), :],
                                   (((1,), (1,)), ((), ())), preferred_element_type=F32)

    def consume(qt, c, src):
        s = src[...]
        m = jnp.where(c == 0, -jnp.inf, m_ref[...])
        m_new = jnp.maximum(m, jnp.max(s, axis=-1, keepdims=True))
        p = jnp.exp2(s - m_new)
        acc = jnp.exp2(m - m_new) * acc_ref[...] + jnp.dot(
            p.astype(BF16), v_ref[key_rows(c), :], preferred_element_type=F32)
        acc_ref[...] = acc
        m_ref[...] = m_new
        o_ref[q_rows(qt), :] = (acc[:, :LANES] / acc[:, LANES:]).astype(o_ref.dtype)

    def step(parity, qt, c):
        wrap = c == nk - 1
        qt_next = jnp.where(wrap, qt + 1, qt)
        c_next = jnp.where(wrap, 0, c + 1)
        scores(qt_next, c_next, bufs[1 - parity])
        consume(qt, c, bufs[parity])
        return qt_next, c_next

    def group(_, carry):
        qt, c = carry
        for u in range(unroll):
            qt, c = step(u % 2, qt, c)
        return qt, c

    m_ref[...] = jnp.full(m_ref.shape, -jnp.inf, F32)
    acc_ref[...] = jnp.zeros(acc_ref.shape, F32)
    zero = jnp.int32(0)
    scores(zero, zero, s0_ref)
    qt, c = lax.fori_loop(0, (n_steps - 1) // unroll, group, (zero, zero))
    for u in range((n_steps - 1) % unroll):
        qt, c = step(u % 2, qt, c)
    consume(qt, c, bufs[(n_steps - 1) % 2])


FLASH_UNROLL = 6


def _flash(q, k, v, out_shape, q_map, kv_map, o_map, grid, tq, q_tiles, n_keys, key_block,
           prev=None):
    tk = _pick(n_keys, (1280, 640, 256))
    q_block = (None,) * (q.ndim - 2) + (tq * q_tiles, LANES)
    o_block = (None,) * (len(out_shape.shape) - 2) + (tq * q_tiles, LANES)
    in_specs = [
        pl.BlockSpec(q_block, q_map),
        pl.BlockSpec((n_keys, LANES), lambda g, r: (key_block, kv_map(g)),
                     pipeline_mode=pl.Buffered(1)),
        pl.BlockSpec((n_keys, 2 * LANES), lambda g, r: (key_block, kv_map(g)),
                     pipeline_mode=pl.Buffered(1)),
    ]
    args = [q, k, v]
    aliases = {}
    if prev is not None:
        in_specs.append(pl.BlockSpec(memory_space=pl.ANY))
        args.append(prev)
        aliases = {3: 0}
    return pl.pallas_call(
        functools.partial(_flash_kernel, tq=tq, tk=tk, unroll=FLASH_UNROLL),
        grid=grid,
        in_specs=in_specs,
        out_specs=pl.BlockSpec(o_block, o_map),
        out_shape=out_shape,
        scratch_shapes=[pltpu.VMEM((tq, tk), F32), pltpu.VMEM((tq, tk), F32),
                        pltpu.VMEM((tq, 1), F32), pltpu.VMEM((tq, 2 * LANES), F32)],
        input_output_aliases=aliases,
        compiler_params=_params("arbitrary", "arbitrary"),
    )(*args)


def _query_tiling(n_lat):
    tq = _pick(n_lat, (1024, 512, 256))
    q_tiles = _pick(n_lat // tq, (8, 4, 2, 1))
    return tq, q_tiles, n_lat // (tq * q_tiles)


def _diff_attention(dq, dk, dv, n_lat, with_ctx):
    n = dk.shape[0]
    n_ctx = n - n_lat
    tq, q_tiles, nb = _query_tiling(n_lat)
    shape = jax.ShapeDtypeStruct((2, n, HEADS * LANES), BF16)
    o = _flash(dq, dk, dv, shape,
               lambda g, r: (r // nb, r % nb, g), lambda g: g, lambda g, r: (r // nb, r % nb, g),
               (HEADS, 2 * nb), tq, q_tiles, n, 0)
    if with_ctx:
        cb = n_lat // n_ctx
        o = _flash(dq, dk, dv, shape,
                   lambda g, r: (r, cb, g), lambda g: g, lambda g, r: (r, cb, g),
                   (HEADS, 2), n_ctx, 1, n_ctx, cb, prev=o)
    return o


def _gqa_attention(gq, gk, gv, n_lat, with_ctx):
    n = gk.shape[0]
    n_ctx = n - n_lat
    tq, q_tiles, nb = _query_tiling(n_lat)
    shape = jax.ShapeDtypeStruct((n, HEADS * LANES), BF16)
    o = _flash(gq, gk, gv, shape,
               lambda g, r: (r % nb, g * GQA_GROUP + r // nb), lambda g: g,
               lambda g, r: (r % nb, g * GQA_GROUP + r // nb),
               (GQA_KV, GQA_GROUP * nb), tq, q_tiles, n, 0)
    if with_ctx:
        cb = n_lat // n_ctx
        o = _flash(gq, gk, gv, shape,
                   lambda g, r: (cb, g * GQA_GROUP + r), lambda g: g,
                   lambda g, r: (cb, g * GQA_GROUP + r),
                   (GQA_KV, GQA_GROUP), n_ctx, 1, n_ctx, cb, prev=o)
    return o


def _finish_kernel(lam_ref, x_ref, mod_ref, gates_ref, rg_ref, dg_ref, gg_ref, ro_ref, do_ref,
                   go_ref, sub_ref, wb_ref, wo_ref, ng_ref, *rest, n_lat, tm, d, lambda_init,
                   final_norm):
    lp = lam_ref[...]
    lam = (jnp.exp(jnp.sum(lp[0:1] * lp[1:2], axis=-1, keepdims=True))
           - jnp.exp(jnp.sum(lp[2:3] * lp[3:4], axis=-1, keepdims=True)) + lambda_init)

    def head_norm(o):
        return o * lax.rsqrt(jnp.mean(o * o, axis=-1, keepdims=True) + EPS)

    def retention_head(sl):
        return head_norm(ro_ref[:, sl].astype(F32)) * _silu(rg_ref[:, sl].astype(F32))

    def diff_head(sl):
        o = do_ref[0, :, sl].astype(F32) - lam * do_ref[1, :, sl].astype(F32)
        o = head_norm(o) * sub_ref[...]
        return o * (1.0 - lambda_init) * _silu(dg_ref[:, sl].astype(F32))

    def gqa_head(sl):
        return go_ref[:, sl].astype(F32) * _silu(gg_ref[:, sl].astype(F32))

    merged = jnp.zeros((tm, d), F32)
    for b, head in enumerate((retention_head, diff_head, gqa_head)):
        br = jnp.concatenate([head(slice(h * LANES, (h + 1) * LANES)).astype(BF16)
                              for h in range(HEADS)], axis=1)
        y = jnp.dot(br, wb_ref[b], preferred_element_type=F32)
        merged = merged + _sigmoid(gates_ref[:, b * d:(b + 1) * d].astype(F32)) * y
    out = jnp.dot(merged.astype(BF16), wo_ref[...], preferred_element_type=F32)

    first_row = pl.program_id(0) * tm
    rows = first_row + lax.broadcasted_iota(jnp.int32, (tm, 1), 0)
    gate = _row_mod(mod_ref, 2 * d, 3 * d, rows >= n_lat)
    x = x_ref[...] + gate * out
    if final_norm:
        (o_ref,) = rest
        o_ref[...] = x * lax.rsqrt(jnp.mean(x * x, axis=-1, keepdims=True) + EPS) * ng_ref[...]
    else:
        next_mod_ref, o_ref, h_ref = rest
        o_ref[...] = x
        h_ref[...] = _modulated_norm(x, ng_ref[...], next_mod_ref, first_row, n_lat, d)


def _finish(xx, z, mods, lam_params, ro, do, go, subln, wb, wo, next_gain, layer, n_lat,
            lambda_init, last):
    n, d = xx.shape
    tm = 256
    n_rows = n_lat if last else n
    bw = HEADS * LANES

    def rows(width, off=0):
        return pl.BlockSpec((tm, width), lambda i, b=off // width: (i, b))

    const2 = lambda i: (0, 0)
    kern = functools.partial(_finish_kernel, n_lat=n_lat, tm=tm, d=d, lambda_init=lambda_init,
                             final_norm=last)
    if last:
        extra_specs, extra_args = [], []
        out_specs = rows(d)
        out_shape = jax.ShapeDtypeStruct((n_rows, d), F32)
    else:
        extra_specs = [pl.BlockSpec((None, 8, 3 * d), lambda i: (layer + 1, 0, 0))]
        extra_args = [mods]
        out_specs = [rows(d), rows(d)]
        out_shape = [jax.ShapeDtypeStruct((n_rows, d), F32), jax.ShapeDtypeStruct((n_rows, d), BF16)]
    return pl.pallas_call(
        kern,
        grid=(n_rows // tm,),
        in_specs=[
            pl.BlockSpec((4, KEY_W), const2),
            rows(d),
            pl.BlockSpec((None, 8, 3 * d), lambda i: (layer, 0, 0)),
            rows(3 * d, Z_GATES), rows(bw, Z_RG), rows(bw, Z_DG), rows(bw, Z_GG),
            rows(bw),
            pl.BlockSpec((2, tm, bw), lambda i: (0, i, 0)),
            rows(bw),
            pl.BlockSpec((1, LANES), const2),
            pl.BlockSpec((None, 3, bw, d), lambda i: (layer, 0, 0, 0),
                         pipeline_mode=pl.Buffered(1)),
            pl.BlockSpec((None, d, d), lambda i: (layer, 0, 0), pipeline_mode=pl.Buffered(1)),
            pl.BlockSpec((1, d), const2),
        ] + extra_specs,
        out_specs=out_specs,
        out_shape=out_shape,
        compiler_params=_params("arbitrary"),
    )(lam_params, xx, mods, z, z, z, z, ro, do, go, subln.reshape(1, LANES), wb, wo,
      next_gain.reshape(1, d), *extra_args)


def kernel(x, c, ctx, c_ctx, norm_gain, w_ada, b_ada, w_in, ret_log_rate, diff_lambda,
           diff_subln_gain, gqa_q_gain, gqa_k_gain, w_branch, w_out, final_norm_gain):
    _, n_lat, d = x.shape
    n_ctx = ctx.shape[1]
    depth = w_in.shape[0]
    assert x.shape[0] == 1 and d == 2048 and w_in.shape[2] == Z_COLS
    assert n_lat % n_ctx == 0 and n_ctx % CHUNK == 0 and n_lat % GRID_W == 0

    xx = jnp.concatenate([x[0], ctx[0]], axis=0)
    c8 = jnp.concatenate([c, c_ctx[None], jnp.zeros((6, d), F32)], axis=0)
    mods = _ada_all(c8, w_ada, b_ada)
    tabs = _rope_tables(n_lat, n_ctx)
    wb16 = w_branch.astype(BF16)
    wo16 = w_out.astype(BF16)

    h = _norm_mod(xx, mods, norm_gain[0], 0, n_lat)
    for l in range(depth):
        last = l == depth - 1
        lambda_init = 0.8 - 0.6 * math.exp(-0.3 * l)
        z = _in_proj(h, w_in, l)
        rq, rk, dq, dk, dv, gq, gk, gv = _prep(z, tabs, gqa_q_gain[l], gqa_k_gain[l])
        ro = _retention(rq, rk, z, ret_log_rate[l], n_lat)
        do = _diff_attention(dq, dk, dv, n_lat, not last)
        go = _gqa_attention(gq, gk, gv, n_lat, not last)
        next_gain = final_norm_gain if last else norm_gain[l + 1]
        out = _finish(xx, z, mods, diff_lambda[l], ro, do, go, diff_subln_gain[l], wb16, wo16,
                      next_gain, l, n_lat, lambda_init, last)
        if last:
            return out[None]
        xx, h = out
```

```python
import functools
import math

import jax
import jax.numpy as jnp
from jax import lax
from jax.experimental import pallas as pl
from jax.experimental.pallas import tpu as pltpu

F32 = jnp.float32
BF16 = jnp.bfloat16

EPS = 1e-6
ROPE_THETA = 10000.0
GRID_W = 64
LANES = 128
VMEM_LIMIT = 56 * 1024 * 1024

HEADS = 8
HEAD_W = 128
KEY_W = 64
GQA_KV = 2
GQA_GROUP = 4
CHUNK = 128

Z_GATES, Z_RQ, Z_RK, Z_RV, Z_RG = 0, 6144, 6656, 7168, 8192
Z_DQ, Z_DK, Z_DV, Z_DG = 9216, 10240, 11264, 12288
Z_GQ, Z_GG, Z_GK, Z_GV = 13312, 14336, 15360, 15616
Z_COLS = 15872
W_TILE = 512
NORM_ROWS = 128
LOG2_E = math.log2(math.e)


def _pick(n, candidates):
    for c in candidates:
        if n % c == 0:
            return c
    raise ValueError(f"no tile in {candidates} divides {n}")


def _params(*sem):
    return pltpu.CompilerParams(dimension_semantics=sem, vmem_limit_bytes=VMEM_LIMIT)


def _sigmoid(x):
    return 0.5 * jnp.tanh(0.5 * x) + 0.5


def _silu(x):
    return x * _sigmoid(x)


def _ada_kernel(c_ref, w_ref, b_ref, o_ref):
    s = _silu(c_ref[...])
    o_ref[...] = jnp.dot(s, w_ref[...], preferred_element_type=F32,
                         precision=lax.Precision.HIGHEST) + b_ref[...]


def _ada_all(c8, w_ada, b_ada):
    depth, d, d3 = w_ada.shape
    tn = 1024
    return pl.pallas_call(
        _ada_kernel,
        grid=(depth, d3 // tn),
        in_specs=[
            pl.BlockSpec((8, d), lambda l, j: (0, 0)),
            pl.BlockSpec((None, d, tn), lambda l, j: (l, 0, j)),
            pl.BlockSpec((None, 1, tn), lambda l, j: (l, 0, j)),
        ],
        out_specs=pl.BlockSpec((None, 8, tn), lambda l, j: (l, 0, j)),
        out_shape=jax.ShapeDtypeStruct((depth, 8, d3), F32),
        compiler_params=_params("arbitrary", "arbitrary"),
    )(c8, w_ada, b_ada.reshape(depth, 1, d3))


def _row_mod(mod_ref, lo, hi, is_ctx):
    return jnp.where(is_ctx, mod_ref[1:2, lo:hi], mod_ref[0:1, lo:hi])


def _modulated_norm(x, gain, mod_ref, first_row, n_lat, d):
    y = x * lax.rsqrt(jnp.mean(x * x, axis=-1, keepdims=True) + EPS) * gain
    rows = first_row + lax.broadcasted_iota(jnp.int32, (x.shape[0], 1), 0)
    is_ctx = rows >= n_lat
    shift = _row_mod(mod_ref, 0, d, is_ctx)
    scale = _row_mod(mod_ref, d, 2 * d, is_ctx)
    return (y * (1.0 + scale) + shift).astype(BF16)


def _norm_kernel(x_ref, mod_ref, g_ref, h_ref, *, n_lat, tm, d):
    h_ref[...] = _modulated_norm(x_ref[...], g_ref[...], mod_ref, pl.program_id(0) * tm, n_lat, d)


def _norm_mod(xx, mods, gain, layer, n_lat):
    n, d = xx.shape
    tm = NORM_ROWS
    return pl.pallas_call(
        functools.partial(_norm_kernel, n_lat=n_lat, tm=tm, d=d),
        grid=(n // tm,),
        in_specs=[
            pl.BlockSpec((tm, d), lambda i: (i, 0)),
            pl.BlockSpec((None, 8, 3 * d), lambda i: (layer, 0, 0)),
            pl.BlockSpec((1, d), lambda i: (0, 0)),
        ],
        out_specs=pl.BlockSpec((tm, d), lambda i: (i, 0)),
        out_shape=jax.ShapeDtypeStruct((n, d), BF16),
        compiler_params=_params("arbitrary"),
    )(xx, mods, gain.reshape(1, d))


def _inproj_kernel(h_ref, w_ref, z_ref):
    z_ref[...] = jnp.dot(h_ref[...], w_ref[...].astype(BF16),
                         preferred_element_type=F32).astype(BF16)


def _w_block(j):
    return jnp.where(j < 12, j + 19, jnp.where(j < 28, j - 12, jnp.where(j < 30, j - 11, 16)))


def _in_proj(h, w, layer):
    n, d = h.shape
    tm = _pick(n, (3328, 1280, 640, 256))
    return pl.pallas_call(
        _inproj_kernel,
        grid=(n // tm, Z_COLS // W_TILE),
        in_specs=[
            pl.BlockSpec((tm, d), lambda i, j: (i, 0)),
            pl.BlockSpec((None, d, W_TILE), lambda i, j: (layer, 0, _w_block(j))),
        ],
        out_specs=pl.BlockSpec((tm, W_TILE), lambda i, j: (i, j)),
        out_shape=jax.ShapeDtypeStruct((n, Z_COLS), BF16),
        compiler_params=_params("arbitrary", "arbitrary"),
    )(h, w)


def _swap_halves(x, half, in_first_half):
    if 2 * half == LANES:
        return pltpu.roll(x, half, 1)
    return jnp.where(in_first_half, pltpu.roll(x, LANES - half, 1), pltpu.roll(x, half, 1))


def _rope(x, c, s, half, in_first_half=None):
    return x * c + _swap_halves(x, half, in_first_half) * s


def _prep_kernel(rq_ref, rk_ref, dq_ref, dk_ref, dv_ref, gq_ref, gk_ref, gv_ref,
                 sc_ref, ss_ref, ac_ref, as_ref, bc_ref, bs_ref, qg_ref, kg_ref,
                 orq_ref, ork_ref, odq_ref, odk_ref, odv_ref, ogq_ref, ogk_ref, ogv_ref):
    tp = rq_ref.shape[0]
    lane = lax.broadcasted_iota(jnp.int32, (tp, LANES), 1)
    low = lane < KEY_W
    pair_low = lane % KEY_W < KEY_W // 2
    sc, ss = sc_ref[...], ss_ref[...]
    ac, as_ = ac_ref[...], as_ref[...]
    bc, bs = bc_ref[...], bs_ref[...]
    ones = jnp.ones((tp, LANES), BF16)

    k_scale = KEY_W ** -0.5
    for p in range(HEADS // 2):
        sl = slice(p * LANES, (p + 1) * LANES)
        for src, dst, mul in ((rq_ref, orq_ref, 1.0), (rk_ref, ork_ref, k_scale)):
            y = _rope(src[:, sl].astype(F32), sc, ss, KEY_W // 2, pair_low) * mul
            dst[:, (2 * p) * LANES:(2 * p + 1) * LANES] = jnp.where(low, y, 0.0).astype(BF16)
            dst[:, (2 * p + 1) * LANES:(2 * p + 2) * LANES] = jnp.where(
                low, pltpu.roll(y, KEY_W, 1), 0.0).astype(BF16)

    d_scale = KEY_W ** -0.5 * LOG2_E
    for h in range(HEADS):
        sl = slice(h * LANES, (h + 1) * LANES)
        q = _rope(dq_ref[:, sl].astype(F32), ac, as_, KEY_W // 2, pair_low) * d_scale
        odq_ref[0, :, sl] = jnp.where(low, q, 0.0).astype(BF16)
        odq_ref[1, :, sl] = jnp.where(low, 0.0, q).astype(BF16)
        odk_ref[:, sl] = _rope(dk_ref[:, sl].astype(F32), ac, as_, KEY_W // 2,
                               pair_low).astype(BF16)
        odv_ref[:, (2 * h) * LANES:(2 * h + 1) * LANES] = dv_ref[:, sl]
        odv_ref[:, (2 * h + 1) * LANES:(2 * h + 2) * LANES] = ones

    g_scale = HEAD_W ** -0.5 * LOG2_E

    def normed(x, gain):
        return x * lax.rsqrt(jnp.mean(x * x, axis=-1, keepdims=True) + EPS) * gain

    for h in range(HEADS):
        sl = slice(h * LANES, (h + 1) * LANES)
        q = _rope(normed(gq_ref[:, sl].astype(F32), qg_ref[...]), bc, bs, HEAD_W // 2)
        ogq_ref[:, sl] = (q * g_scale).astype(BF16)
    for h in range(GQA_KV):
        sl = slice(h * LANES, (h + 1) * LANES)
        k = _rope(normed(gk_ref[:, sl].astype(F32), kg_ref[...]), bc, bs, HEAD_W // 2)
        ogk_ref[:, sl] = k.astype(BF16)
        ogv_ref[:, (2 * h) * LANES:(2 * h + 1) * LANES] = gv_ref[:, sl]
        ogv_ref[:, (2 * h + 1) * LANES:(2 * h + 2) * LANES] = ones


def _prep(z, tabs, q_gain, k_gain):
    n = z.shape[0]
    tp = _pick(n, (640, 256))

    def zspec(off, width):
        return pl.BlockSpec((tp, width), lambda i, b=off // width: (i, b))

    tab = pl.BlockSpec((tp, LANES), lambda i: (i, 0))
    vec = pl.BlockSpec((1, LANES), lambda i: (0, 0))

    def ospec(width):
        return pl.BlockSpec((tp, width), lambda i: (i, 0))

    return pl.pallas_call(
        _prep_kernel,
        grid=(n // tp,),
        in_specs=[zspec(Z_RQ, 512), zspec(Z_RK, 512), zspec(Z_DQ, 1024), zspec(Z_DK, 1024),
                  zspec(Z_DV, 1024), zspec(Z_GQ, 1024), zspec(Z_GK, 256), zspec(Z_GV, 256),
                  tab, tab, tab, tab, tab, tab, vec, vec],
        out_specs=[ospec(1024), ospec(1024),
                   pl.BlockSpec((2, tp, 1024), lambda i: (0, i, 0)),
                   ospec(1024), ospec(2048), ospec(1024), ospec(256), ospec(512)],
        out_shape=[jax.ShapeDtypeStruct((n, 1024), BF16), jax.ShapeDtypeStruct((n, 1024), BF16),
                   jax.ShapeDtypeStruct((2, n, 1024), BF16), jax.ShapeDtypeStruct((n, 1024), BF16),
                   jax.ShapeDtypeStruct((n, 2048), BF16), jax.ShapeDtypeStruct((n, 1024), BF16),
                   jax.ShapeDtypeStruct((n, 256), BF16), jax.ShapeDtypeStruct((n, 512), BF16)],
        compiler_params=_params("arbitrary"),
    )(z, z, z, z, z, z, z, z, *tabs, q_gain.reshape(1, LANES), k_gain.reshape(1, LANES))


def _rope_tables(n_lat, n_ctx):
    def pattern(cos, sin):
        reps = LANES // (2 * cos.shape[1])
        c = jnp.tile(jnp.concatenate([cos, cos], axis=1), (1, reps))
        s = jnp.tile(jnp.concatenate([-sin, sin], axis=1), (1, reps))
        c = jnp.concatenate([c, jnp.ones((n_ctx, LANES), F32)], axis=0)
        s = jnp.concatenate([s, jnp.zeros((n_ctx, LANES), F32)], axis=0)
        return c, s

    def axial(head_dim):
        n_rows = n_lat // GRID_W
        rows = jnp.repeat(jnp.arange(n_rows), GRID_W).astype(F32)
        cols = jnp.tile(jnp.arange(GRID_W), n_rows).astype(F32)
        n_freq = head_dim // 4
        freqs = ROPE_THETA ** (-jnp.arange(n_freq, dtype=F32) / n_freq)
        ang = jnp.concatenate([rows[:, None] * freqs, cols[:, None] * freqs], axis=-1)
        return jnp.cos(ang), jnp.sin(ang)

    freqs = 1.0 / (ROPE_THETA ** jnp.linspace(0.0, 1.0, KEY_W // 2, dtype=F32))
    ang = jnp.arange(n_lat, dtype=F32)[:, None] * freqs
    return (*pattern(jnp.cos(ang), jnp.sin(ang)), *pattern(*axial(KEY_W)), *pattern(*axial(HEAD_W)))


def _log_decay(lr_ref, direction, h, shape):
    return -jnp.exp(jnp.full(shape, lr_ref[direction, h], F32))


def _ret_sum_kernel(lr_ref, k_ref, v_ref, kv_ref):
    h = pl.program_id(0)
    j = lax.broadcasted_iota(jnp.int32, (CHUNK, LANES), 0).astype(F32)
    w_f = jnp.exp(_log_decay(lr_ref, 0, h, (CHUNK, LANES)) * (CHUNK - 1 - j))
    w_b = jnp.exp(_log_decay(lr_ref, 1, h, (CHUNK, LANES)) * j)
    for c in range(k_ref.shape[0] // CHUNK):
        rows = slice(c * CHUNK, (c + 1) * CHUNK)
        k = k_ref[rows, :].astype(F32)
        kk = (k * w_f + pltpu.roll(k * w_b, KEY_W, 1)).T.astype(BF16)
        kv_ref[c] = jnp.dot(kk, v_ref[rows, :], preferred_element_type=F32)


def _ret_scan_kernel(lr_ref, kv_ref, st_ref, *, n_lat_chunks):
    h = pl.program_id(0)
    nc = kv_ref.shape[0]
    shape = (KEY_W, LANES)
    g_f = jnp.exp(_log_decay(lr_ref, 0, h, shape) * CHUNK)
    g_b = jnp.exp(_log_decay(lr_ref, 1, h, shape) * CHUNK)

    def fwd(c, s):
        st_ref[c, 0:KEY_W, :] = s.astype(BF16)
        return g_f * s + kv_ref[c, 0:KEY_W, :]

    def bwd(t, s):
        c = nc - 1 - t
        st_ref[c, KEY_W:, :] = s.astype(BF16)
        return g_b * s + kv_ref[c, KEY_W:, :]

    zero = jnp.zeros(shape, F32)
    s = lax.fori_loop(n_lat_chunks, nc, fwd, zero)
    lax.fori_loop(0, n_lat_chunks, fwd, s)
    lax.fori_loop(0, nc, bwd, zero)


def _ret_out_kernel(lr_ref, q_ref, k_ref, v_ref, st_ref, o_ref):
    h = pl.program_id(0)
    i = lax.broadcasted_iota(jnp.int32, (CHUNK, CHUNK), 0)
    j = lax.broadcasted_iota(jnp.int32, (CHUNK, CHUNK), 1)
    rel = (i - j).astype(F32)
    lg_f = _log_decay(lr_ref, 0, h, (CHUNK, CHUNK))
    lg_b = _log_decay(lr_ref, 1, h, (CHUNK, CHUNK))
    decay = jnp.where(i >= j, jnp.exp(lg_f * jnp.maximum(rel, 0.0)),
                      jnp.exp(lg_b * jnp.maximum(-rel, 0.0)))
    pos = i.astype(F32)
    cross_f = jnp.exp(lg_f * (pos + 1.0))
    cross_b = jnp.exp(lg_b * (CHUNK - pos))
    for c in range(q_ref.shape[0] // CHUNK):
        rows = slice(c * CHUNK, (c + 1) * CHUNK)
        q = q_ref[rows, :]
        att = lax.dot_general(q, k_ref[rows, :], (((1,), (1,)), ((), ())),
                              preferred_element_type=F32) * decay
        qf = q.astype(F32)
        qs = (qf * cross_f + pltpu.roll(qf * cross_b, KEY_W, 1)).astype(BF16)
        lhs = jnp.concatenate([att.astype(BF16), qs], axis=1)
        rhs = jnp.concatenate([v_ref[rows, :], st_ref[c]], axis=0)
        o_ref[rows, :] = jnp.dot(lhs, rhs, preferred_element_type=F32)


def _retention(rq, rk, z, log_rate, n_lat):
    n = z.shape[0]
    nc = n // CHUNK
    tr = _pick(n, (3328, 1280, 640, 256))
    cpt = tr // CHUNK
    smem = pl.BlockSpec(memory_space=pltpu.SMEM)
    head_rows = pl.BlockSpec((tr, LANES), lambda h, i: (i, h))
    v_rows = pl.BlockSpec((tr, LANES), lambda h, i: (i, Z_RV // LANES + h))
    chunk_mats = pl.BlockSpec((None, cpt, CHUNK, LANES), lambda h, i: (h, i, 0, 0))

    kv = pl.pallas_call(
        _ret_sum_kernel,
        grid=(HEADS, n // tr),
        in_specs=[smem, head_rows, v_rows],
        out_specs=chunk_mats,
        out_shape=jax.ShapeDtypeStruct((HEADS, nc, CHUNK, LANES), F32),
        compiler_params=_params("arbitrary", "arbitrary"),
    )(log_rate, rk, z)

    all_chunks = pl.BlockSpec((None, nc, CHUNK, LANES), lambda h: (h, 0, 0, 0))
    st = pl.pallas_call(
        functools.partial(_ret_scan_kernel, n_lat_chunks=n_lat // CHUNK),
        grid=(HEADS,),
        in_specs=[smem, all_chunks],
        out_specs=all_chunks,
        out_shape=jax.ShapeDtypeStruct((HEADS, nc, CHUNK, LANES), BF16),
        compiler_params=_params("arbitrary"),
    )(log_rate, kv)

    return pl.pallas_call(
        _ret_out_kernel,
        grid=(HEADS, n // tr),
        in_specs=[smem, head_rows, head_rows, v_rows, chunk_mats],
        out_specs=head_rows,
        out_shape=jax.ShapeDtypeStruct((n, HEADS * LANES), F32),
        compiler_params=_params("arbitrary", "arbitrary"),
    )(log_rate, rq, rk, z, st)


def _flash_kernel(q_ref, k_ref, v_ref, *rest, tq, tk, unroll):
    o_ref, s0_ref, s1_ref, s2_ref, m_ref, acc_ref = rest[-6:]
    bufs = (s0_ref, s1_ref, s2_ref)
    nk = k_ref.shape[0] // tk
    n_steps = (q_ref.shape[0] // tq) * nk

    def q_rows(qt):
        return pl.ds(pl.multiple_of(qt * tq, tq), tq)

    def key_rows(c):
        return pl.ds(pl.multiple_of(c * tk, tk), tk)

    def scores(qt, c, dst):
        dst[...] = lax.dot_general(q_ref[q_rows(qt), :], k_ref[key_rows(c), :],
                                   (((1,), (1,)), ((), ())), preferred_element_type=F32)

    def consume(qt, c, src):
        s = src[...]
        m = jnp.where(c == 0, -jnp.inf, m_ref[...])
        m_new = jnp.maximum(m, jnp.max(s, axis=-1, keepdims=True))
        p = jnp.exp2(s - m_new)
        acc = jnp.exp2(m - m_new) * acc_ref[...] + jnp.dot(
            p.astype(BF16), v_ref[key_rows(c), :], preferred_element_type=F32)
        acc_ref[...] = acc
        m_ref[...] = m_new
        o_ref[q_rows(qt), :] = (acc[:, :LANES] / acc[:, LANES:]).astype(o_ref.dtype)

    def step(slot, qt, c):
        wrap = c == nk - 1
        qt_next = jnp.where(wrap, qt + 1, qt)
        c_next = jnp.where(wrap, 0, c + 1)
        scores(qt_next, c_next, bufs[(slot + 1) % 3])
        consume(qt, c, bufs[slot])
        return qt_next, c_next

    def group(_, carry):
        qt, c = carry
        for u in range(unroll):
            qt, c = step(u % 3, qt, c)
        return qt, c

    m_ref[...] = jnp.full(m_ref.shape, -jnp.inf, F32)
    acc_ref[...] = jnp.zeros(acc_ref.shape, F32)
    zero = jnp.int32(0)
    scores(zero, zero, s0_ref)
    qt, c = lax.fori_loop(0, (n_steps - 1) // unroll, group, (zero, zero))
    for u in range((n_steps - 1) % unroll):
        qt, c = step(u % 3, qt, c)
    consume(qt, c, bufs[(n_steps - 1) % unroll % 3])


FLASH_UNROLL = 6


def _flash(q, k, v, out_shape, q_map, kv_map, o_map, grid, tq, q_tiles, n_keys, key_block,
           prev=None):
    tk = _pick(n_keys, (1280, 640, 256))
    q_block = (None,) * (q.ndim - 2) + (tq * q_tiles, LANES)
    o_block = (None,) * (len(out_shape.shape) - 2) + (tq * q_tiles, LANES)
    in_specs = [
        pl.BlockSpec(q_block, q_map),
        pl.BlockSpec((n_keys, LANES), lambda g, r: (key_block, kv_map(g)),
                     pipeline_mode=pl.Buffered(1)),
        pl.BlockSpec((n_keys, 2 * LANES), lambda g, r: (key_block, kv_map(g)),
                     pipeline_mode=pl.Buffered(1)),
    ]
    args = [q, k, v]
    aliases = {}
    if prev is not None:
        in_specs.append(pl.BlockSpec(memory_space=pl.ANY))
        args.append(prev)
        aliases = {3: 0}
    return pl.pallas_call(
        functools.partial(_flash_kernel, tq=tq, tk=tk, unroll=FLASH_UNROLL),
        grid=grid,
        in_specs=in_specs,
        out_specs=pl.BlockSpec(o_block, o_map),
        out_shape=out_shape,
        scratch_shapes=[pltpu.VMEM((tq, tk), F32), pltpu.VMEM((tq, tk), F32),
                        pltpu.VMEM((tq, tk), F32),
                        pltpu.VMEM((tq, 1), F32), pltpu.VMEM((tq, 2 * LANES), F32)],
        input_output_aliases=aliases,
        compiler_params=_params("arbitrary", "arbitrary"),
    )(*args)


def _query_tiling(n_lat):
    tq = _pick(n_lat, (1024, 512, 256))
    q_tiles = _pick(n_lat // tq, (8, 4, 2, 1))
    return tq, q_tiles, n_lat // (tq * q_tiles)


def _diff_attention(dq, dk, dv, n_lat, with_ctx):
    n = dk.shape[0]
    n_ctx = n - n_lat
    tq, q_tiles, nb = _query_tiling(n_lat)
    shape = jax.ShapeDtypeStruct((2, n, HEADS * LANES), F32)
    o = _flash(dq, dk, dv, shape,
               lambda g, r: (r // nb, r % nb, g), lambda g: g, lambda g, r: (r // nb, r % nb, g),
               (HEADS, 2 * nb), tq, q_tiles, n, 0)
    if with_ctx:
        cb = n_lat // n_ctx
        o = _flash(dq, dk, dv, shape,
                   lambda g, r: (r, cb, g), lambda g: g, lambda g, r: (r, cb, g),
                   (HEADS, 2), n_ctx, 1, n_ctx, cb, prev=o)
    return o


def _gqa_attention(gq, gk, gv, n_lat, with_ctx):
    n = gk.shape[0]
    n_ctx = n - n_lat
    tq, q_tiles, nb = _query_tiling(n_lat)
    shape = jax.ShapeDtypeStruct((n, HEADS * LANES), BF16)
    o = _flash(gq, gk, gv, shape,
               lambda g, r: (r % nb, g * GQA_GROUP + r // nb), lambda g: g,
               lambda g, r: (r % nb, g * GQA_GROUP + r // nb),
               (GQA_KV, GQA_GROUP * nb), tq, q_tiles, n, 0)
    if with_ctx:
        cb = n_lat // n_ctx
        o = _flash(gq, gk, gv, shape,
                   lambda g, r: (cb, g * GQA_GROUP + r), lambda g: g,
                   lambda g, r: (cb, g * GQA_GROUP + r),
                   (GQA_KV, GQA_GROUP), n_ctx, 1, n_ctx, cb, prev=o)
    return o


def _finish_kernel(lam_ref, x_ref, mod_ref, gates_ref, rg_ref, dg_ref, gg_ref, ro_ref, do_ref,
                   go_ref, sub_ref, wb_ref, wo_ref, ng_ref, *rest, n_lat, tm, d, lambda_init,
                   final_norm):
    lp = lam_ref[...]
    lam = (jnp.exp(jnp.sum(lp[0:1] * lp[1:2], axis=-1, keepdims=True))
           - jnp.exp(jnp.sum(lp[2:3] * lp[3:4], axis=-1, keepdims=True)) + lambda_init)

    def head_norm(o):
        return o * lax.rsqrt(jnp.mean(o * o, axis=-1, keepdims=True) + EPS)

    def retention_head(sl):
        return head_norm(ro_ref[:, sl]) * _silu(rg_ref[:, sl].astype(F32))

    def diff_head(sl):
        o = head_norm(do_ref[0, :, sl] - lam * do_ref[1, :, sl]) * sub_ref[...]
        return o * (1.0 - lambda_init) * _silu(dg_ref[:, sl].astype(F32))

    def gqa_head(sl):
        return go_ref[:, sl].astype(F32) * _silu(gg_ref[:, sl].astype(F32))

    merged = jnp.zeros((tm, d), F32)
    for b, head in enumerate((retention_head, diff_head, gqa_head)):
        br = jnp.concatenate([head(slice(h * LANES, (h + 1) * LANES)).astype(BF16)
                              for h in range(HEADS)], axis=1)
        y = jnp.dot(br, wb_ref[b], preferred_element_type=F32)
        merged = merged + _sigmoid(gates_ref[:, b * d:(b + 1) * d].astype(F32)) * y
    out = jnp.dot(merged.astype(BF16), wo_ref[...], preferred_element_type=F32)

    first_row = pl.program_id(0) * tm
    rows = first_row + lax.broadcasted_iota(jnp.int32, (tm, 1), 0)
    gate = _row_mod(mod_ref, 2 * d, 3 * d, rows >= n_lat)
    x = x_ref[...] + gate * out
    if final_norm:
        (o_ref,) = rest
        o_ref[...] = x * lax.rsqrt(jnp.mean(x * x, axis=-1, keepdims=True) + EPS) * ng_ref[...]
    else:
        next_mod_ref, o_ref, h_ref = rest
        o_ref[...] = x
        h_ref[...] = _modulated_norm(x, ng_ref[...], next_mod_ref, first_row, n_lat, d)


def _finish(xx, z, mods, lam_params, ro, do, go, subln, wb, wo, next_gain, layer, n_lat,
            lambda_init, last):
    n, d = xx.shape
    tm = 256
    n_rows = n_lat if last else n
    bw = HEADS * LANES

    def rows(width, off=0):
        return pl.BlockSpec((tm, width), lambda i, b=off // width: (i, b))

    const2 = lambda i: (0, 0)
    kern = functools.partial(_finish_kernel, n_lat=n_lat, tm=tm, d=d, lambda_init=lambda_init,
                             final_norm=last)
    if last:
        extra_specs, extra_args = [], []
        out_specs = rows(d)
        out_shape = jax.ShapeDtypeStruct((n_rows, d), F32)
    else:
        extra_specs = [pl.BlockSpec((None, 8, 3 * d), lambda i: (layer + 1, 0, 0))]
        extra_args = [mods]
        out_specs = [rows(d), rows(d)]
        out_shape = [jax.ShapeDtypeStruct((n_rows, d), F32), jax.ShapeDtypeStruct((n_rows, d), BF16)]
    return pl.pallas_call(
        kern,
        grid=(n_rows // tm,),
        in_specs=[
            pl.BlockSpec((4, KEY_W), const2),
            rows(d),
            pl.BlockSpec((None, 8, 3 * d), lambda i: (layer, 0, 0)),
            rows(3 * d, Z_GATES), rows(bw, Z_RG), rows(bw, Z_DG), rows(bw, Z_GG),
            rows(bw),
            pl.BlockSpec((2, tm, bw), lambda i: (0, i, 0)),
            rows(bw),
            pl.BlockSpec((1, LANES), const2),
            pl.BlockSpec((None, 3, bw, d), lambda i: (layer, 0, 0, 0),
                         pipeline_mode=pl.Buffered(1)),
            pl.BlockSpec((None, d, d), lambda i: (layer, 0, 0), pipeline_mode=pl.Buffered(1)),
            pl.BlockSpec((1, d), const2),
        ] + extra_specs,
        out_specs=out_specs,
        out_shape=out_shape,
        compiler_params=_params("arbitrary"),
    )(lam_params, xx, mods, z, z, z, z, ro, do, go, subln.reshape(1, LANES), wb, wo,
      next_gain.reshape(1, d), *extra_args)


def kernel(x, c, ctx, c_ctx, norm_gain, w_ada, b_ada, w_in, ret_log_rate, diff_lambda,
           diff_subln_gain, gqa_q_gain, gqa_k_gain, w_branch, w_out, final_norm_gain):
    _, n_lat, d = x.shape
    n_ctx = ctx.shape[1]
    depth = w_in.shape[0]
    assert x.shape[0] == 1 and d == 2048 and w_in.shape[2] == Z_COLS
    assert n_lat % n_ctx == 0 and n_ctx % CHUNK == 0 and n_lat % GRID_W == 0

    xx = jnp.concatenate([x[0], ctx[0]], axis=0)
    c8 = jnp.concatenate([c, c_ctx[None], jnp.zeros((6, d), F32)], axis=0)
    mods = _ada_all(c8, w_ada, b_ada)
    tabs = _rope_tables(n_lat, n_ctx)
    wb16 = w_branch.astype(BF16)
    wo16 = w_out.astype(BF16)

    h = _norm_mod(xx, mods, norm_gain[0], 0, n_lat)
    for l in range(depth):
        last = l == depth - 1
        lambda_init = 0.8 - 0.6 * math.exp(-0.3 * l)
        z = _in_proj(h, w_in, l)
        rq, rk, dq, dk, dv, gq, gk, gv = _prep(z, tabs, gqa_q_gain[l], gqa_k_gain[l])
        ro = _retention(rq, rk, z, ret_log_rate[l], n_lat)
        do = _diff_attention(dq, dk, dv, n_lat, not last)
        go = _gqa_attention(gq, gk, gv, n_lat, not last)
        next_gain = final_norm_gain if last else norm_gain[l + 1]
        out = _finish(xx, z, mods, diff_lambda[l], ro, do, go, diff_subln_gain[l], wb16, wo16,
                      next_gain, l, n_lat, lambda_init, last)
        if last:
            return out[None]
        xx, h = out
```

```python
import functools
import math

import jax
import jax.numpy as jnp
from jax import lax
from jax.experimental import pallas as pl
from jax.experimental.pallas import tpu as pltpu

F32 = jnp.float32
BF16 = jnp.bfloat16

EPS = 1e-6
ROPE_THETA = 10000.0
GRID_W = 64
LANES = 128
VMEM_LIMIT = 56 * 1024 * 1024

HEADS = 8
HEAD_W = 128
KEY_W = 64
GQA_KV = 2
GQA_GROUP = 4
CHUNK = 128

Z_GATES, Z_RQ, Z_RK, Z_RV, Z_RG = 0, 6144, 6656, 7168, 8192
Z_DQ, Z_DK, Z_DV, Z_DG = 9216, 10240, 11264, 12288
Z_GQ, Z_GG, Z_GK, Z_GV = 13312, 14336, 15360, 15616
Z_COLS = 15872
W_TILE = 512
NORM_ROWS = 128
LOG2_E = math.log2(math.e)


def _pick(n, candidates):
    for c in candidates:
        if n % c == 0:
            return c
    raise ValueError(f"no tile in {candidates} divides {n}")


def _params(*sem):
    return pltpu.CompilerParams(dimension_semantics=sem, vmem_limit_bytes=VMEM_LIMIT)


def _sigmoid(x):
    return 0.5 * jnp.tanh(0.5 * x) + 0.5


def _silu(x):
    return x * _sigmoid(x)


def _ada_kernel(c_ref, w_ref, b_ref, o_ref):
    s = _silu(c_ref[...])
    o_ref[...] = jnp.dot(s, w_ref[...], preferred_element_type=F32,
                         precision=lax.Precision.HIGHEST) + b_ref[...]


def _ada_all(c8, w_ada, b_ada):
    depth, d, d3 = w_ada.shape
    tn = 1024
    return pl.pallas_call(
        _ada_kernel,
        grid=(depth, d3 // tn),
        in_specs=[
            pl.BlockSpec((8, d), lambda l, j: (0, 0)),
            pl.BlockSpec((None, d, tn), lambda l, j: (l, 0, j)),
            pl.BlockSpec((None, 1, tn), lambda l, j: (l, 0, j)),
        ],
        out_specs=pl.BlockSpec((None, 8, tn), lambda l, j: (l, 0, j)),
        out_shape=jax.ShapeDtypeStruct((depth, 8, d3), F32),
        compiler_params=_params("arbitrary", "arbitrary"),
    )(c8, w_ada, b_ada.reshape(depth, 1, d3))


def _row_mod(mod_ref, lo, hi, is_ctx):
    return jnp.where(is_ctx, mod_ref[1:2, lo:hi], mod_ref[0:1, lo:hi])


def _modulated_norm(x, gain, mod_ref, first_row, n_lat, d):
    y = x * lax.rsqrt(jnp.mean(x * x, axis=-1, keepdims=True) + EPS) * gain
    rows = first_row + lax.broadcasted_iota(jnp.int32, (x.shape[0], 1), 0)
    is_ctx = rows >= n_lat
    shift = _row_mod(mod_ref, 0, d, is_ctx)
    scale = _row_mod(mod_ref, d, 2 * d, is_ctx)
    return (y * (1.0 + scale) + shift).astype(BF16)


def _norm_kernel(x_ref, mod_ref, g_ref, h_ref, *, n_lat, tm, d):
    h_ref[...] = _modulated_norm(x_ref[...], g_ref[...], mod_ref, pl.program_id(0) * tm, n_lat, d)


def _norm_mod(xx, mods, gain, layer, n_lat):
    n, d = xx.shape
    tm = NORM_ROWS
    return pl.pallas_call(
        functools.partial(_norm_kernel, n_lat=n_lat, tm=tm, d=d),
        grid=(n // tm,),
        in_specs=[
            pl.BlockSpec((tm, d), lambda i: (i, 0)),
            pl.BlockSpec((None, 8, 3 * d), lambda i: (layer, 0, 0)),
            pl.BlockSpec((1, d), lambda i: (0, 0)),
        ],
        out_specs=pl.BlockSpec((tm, d), lambda i: (i, 0)),
        out_shape=jax.ShapeDtypeStruct((n, d), BF16),
        compiler_params=_params("arbitrary"),
    )(xx, mods, gain.reshape(1, d))


def _inproj_kernel(h_ref, w_ref, z_ref):
    z_ref[...] = jnp.dot(h_ref[...], w_ref[...].astype(BF16),
                         preferred_element_type=F32).astype(BF16)


def _w_block(j):
    return jnp.where(j < 12, j + 19, jnp.where(j < 28, j - 12, jnp.where(j < 30, j - 11, 16)))


def _in_proj(h, w, layer):
    n, d = h.shape
    tm = _pick(n, (3328, 1280, 640, 256))
    return pl.pallas_call(
        _inproj_kernel,
        grid=(n // tm, Z_COLS // W_TILE),
        in_specs=[
            pl.BlockSpec((tm, d), lambda i, j: (i, 0)),
            pl.BlockSpec((None, d, W_TILE), lambda i, j: (layer, 0, _w_block(j))),
        ],
        out_specs=pl.BlockSpec((tm, W_TILE), lambda i, j: (i, j)),
        out_shape=jax.ShapeDtypeStruct((n, Z_COLS), BF16),
        compiler_params=_params("arbitrary", "arbitrary"),
    )(h, w)


def _swap_halves(x, half, in_first_half):
    if 2 * half == LANES:
        return pltpu.roll(x, half, 1)
    return jnp.where(in_first_half, pltpu.roll(x, LANES - half, 1), pltpu.roll(x, half, 1))


def _rope(x, c, s, half, in_first_half=None):
    return x * c + _swap_halves(x, half, in_first_half) * s


def _prep_kernel(rq_ref, rk_ref, dq_ref, dk_ref, dv_ref, gq_ref, gk_ref, gv_ref,
                 sc_ref, ss_ref, ac_ref, as_ref, bc_ref, bs_ref, qg_ref, kg_ref,
                 orq_ref, ork_ref, odq_ref, odk_ref, odv_ref, ogq_ref, ogk_ref, ogv_ref):
    tp = rq_ref.shape[0]
    lane = lax.broadcasted_iota(jnp.int32, (tp, LANES), 1)
    low = lane < KEY_W
    pair_low = lane % KEY_W < KEY_W // 2
    sc, ss = sc_ref[...], ss_ref[...]
    ac, as_ = ac_ref[...], as_ref[...]
    bc, bs = bc_ref[...], bs_ref[...]
    ones = jnp.ones((tp, LANES), BF16)

    k_scale = KEY_W ** -0.5
    for p in range(HEADS // 2):
        sl = slice(p * LANES, (p + 1) * LANES)
        for src, dst, mul in ((rq_ref, orq_ref, 1.0), (rk_ref, ork_ref, k_scale)):
            y = _rope(src[:, sl].astype(F32), sc, ss, KEY_W // 2, pair_low) * mul
            dst[:, (2 * p) * LANES:(2 * p + 1) * LANES] = jnp.where(low, y, 0.0).astype(BF16)
            dst[:, (2 * p + 1) * LANES:(2 * p + 2) * LANES] = jnp.where(
                low, pltpu.roll(y, KEY_W, 1), 0.0).astype(BF16)

    d_scale = KEY_W ** -0.5 * LOG2_E
    for h in range(HEADS):
        sl = slice(h * LANES, (h + 1) * LANES)
        q = _rope(dq_ref[:, sl].astype(F32), ac, as_, KEY_W // 2, pair_low) * d_scale
        odq_ref[0, :, sl] = jnp.where(low, q, 0.0).astype(BF16)
        odq_ref[1, :, sl] = jnp.where(low, 0.0, q).astype(BF16)
        odk_ref[:, sl] = _rope(dk_ref[:, sl].astype(F32), ac, as_, KEY_W // 2,
                               pair_low).astype(BF16)
        odv_ref[:, (2 * h) * LANES:(2 * h + 1) * LANES] = dv_ref[:, sl]
        odv_ref[:, (2 * h + 1) * LANES:(2 * h + 2) * LANES] = ones

    g_scale = HEAD_W ** -0.5 * LOG2_E

    def normed(x, gain):
        return x * lax.rsqrt(jnp.mean(x * x, axis=-1, keepdims=True) + EPS) * gain

    for h in range(HEADS):
        sl = slice(h * LANES, (h + 1) * LANES)
        q = _rope(normed(gq_ref[:, sl].astype(F32), qg_ref[...]), bc, bs, HEAD_W // 2)
        ogq_ref[:, sl] = (q * g_scale).astype(BF16)
    for h in range(GQA_KV):
        sl = slice(h * LANES, (h + 1) * LANES)
        k = _rope(normed(gk_ref[:, sl].astype(F32), kg_ref[...]), bc, bs, HEAD_W // 2)
        ogk_ref[:, sl] = k.astype(BF16)
        ogv_ref[:, (2 * h) * LANES:(2 * h + 1) * LANES] = gv_ref[:, sl]
        ogv_ref[:, (2 * h + 1) * LANES:(2 * h + 2) * LANES] = ones


def _prep(z, tabs, q_gain, k_gain):
    n = z.shape[0]
    tp = _pick(n, (640, 256))

    def zspec(off, width):
        return pl.BlockSpec((tp, width), lambda i, b=off // width: (i, b))

    tab = pl.BlockSpec((tp, LANES), lambda i: (i, 0))
    vec = pl.BlockSpec((1, LANES), lambda i: (0, 0))

    def ospec(width):
        return pl.BlockSpec((tp, width), lambda i: (i, 0))

    return pl.pallas_call(
        _prep_kernel,
        grid=(n // tp,),
        in_specs=[zspec(Z_RQ, 512), zspec(Z_RK, 512), zspec(Z_DQ, 1024), zspec(Z_DK, 1024),
                  zspec(Z_DV, 1024), zspec(Z_GQ, 1024), zspec(Z_GK, 256), zspec(Z_GV, 256),
                  tab, tab, tab, tab, tab, tab, vec, vec],
        out_specs=[ospec(1024), ospec(1024),
                   pl.BlockSpec((2, tp, 1024), lambda i: (0, i, 0)),
                   ospec(1024), ospec(2048), ospec(1024), ospec(256), ospec(512)],
        out_shape=[jax.ShapeDtypeStruct((n, 1024), BF16), jax.ShapeDtypeStruct((n, 1024), BF16),
                   jax.ShapeDtypeStruct((2, n, 1024), BF16), jax.ShapeDtypeStruct((n, 1024), BF16),
                   jax.ShapeDtypeStruct((n, 2048), BF16), jax.ShapeDtypeStruct((n, 1024), BF16),
                   jax.ShapeDtypeStruct((n, 256), BF16), jax.ShapeDtypeStruct((n, 512), BF16)],
        compiler_params=_params("arbitrary"),
    )(z, z, z, z, z, z, z, z, *tabs, q_gain.reshape(1, LANES), k_gain.reshape(1, LANES))


def _rope_tables(n_lat, n_ctx):
    def pattern(cos, sin):
        reps = LANES // (2 * cos.shape[1])
        c = jnp.tile(jnp.concatenate([cos, cos], axis=1), (1, reps))
        s = jnp.tile(jnp.concatenate([-sin, sin], axis=1), (1, reps))
        c = jnp.concatenate([c, jnp.ones((n_ctx, LANES), F32)], axis=0)
        s = jnp.concatenate([s, jnp.zeros((n_ctx, LANES), F32)], axis=0)
        return c, s

    def axial(head_dim):
        n_rows = n_lat // GRID_W
        rows = jnp.repeat(jnp.arange(n_rows), GRID_W).astype(F32)
        cols = jnp.tile(jnp.arange(GRID_W), n_rows).astype(F32)
        n_freq = head_dim // 4
        freqs = ROPE_THETA ** (-jnp.arange(n_freq, dtype=F32) / n_freq)
        ang = jnp.concatenate([rows[:, None] * freqs, cols[:, None] * freqs], axis=-1)
        return jnp.cos(ang), jnp.sin(ang)

    freqs = 1.0 / (ROPE_THETA ** jnp.linspace(0.0, 1.0, KEY_W // 2, dtype=F32))
    ang = jnp.arange(n_lat, dtype=F32)[:, None] * freqs
    return (*pattern(jnp.cos(ang), jnp.sin(ang)), *pattern(*axial(KEY_W)), *pattern(*axial(HEAD_W)))


def _log_decay(lr_ref, direction, h, shape):
    return -jnp.exp(jnp.full(shape, lr_ref[direction, h], F32))


def _ret_sum_kernel(lr_ref, k_ref, v_ref, kv_ref):
    h = pl.program_id(0)
    j = lax.broadcasted_iota(jnp.int32, (CHUNK, LANES), 0).astype(F32)
    w_f = jnp.exp(_log_decay(lr_ref, 0, h, (CHUNK, LANES)) * (CHUNK - 1 - j))
    w_b = jnp.exp(_log_decay(lr_ref, 1, h, (CHUNK, LANES)) * j)
    for c in range(k_ref.shape[0] // CHUNK):
        rows = slice(c * CHUNK, (c + 1) * CHUNK)
        k = k_ref[rows, :].astype(F32)
        kk = (k * w_f + pltpu.roll(k * w_b, KEY_W, 1)).T.astype(BF16)
        kv_ref[c] = jnp.dot(kk, v_ref[rows, :], preferred_element_type=F32)


def _ret_scan_kernel(lr_ref, kv_ref, st_ref, *, n_lat_chunks):
    h = pl.program_id(0)
    nc = kv_ref.shape[0]
    shape = (KEY_W, LANES)
    g_f = jnp.exp(_log_decay(lr_ref, 0, h, shape) * CHUNK)
    g_b = jnp.exp(_log_decay(lr_ref, 1, h, shape) * CHUNK)

    def fwd(c, s):
        st_ref[c, 0:KEY_W, :] = s.astype(BF16)
        return g_f * s + kv_ref[c, 0:KEY_W, :]

    def bwd(t, s):
        c = nc - 1 - t
        st_ref[c, KEY_W:, :] = s.astype(BF16)
        return g_b * s + kv_ref[c, KEY_W:, :]

    zero = jnp.zeros(shape, F32)
    s = lax.fori_loop(n_lat_chunks, nc, fwd, zero)
    lax.fori_loop(0, n_lat_chunks, fwd, s)
    lax.fori_loop(0, nc, bwd, zero)


def _ret_out_kernel(lr_ref, q_ref, k_ref, v_ref, st_ref, o_ref):
    h = pl.program_id(0)
    i = lax.broadcasted_iota(jnp.int32, (CHUNK, CHUNK), 0)
    j = lax.broadcasted_iota(jnp.int32, (CHUNK, CHUNK), 1)
    rel = (i - j).astype(F32)
    lg_f = _log_decay(lr_ref, 0, h, (CHUNK, CHUNK))
    lg_b = _log_decay(lr_ref, 1, h, (CHUNK, CHUNK))
    decay = jnp.where(i >= j, jnp.exp(lg_f * jnp.maximum(rel, 0.0)),
                      jnp.exp(lg_b * jnp.maximum(-rel, 0.0)))
    pos = i.astype(F32)
    cross_f = jnp.exp(lg_f * (pos + 1.0))
    cross_b = jnp.exp(lg_b * (CHUNK - pos))
    for c in range(q_ref.shape[0] // CHUNK):
        rows = slice(c * CHUNK, (c + 1) * CHUNK)
        q = q_ref[rows, :]
        att = lax.dot_general(q, k_ref[rows, :], (((1,), (1,)), ((), ())),
                              preferred_element_type=F32) * decay
        qf = q.astype(F32)
        qs = (qf * cross_f + pltpu.roll(qf * cross_b, KEY_W, 1)).astype(BF16)
        lhs = jnp.concatenate([att.astype(BF16), qs], axis=1)
        rhs = jnp.concatenate([v_ref[rows, :], st_ref[c]], axis=0)
        o_ref[rows, :] = jnp.dot(lhs, rhs, preferred_element_type=F32)


def _retention(rq, rk, z, log_rate, n_lat):
    n = z.shape[0]
    nc = n // CHUNK
    tr = _pick(n, (3328, 1280, 640, 256))
    cpt = tr // CHUNK
    smem = pl.BlockSpec(memory_space=pltpu.SMEM)
    head_rows = pl.BlockSpec((tr, LANES), lambda h, i: (i, h))
    v_rows = pl.BlockSpec((tr, LANES), lambda h, i: (i, Z_RV // LANES + h))
    chunk_mats = pl.BlockSpec((None, cpt, CHUNK, LANES), lambda h, i: (h, i, 0, 0))

    kv = pl.pallas_call(
        _ret_sum_kernel,
        grid=(HEADS, n // tr),
        in_specs=[smem, head_rows, v_rows],
        out_specs=chunk_mats,
        out_shape=jax.ShapeDtypeStruct((HEADS, nc, CHUNK, LANES), F32),
        compiler_params=_params("arbitrary", "arbitrary"),
    )(log_rate, rk, z)

    all_chunks = pl.BlockSpec((None, nc, CHUNK, LANES), lambda h: (h, 0, 0, 0))
    st = pl.pallas_call(
        functools.partial(_ret_scan_kernel, n_lat_chunks=n_lat // CHUNK),
        grid=(HEADS,),
        in_specs=[smem, all_chunks],
        out_specs=all_chunks,
        out_shape=jax.ShapeDtypeStruct((HEADS, nc, CHUNK, LANES), BF16),
        compiler_params=_params("arbitrary"),
    )(log_rate, kv)

    return pl.pallas_call(
        _ret_out_kernel,
        grid=(HEADS, n // tr),
        in_specs=[smem, head_rows, head_rows, v_rows, chunk_mats],
        out_specs=head_rows,
        out_shape=jax.ShapeDtypeStruct((n, HEADS * LANES), F32),
        compiler_params=_params("arbitrary", "arbitrary"),
    )(log_rate, rq, rk, z, st)


def _flash_kernel(q_ref, k_ref, v_ref, *rest, tq, tk, unroll):
    o_ref, s0_ref, s1_ref, s2_ref, m_ref, acc_ref = rest[-6:]
    bufs = (s0_ref, s1_ref, s2_ref)
    nk = k_ref.shape[0] // tk
    n_steps = (q_ref.shape[0] // tq) * nk

    def q_rows(qt):
        return pl.ds(pl.multiple_of(qt * tq, tq), tq)

    def key_rows(c):
        return pl.ds(pl.multiple_of(c * tk, tk), tk)

    def scores(qt, c, dst):
        dst[...] = lax.dot_general(q_ref[q_rows(qt), :], k_ref[key_rows(c), :],
                                   (((1,), (1,)), ((), ())), preferred_element_type=F32)

    def consume(qt, c, src):
        s = src[...]
        m = jnp.where(c == 0, -jnp.inf, m_ref[...])
        m_new = jnp.maximum(m, jnp.max(s, axis=-1, keepdims=True))
        p = jnp.exp2(s - m_new)
        acc = jnp.exp2(m - m_new) * acc_ref[...] + jnp.dot(
            p.astype(BF16), v_ref[key_rows(c), :], preferred_element_type=F32)
        acc_ref[...] = acc
        m_ref[...] = m_new
        o_ref[q_rows(qt), :] = (acc[:, :LANES] / acc[:, LANES:]).astype(o_ref.dtype)

    def step(slot, qt, c):
        wrap = c == nk - 1
        qt_next = jnp.where(wrap, qt + 1, qt)
        c_next = jnp.where(wrap, 0, c + 1)
        scores(qt_next, c_next, bufs[(slot + 1) % 3])
        consume(qt, c, bufs[slot])
        return qt_next, c_next

    def group(_, carry):
        qt, c = carry
        for u in range(unroll):
            qt, c = step(u % 3, qt, c)
        return qt, c

    m_ref[...] = jnp.full(m_ref.shape, -jnp.inf, F32)
    acc_ref[...] = jnp.zeros(acc_ref.shape, F32)
    zero = jnp.int32(0)
    scores(zero, zero, s0_ref)
    qt, c = lax.fori_loop(0, (n_steps - 1) // unroll, group, (zero, zero))
    for u in range((n_steps - 1) % unroll):
        qt, c = step(u % 3, qt, c)
    consume(qt, c, bufs[(n_steps - 1) % unroll % 3])


FLASH_UNROLL = 6
assert FLASH_UNROLL % 3 == 0


def _flash(q, k, v, out_shape, q_map, kv_map, o_map, grid, tq, q_tiles, n_keys, key_block,
           prev=None):
    tk = _pick(n_keys, (1280, 640, 256))
    q_block = (None,) * (q.ndim - 2) + (tq * q_tiles, LANES)
    o_block = (None,) * (len(out_shape.shape) - 2) + (tq * q_tiles, LANES)
    in_specs = [
        pl.BlockSpec(q_block, q_map),
        pl.BlockSpec((n_keys, LANES), lambda g, r: (key_block, kv_map(g)),
                     pipeline_mode=pl.Buffered(1)),
        pl.BlockSpec((n_keys, 2 * LANES), lambda g, r: (key_block, kv_map(g)),
                     pipeline_mode=pl.Buffered(1)),
    ]
    args = [q, k, v]
    aliases = {}
    if prev is not None:
        in_specs.append(pl.BlockSpec(memory_space=pl.ANY))
        args.append(prev)
        aliases = {3: 0}
    return pl.pallas_call(
        functools.partial(_flash_kernel, tq=tq, tk=tk, unroll=FLASH_UNROLL),
        grid=grid,
        in_specs=in_specs,
        out_specs=pl.BlockSpec(o_block, o_map),
        out_shape=out_shape,
        scratch_shapes=[pltpu.VMEM((tq, tk), F32), pltpu.VMEM((tq, tk), F32),
                        pltpu.VMEM((tq, tk), F32),
                        pltpu.VMEM((tq, 1), F32), pltpu.VMEM((tq, 2 * LANES), F32)],
        input_output_aliases=aliases,
        compiler_params=_params("arbitrary", "arbitrary"),
    )(*args)


def _query_tiling(n_lat):
    tq = _pick(n_lat, (1024, 512, 256))
    q_tiles = _pick(n_lat // tq, (8, 4, 2, 1))
    return tq, q_tiles, n_lat // (tq * q_tiles)


def _diff_attention(dq, dk, dv, n_lat, with_ctx):
    n = dk.shape[0]
    n_ctx = n - n_lat
    tq, q_tiles, nb = _query_tiling(n_lat)
    shape = jax.ShapeDtypeStruct((2, n, HEADS * LANES), F32)
    o = _flash(dq, dk, dv, shape,
               lambda g, r: (r // nb, r % nb, g), lambda g: g, lambda g, r: (r // nb, r % nb, g),
               (HEADS, 2 * nb), tq, q_tiles, n, 0)
    if with_ctx:
        cb = n_lat // n_ctx
        o = _flash(dq, dk, dv, shape,
                   lambda g, r: (r, cb, g), lambda g: g, lambda g, r: (r, cb, g),
                   (HEADS, 2), n_ctx, 1, n_ctx, cb, prev=o)
    return o


def _gqa_attention(gq, gk, gv, n_lat, with_ctx):
    n = gk.shape[0]
    n_ctx = n - n_lat
    tq, q_tiles, nb = _query_tiling(n_lat)
    shape = jax.ShapeDtypeStruct((n, HEADS * LANES), BF16)
    o = _flash(gq, gk, gv, shape,
               lambda g, r: (r % nb, g * GQA_GROUP + r // nb), lambda g: g,
               lambda g, r: (r % nb, g * GQA_GROUP + r // nb),
               (GQA_KV, GQA_GROUP * nb), tq, q_tiles, n, 0)
    if with_ctx:
        cb = n_lat // n_ctx
        o = _flash(gq, gk, gv, shape,
                   lambda g, r: (cb, g * GQA_GROUP + r), lambda g: g,
                   lambda g, r: (cb, g * GQA_GROUP + r),
                   (GQA_KV, GQA_GROUP), n_ctx, 1, n_ctx, cb, prev=o)
    return o


def _finish_kernel(lam_ref, x_ref, mod_ref, gates_ref, rg_ref, dg_ref, gg_ref, ro_ref, do_ref,
                   go_ref, sub_ref, wb_ref, wo_ref, ng_ref, *rest, n_lat, tm, d, lambda_init,
                   final_norm):
    lp = lam_ref[...]
    lam = (jnp.exp(jnp.sum(lp[0:1] * lp[1:2], axis=-1, keepdims=True))
           - jnp.exp(jnp.sum(lp[2:3] * lp[3:4], axis=-1, keepdims=True)) + lambda_init)

    def head_norm(o):
        return o * lax.rsqrt(jnp.mean(o * o, axis=-1, keepdims=True) + EPS)

    def retention_head(sl):
        return head_norm(ro_ref[:, sl]) * _silu(rg_ref[:, sl].astype(F32))

    def diff_head(sl):
        o = head_norm(do_ref[0, :, sl] - lam * do_ref[1, :, sl]) * sub_ref[...]
        return o * (1.0 - lambda_init) * _silu(dg_ref[:, sl].astype(F32))

    def gqa_head(sl):
        return go_ref[:, sl].astype(F32) * _silu(gg_ref[:, sl].astype(F32))

    merged = jnp.zeros((tm, d), F32)
    for b, head in enumerate((retention_head, diff_head, gqa_head)):
        br = jnp.concatenate([head(slice(h * LANES, (h + 1) * LANES)).astype(BF16)
                              for h in range(HEADS)], axis=1)
        y = jnp.dot(br, wb_ref[b], preferred_element_type=F32)
        merged = merged + _sigmoid(gates_ref[:, b * d:(b + 1) * d].astype(F32)) * y
    out = jnp.dot(merged.astype(BF16), wo_ref[...], preferred_element_type=F32)

    first_row = pl.program_id(0) * tm
    rows = first_row + lax.broadcasted_iota(jnp.int32, (tm, 1), 0)
    gate = _row_mod(mod_ref, 2 * d, 3 * d, rows >= n_lat)
    x = x_ref[...] + gate * out
    if final_norm:
        (o_ref,) = rest
        o_ref[...] = x * lax.rsqrt(jnp.mean(x * x, axis=-1, keepdims=True) + EPS) * ng_ref[...]
    else:
        next_mod_ref, o_ref, h_ref = rest
        o_ref[...] = x
        h_ref[...] = _modulated_norm(x, ng_ref[...], next_mod_ref, first_row, n_lat, d)


def _finish(xx, z, mods, lam_params, ro, do, go, subln, wb, wo, next_gain, layer, n_lat,
            lambda_init, last):
    n, d = xx.shape
    tm = 256
    n_rows = n_lat if last else n
    bw = HEADS * LANES

    def rows(width, off=0):
        return pl.BlockSpec((tm, width), lambda i, b=off // width: (i, b))

    const2 = lambda i: (0, 0)
    kern = functools.partial(_finish_kernel, n_lat=n_lat, tm=tm, d=d, lambda_init=lambda_init,
                             final_norm=last)
    if last:
        extra_specs, extra_args = [], []
        out_specs = rows(d)
        out_shape = jax.ShapeDtypeStruct((n_rows, d), F32)
    else:
        extra_specs = [pl.BlockSpec((None, 8, 3 * d), lambda i: (layer + 1, 0, 0))]
        extra_args = [mods]
        out_specs = [rows(d), rows(d)]
        out_shape = [jax.ShapeDtypeStruct((n_rows, d), F32), jax.ShapeDtypeStruct((n_rows, d), BF16)]
    return pl.pallas_call(
        kern,
        grid=(n_rows // tm,),
        in_specs=[
            pl.BlockSpec((4, KEY_W), const2),
            rows(d),
            pl.BlockSpec((None, 8, 3 * d), lambda i: (layer, 0, 0)),
            rows(3 * d, Z_GATES), rows(bw, Z_RG), rows(bw, Z_DG), rows(bw, Z_GG),
            rows(bw),
            pl.BlockSpec((2, tm, bw), lambda i: (0, i, 0)),
            rows(bw),
            pl.BlockSpec((1, LANES), const2),
            pl.BlockSpec((None, 3, bw, d), lambda i: (layer, 0, 0, 0),
                         pipeline_mode=pl.Buffered(1)),
            pl.BlockSpec((None, d, d), lambda i: (layer, 0, 0), pipeline_mode=pl.Buffered(1)),
            pl.BlockSpec((1, d), const2),
        ] + extra_specs,
        out_specs=out_specs,
        out_shape=out_shape,
        compiler_params=_params("arbitrary"),
    )(lam_params, xx, mods, z, z, z, z, ro, do, go, subln.reshape(1, LANES), wb, wo,
      next_gain.reshape(1, d), *extra_args)


def kernel(x, c, ctx, c_ctx, norm_gain, w_ada, b_ada, w_in, ret_log_rate, diff_lambda,
           diff_subln_gain, gqa_q_gain, gqa_k_gain, w_branch, w_out, final_norm_gain):
    _, n_lat, d = x.shape
    n_ctx = ctx.shape[1]
    depth = w_in.shape[0]
    assert x.shape[0] == 1 and d == 2048 and w_in.shape[2] == Z_COLS
    assert n_lat % n_ctx == 0 and n_ctx % CHUNK == 0 and n_lat % GRID_W == 0

    xx = jnp.concatenate([x[0], ctx[0]], axis=0)
    c8 = jnp.concatenate([c, c_ctx[None], jnp.zeros((6, d), F32)], axis=0)
    mods = _ada_all(c8, w_ada, b_ada)
    tabs = _rope_tables(n_lat, n_ctx)
    wb16 = w_branch.astype(BF16)
    wo16 = w_out.astype(BF16)

    h = _norm_mod(xx, mods, norm_gain[0], 0, n_lat)
    for l in range(depth):
        last = l == depth - 1
        lambda_init = 0.8 - 0.6 * math.exp(-0.3 * l)
        z = _in_proj(h, w_in, l)
        rq, rk, dq, dk, dv, gq, gk, gv = _prep(z, tabs, gqa_q_gain[l], gqa_k_gain[l])
        ro = _retention(rq, rk, z, ret_log_rate[l], n_lat)
        do = _diff_attention(dq, dk, dv, n_lat, not last)
        go = _gqa_attention(gq, gk, gv, n_lat, not last)
        next_gain = final_norm_gain if last else norm_gain[l + 1]
        out = _finish(xx, z, mods, diff_lambda[l], ro, do, go, diff_subln_gain[l], wb16, wo16,
                      next_gain, l, n_lat, lambda_init, last)
        if last:
            return out[None]
        xx, h = out
```

```python
import functools
import math

import jax
import jax.numpy as jnp
from jax import lax
from jax.experimental import pallas as pl
from jax.experimental.pallas import tpu as pltpu

F32 = jnp.float32
BF16 = jnp.bfloat16

EPS = 1e-6
ROPE_THETA = 10000.0
GRID_W = 64
LANES = 128
VMEM_LIMIT = 56 * 1024 * 1024

HEADS = 8
HEAD_W = 128
KEY_W = 64
GQA_KV = 2
GQA_GROUP = 4
CHUNK = 128

Z_GATES, Z_RQ, Z_RK, Z_RV, Z_RG = 0, 6144, 6656, 7168, 8192
Z_DQ, Z_DK, Z_DV, Z_DG = 9216, 10240, 11264, 12288
Z_GQ, Z_GG, Z_GK, Z_GV = 13312, 14336, 15360, 15616
Z_COLS = 15872
W_TILE = 512
NORM_ROWS = 128
LOG2_E = math.log2(math.e)


def _pick(n, candidates):
    for c in candidates:
        if n % c == 0:
            return c
    raise ValueError(f"no tile in {candidates} divides {n}")


def _params(*sem):
    return pltpu.CompilerParams(dimension_semantics=sem, vmem_limit_bytes=VMEM_LIMIT)


def _sigmoid(x):
    return 0.5 * jnp.tanh(0.5 * x) + 0.5


def _silu(x):
    return x * _sigmoid(x)


def _ada_kernel(c_ref, w_ref, b_ref, o_ref):
    s = _silu(c_ref[...])
    o_ref[...] = jnp.dot(s, w_ref[...], preferred_element_type=F32,
                         precision=lax.Precision.HIGHEST) + b_ref[...]


def _ada_all(c8, w_ada, b_ada):
    depth, d, d3 = w_ada.shape
    tn = 1024
    return pl.pallas_call(
        _ada_kernel,
        grid=(depth, d3 // tn),
        in_specs=[
            pl.BlockSpec((8, d), lambda l, j: (0, 0)),
            pl.BlockSpec((None, d, tn), lambda l, j: (l, 0, j)),
            pl.BlockSpec((None, 1, tn), lambda l, j: (l, 0, j)),
        ],
        out_specs=pl.BlockSpec((None, 8, tn), lambda l, j: (l, 0, j)),
        out_shape=jax.ShapeDtypeStruct((depth, 8, d3), F32),
        compiler_params=_params("arbitrary", "arbitrary"),
    )(c8, w_ada, b_ada.reshape(depth, 1, d3))


def _row_mod(mod_ref, lo, hi, is_ctx):
    return jnp.where(is_ctx, mod_ref[1:2, lo:hi], mod_ref[0:1, lo:hi])


def _modulated_norm(x, gain, mod_ref, first_row, n_lat, d):
    y = x * lax.rsqrt(jnp.mean(x * x, axis=-1, keepdims=True) + EPS) * gain
    rows = first_row + lax.broadcasted_iota(jnp.int32, (x.shape[0], 1), 0)
    is_ctx = rows >= n_lat
    shift = _row_mod(mod_ref, 0, d, is_ctx)
    scale = _row_mod(mod_ref, d, 2 * d, is_ctx)
    return (y * (1.0 + scale) + shift).astype(BF16)


def _norm_kernel(x_ref, mod_ref, g_ref, h_ref, *, n_lat, tm, d):
    h_ref[...] = _modulated_norm(x_ref[...], g_ref[...], mod_ref, pl.program_id(0) * tm, n_lat, d)


def _norm_mod(xx, mods, gain, layer, n_lat):
    n, d = xx.shape
    tm = NORM_ROWS
    return pl.pallas_call(
        functools.partial(_norm_kernel, n_lat=n_lat, tm=tm, d=d),
        grid=(n // tm,),
        in_specs=[
            pl.BlockSpec((tm, d), lambda i: (i, 0)),
            pl.BlockSpec((None, 8, 3 * d), lambda i: (layer, 0, 0)),
            pl.BlockSpec((1, d), lambda i: (0, 0)),
        ],
        out_specs=pl.BlockSpec((tm, d), lambda i: (i, 0)),
        out_shape=jax.ShapeDtypeStruct((n, d), BF16),
        compiler_params=_params("arbitrary"),
    )(xx, mods, gain.reshape(1, d))


def _inproj_kernel(h_ref, w_ref, z_ref):
    z_ref[...] = jnp.dot(h_ref[...], w_ref[...].astype(BF16),
                         preferred_element_type=F32).astype(BF16)


def _w_block(j):
    return jnp.where(j < 12, j + 19, jnp.where(j < 28, j - 12, jnp.where(j < 30, j - 11, 16)))


def _in_proj(h, w, layer):
    n, d = h.shape
    tm = _pick(n, (3328, 1280, 640, 256))
    return pl.pallas_call(
        _inproj_kernel,
        grid=(n // tm, Z_COLS // W_TILE),
        in_specs=[
            pl.BlockSpec((tm, d), lambda i, j: (i, 0)),
            pl.BlockSpec((None, d, W_TILE), lambda i, j: (layer, 0, _w_block(j))),
        ],
        out_specs=pl.BlockSpec((tm, W_TILE), lambda i, j: (i, j)),
        out_shape=jax.ShapeDtypeStruct((n, Z_COLS), BF16),
        compiler_params=_params("arbitrary", "arbitrary"),
    )(h, w)


def _swap_halves(x, half, in_first_half):
    if 2 * half == LANES:
        return pltpu.roll(x, half, 1)
    return jnp.where(in_first_half, pltpu.roll(x, LANES - half, 1), pltpu.roll(x, half, 1))


def _rope(x, c, s, half, in_first_half=None):
    return x * c + _swap_halves(x, half, in_first_half) * s


def _prep_kernel(rq_ref, rk_ref, dq_ref, dk_ref, dv_ref, gq_ref, gk_ref, gv_ref,
                 sc_ref, ss_ref, ac_ref, as_ref, bc_ref, bs_ref, qg_ref, kg_ref,
                 orq_ref, ork_ref, odq_ref, odk_ref, odv_ref, ogq_ref, ogk_ref, ogv_ref):
    tp = rq_ref.shape[0]
    lane = lax.broadcasted_iota(jnp.int32, (tp, LANES), 1)
    low = lane < KEY_W
    pair_low = lane % KEY_W < KEY_W // 2
    sc, ss = sc_ref[...], ss_ref[...]
    ac, as_ = ac_ref[...], as_ref[...]
    bc, bs = bc_ref[...], bs_ref[...]
    ones = jnp.ones((tp, LANES), BF16)

    k_scale = KEY_W ** -0.5
    for p in range(HEADS // 2):
        sl = slice(p * LANES, (p + 1) * LANES)
        for src, dst, mul in ((rq_ref, orq_ref, 1.0), (rk_ref, ork_ref, k_scale)):
            y = _rope(src[:, sl].astype(F32), sc, ss, KEY_W // 2, pair_low) * mul
            dst[:, (2 * p) * LANES:(2 * p + 1) * LANES] = jnp.where(low, y, 0.0).astype(BF16)
            dst[:, (2 * p + 1) * LANES:(2 * p + 2) * LANES] = jnp.where(
                low, pltpu.roll(y, KEY_W, 1), 0.0).astype(BF16)

    d_scale = KEY_W ** -0.5 * LOG2_E
    for h in range(HEADS):
        sl = slice(h * LANES, (h + 1) * LANES)
        q = _rope(dq_ref[:, sl].astype(F32), ac, as_, KEY_W // 2, pair_low) * d_scale
        odq_ref[0, :, sl] = jnp.where(low, q, 0.0).astype(BF16)
        odq_ref[1, :, sl] = jnp.where(low, 0.0, q).astype(BF16)
        odk_ref[:, sl] = _rope(dk_ref[:, sl].astype(F32), ac, as_, KEY_W // 2,
                               pair_low).astype(BF16)
        odv_ref[:, (2 * h) * LANES:(2 * h + 1) * LANES] = dv_ref[:, sl]
        odv_ref[:, (2 * h + 1) * LANES:(2 * h + 2) * LANES] = ones

    g_scale = HEAD_W ** -0.5 * LOG2_E

    def normed(x, gain):
        return x * lax.rsqrt(jnp.mean(x * x, axis=-1, keepdims=True) + EPS) * gain

    for h in range(HEADS):
        sl = slice(h * LANES, (h + 1) * LANES)
        q = _rope(normed(gq_ref[:, sl].astype(F32), qg_ref[...]), bc, bs, HEAD_W // 2)
        ogq_ref[:, sl] = (q * g_scale).astype(BF16)
    for h in range(GQA_KV):
        sl = slice(h * LANES, (h + 1) * LANES)
        k = _rope(normed(gk_ref[:, sl].astype(F32), kg_ref[...]), bc, bs, HEAD_W // 2)
        ogk_ref[:, sl] = k.astype(BF16)
        ogv_ref[:, (2 * h) * LANES:(2 * h + 1) * LANES] = gv_ref[:, sl]
        ogv_ref[:, (2 * h + 1) * LANES:(2 * h + 2) * LANES] = ones


def _prep(z, tabs, q_gain, k_gain):
    n = z.shape[0]
    tp = _pick(n, (640, 256))

    def zspec(off, width):
        return pl.BlockSpec((tp, width), lambda i, b=off // width: (i, b))

    tab = pl.BlockSpec((tp, LANES), lambda i: (i, 0))
    vec = pl.BlockSpec((1, LANES), lambda i: (0, 0))

    def ospec(width):
        return pl.BlockSpec((tp, width), lambda i: (i, 0))

    return pl.pallas_call(
        _prep_kernel,
        grid=(n // tp,),
        in_specs=[zspec(Z_RQ, 512), zspec(Z_RK, 512), zspec(Z_DQ, 1024), zspec(Z_DK, 1024),
                  zspec(Z_DV, 1024), zspec(Z_GQ, 1024), zspec(Z_GK, 256), zspec(Z_GV, 256),
                  tab, tab, tab, tab, tab, tab, vec, vec],
        out_specs=[ospec(1024), ospec(1024),
                   pl.BlockSpec((2, tp, 1024), lambda i: (0, i, 0)),
                   ospec(1024), ospec(2048), ospec(1024), ospec(256), ospec(512)],
        out_shape=[jax.ShapeDtypeStruct((n, 1024), BF16), jax.ShapeDtypeStruct((n, 1024), BF16),
                   jax.ShapeDtypeStruct((2, n, 1024), BF16), jax.ShapeDtypeStruct((n, 1024), BF16),
                   jax.ShapeDtypeStruct((n, 2048), BF16), jax.ShapeDtypeStruct((n, 1024), BF16),
                   jax.ShapeDtypeStruct((n, 256), BF16), jax.ShapeDtypeStruct((n, 512), BF16)],
        compiler_params=_params("arbitrary"),
    )(z, z, z, z, z, z, z, z, *tabs, q_gain.reshape(1, LANES), k_gain.reshape(1, LANES))


def _rope_tables(n_lat, n_ctx):
    def pattern(cos, sin):
        reps = LANES // (2 * cos.shape[1])
        c = jnp.tile(jnp.concatenate([cos, cos], axis=1), (1, reps))
        s = jnp.tile(jnp.concatenate([-sin, sin], axis=1), (1, reps))
        c = jnp.concatenate([c, jnp.ones((n_ctx, LANES), F32)], axis=0)
        s = jnp.concatenate([s, jnp.zeros((n_ctx, LANES), F32)], axis=0)
        return c, s

    def axial(head_dim):
        n_rows = n_lat // GRID_W
        rows = jnp.repeat(jnp.arange(n_rows), GRID_W).astype(F32)
        cols = jnp.tile(jnp.arange(GRID_W), n_rows).astype(F32)
        n_freq = head_dim // 4
        freqs = ROPE_THETA ** (-jnp.arange(n_freq, dtype=F32) / n_freq)
        ang = jnp.concatenate([rows[:, None] * freqs, cols[:, None] * freqs], axis=-1)
        return jnp.cos(ang), jnp.sin(ang)

    freqs = 1.0 / (ROPE_THETA ** jnp.linspace(0.0, 1.0, KEY_W // 2, dtype=F32))
    ang = jnp.arange(n_lat, dtype=F32)[:, None] * freqs
    return (*pattern(jnp.cos(ang), jnp.sin(ang)), *pattern(*axial(KEY_W)), *pattern(*axial(HEAD_W)))


def _log_decay(lr_ref, direction, h, shape):
    return -jnp.exp(jnp.full(shape, lr_ref[direction, h], F32))


def _ret_sum_kernel(lr_ref, k_ref, v_ref, kv_ref):
    h = pl.program_id(0)
    j = lax.broadcasted_iota(jnp.int32, (CHUNK, LANES), 0).astype(F32)
    w_f = jnp.exp(_log_decay(lr_ref, 0, h, (CHUNK, LANES)) * (CHUNK - 1 - j))
    w_b = jnp.exp(_log_decay(lr_ref, 1, h, (CHUNK, LANES)) * j)
    for c in range(k_ref.shape[0] // CHUNK):
        rows = slice(c * CHUNK, (c + 1) * CHUNK)
        k = k_ref[rows, :].astype(F32)
        kk = (k * w_f + pltpu.roll(k * w_b, KEY_W, 1)).T.astype(BF16)
        kv_ref[c] = jnp.dot(kk, v_ref[rows, :], preferred_element_type=F32)


def _ret_scan_kernel(lr_ref, kv_ref, st_ref, *, n_lat_chunks):
    h = pl.program_id(0)
    nc = kv_ref.shape[0]
    shape = (KEY_W, LANES)
    g_f = jnp.exp(_log_decay(lr_ref, 0, h, shape) * CHUNK)
    g_b = jnp.exp(_log_decay(lr_ref, 1, h, shape) * CHUNK)

    def fwd(c, s):
        st_ref[c, 0:KEY_W, :] = s.astype(BF16)
        return g_f * s + kv_ref[c, 0:KEY_W, :]

    def bwd(t, s):
        c = nc - 1 - t
        st_ref[c, KEY_W:, :] = s.astype(BF16)
        return g_b * s + kv_ref[c, KEY_W:, :]

    zero = jnp.zeros(shape, F32)
    s = lax.fori_loop(n_lat_chunks, nc, fwd, zero)
    lax.fori_loop(0, n_lat_chunks, fwd, s)
    lax.fori_loop(0, nc, bwd, zero)


def _ret_out_kernel(lr_ref, q_ref, k_ref, v_ref, st_ref, o_ref):
    h = pl.program_id(0)
    i = lax.broadcasted_iota(jnp.int32, (CHUNK, CHUNK), 0)
    j = lax.broadcasted_iota(jnp.int32, (CHUNK, CHUNK), 1)
    rel = (i - j).astype(F32)
    lg_f = _log_decay(lr_ref, 0, h, (CHUNK, CHUNK))
    lg_b = _log_decay(lr_ref, 1, h, (CHUNK, CHUNK))
    decay = jnp.where(i >= j, jnp.exp(lg_f * jnp.maximum(rel, 0.0)),
                      jnp.exp(lg_b * jnp.maximum(-rel, 0.0)))
    pos = i.astype(F32)
    cross_f = jnp.exp(lg_f * (pos + 1.0))
    cross_b = jnp.exp(lg_b * (CHUNK - pos))
    for c in range(q_ref.shape[0] // CHUNK):
        rows = slice(c * CHUNK, (c + 1) * CHUNK)
        q = q_ref[rows, :]
        att = lax.dot_general(q, k_ref[rows, :], (((1,), (1,)), ((), ())),
                              preferred_element_type=F32) * decay
        qf = q.astype(F32)
        qs = (qf * cross_f + pltpu.roll(qf * cross_b, KEY_W, 1)).astype(BF16)
        lhs = jnp.concatenate([att.astype(BF16), qs], axis=1)
        rhs = jnp.concatenate([v_ref[rows, :], st_ref[c]], axis=0)
        o_ref[rows, :] = jnp.dot(lhs, rhs, preferred_element_type=F32)


def _retention(rq, rk, z, log_rate, n_lat):
    n = z.shape[0]
    nc = n // CHUNK
    tr = _pick(n, (3328, 1280, 640, 256))
    cpt = tr // CHUNK
    smem = pl.BlockSpec(memory_space=pltpu.SMEM)
    head_rows = pl.BlockSpec((tr, LANES), lambda h, i: (i, h))
    v_rows = pl.BlockSpec((tr, LANES), lambda h, i: (i, Z_RV // LANES + h))
    chunk_mats = pl.BlockSpec((None, cpt, CHUNK, LANES), lambda h, i: (h, i, 0, 0))

    kv = pl.pallas_call(
        _ret_sum_kernel,
        grid=(HEADS, n // tr),
        in_specs=[smem, head_rows, v_rows],
        out_specs=chunk_mats,
        out_shape=jax.ShapeDtypeStruct((HEADS, nc, CHUNK, LANES), F32),
        compiler_params=_params("arbitrary", "arbitrary"),
    )(log_rate, rk, z)

    all_chunks = pl.BlockSpec((None, nc, CHUNK, LANES), lambda h: (h, 0, 0, 0))
    st = pl.pallas_call(
        functools.partial(_ret_scan_kernel, n_lat_chunks=n_lat // CHUNK),
        grid=(HEADS,),
        in_specs=[smem, all_chunks],
        out_specs=all_chunks,
        out_shape=jax.ShapeDtypeStruct((HEADS, nc, CHUNK, LANES), BF16),
        compiler_params=_params("arbitrary"),
    )(log_rate, kv)

    return pl.pallas_call(
        _ret_out_kernel,
        grid=(HEADS, n // tr),
        in_specs=[smem, head_rows, head_rows, v_rows, chunk_mats],
        out_specs=head_rows,
        out_shape=jax.ShapeDtypeStruct((n, HEADS * LANES), F32),
        compiler_params=_params("arbitrary", "arbitrary"),
    )(log_rate, rq, rk, z, st)


def _flash_kernel(q_ref, k_ref, v_ref, *rest, tq, tk, unroll):
    o_ref, s0_ref, s1_ref, s2_ref, m_ref, acc_ref = rest[-6:]
    bufs = (s0_ref, s1_ref, s2_ref)
    nk = k_ref.shape[0] // tk
    n_steps = (q_ref.shape[0] // tq) * nk

    def q_rows(qt):
        return pl.ds(pl.multiple_of(qt * tq, tq), tq)

    def key_rows(c):
        return pl.ds(pl.multiple_of(c * tk, tk), tk)

    def scores(qt, c, dst):
        dst[...] = lax.dot_general(q_ref[q_rows(qt), :], k_ref[key_rows(c), :],
                                   (((1,), (1,)), ((), ())), preferred_element_type=F32)

    def consume(qt, c, src):
        s = src[...]
        m = jnp.where(c == 0, -jnp.inf, m_ref[...])
        m_new = jnp.maximum(m, jnp.max(s, axis=-1, keepdims=True))
        p = jnp.exp2(s - m_new)
        acc = jnp.exp2(m - m_new) * acc_ref[...] + jnp.dot(
            p.astype(BF16), v_ref[key_rows(c), :], preferred_element_type=F32)
        acc_ref[...] = acc
        m_ref[...] = m_new
        o_ref[q_rows(qt), :] = (acc[:, :LANES] / acc[:, LANES:]).astype(o_ref.dtype)

    def following(pos):
        qt, c = pos
        wrap = c == nk - 1
        return jnp.where(wrap, qt + 1, qt), jnp.where(wrap, 0, c + 1)

    def step(slot, cur, nxt):
        ahead = following(nxt)
        scores(*ahead, bufs[(slot + 2) % 3])
        consume(*cur, bufs[slot])
        return nxt, ahead

    def group(_, carry):
        cur, nxt = carry
        for u in range(unroll):
            cur, nxt = step(u % 3, cur, nxt)
        return cur, nxt

    m_ref[...] = jnp.full(m_ref.shape, -jnp.inf, F32)
    acc_ref[...] = jnp.zeros(acc_ref.shape, F32)
    first = (jnp.int32(0), jnp.int32(0))
    scores(*first, s0_ref)
    if n_steps == 1:
        consume(*first, s0_ref)
        return
    second = following(first)
    scores(*second, s1_ref)
    cur, nxt = lax.fori_loop(0, (n_steps - 2) // unroll, group, (first, second))
    rem = (n_steps - 2) % unroll
    for u in range(rem):
        cur, nxt = step(u % 3, cur, nxt)
    consume(*cur, bufs[rem % 3])
    consume(*nxt, bufs[(rem + 1) % 3])


FLASH_UNROLL = 6


def _flash(q, k, v, out_shape, q_map, kv_map, o_map, grid, tq, q_tiles, n_keys, key_block,
           prev=None):
    tk = _pick(n_keys, (1280, 640, 256))
    q_block = (None,) * (q.ndim - 2) + (tq * q_tiles, LANES)
    o_block = (None,) * (len(out_shape.shape) - 2) + (tq * q_tiles, LANES)
    in_specs = [
        pl.BlockSpec(q_block, q_map),
        pl.BlockSpec((n_keys, LANES), lambda g, r: (key_block, kv_map(g)),
                     pipeline_mode=pl.Buffered(1)),
        pl.BlockSpec((n_keys, 2 * LANES), lambda g, r: (key_block, kv_map(g)),
                     pipeline_mode=pl.Buffered(1)),
    ]
    args = [q, k, v]
    aliases = {}
    if prev is not None:
        in_specs.append(pl.BlockSpec(memory_space=pl.ANY))
        args.append(prev)
        aliases = {3: 0}
    return pl.pallas_call(
        functools.partial(_flash_kernel, tq=tq, tk=tk, unroll=FLASH_UNROLL),
        grid=grid,
        in_specs=in_specs,
        out_specs=pl.BlockSpec(o_block, o_map),
        out_shape=out_shape,
        scratch_shapes=[pltpu.VMEM((tq, tk), F32), pltpu.VMEM((tq, tk), F32),
                        pltpu.VMEM((tq, tk), F32),
                        pltpu.VMEM((tq, 1), F32), pltpu.VMEM((tq, 2 * LANES), F32)],
        input_output_aliases=aliases,
        compiler_params=_params("arbitrary", "arbitrary"),
    )(*args)


def _query_tiling(n_lat):
    tq = _pick(n_lat, (1024, 512, 256))
    q_tiles = _pick(n_lat // tq, (8, 4, 2, 1))
    return tq, q_tiles, n_lat // (tq * q_tiles)


def _diff_attention(dq, dk, dv, n_lat, with_ctx):
    n = dk.shape[0]
    n_ctx = n - n_lat
    tq, q_tiles, nb = _query_tiling(n_lat)
    shape = jax.ShapeDtypeStruct((2, n, HEADS * LANES), F32)
    o = _flash(dq, dk, dv, shape,
               lambda g, r: (r // nb, r % nb, g), lambda g: g, lambda g, r: (r // nb, r % nb, g),
               (HEADS, 2 * nb), tq, q_tiles, n, 0)
    if with_ctx:
        cb = n_lat // n_ctx
        o = _flash(dq, dk, dv, shape,
                   lambda g, r: (r, cb, g), lambda g: g, lambda g, r: (r, cb, g),
                   (HEADS, 2), n_ctx, 1, n_ctx, cb, prev=o)
    return o


def _gqa_attention(gq, gk, gv, n_lat, with_ctx):
    n = gk.shape[0]
    n_ctx = n - n_lat
    tq, q_tiles, nb = _query_tiling(n_lat)
    shape = jax.ShapeDtypeStruct((n, HEADS * LANES), BF16)
    o = _flash(gq, gk, gv, shape,
               lambda g, r: (r % nb, g * GQA_GROUP + r // nb), lambda g: g,
               lambda g, r: (r % nb, g * GQA_GROUP + r // nb),
               (GQA_KV, GQA_GROUP * nb), tq, q_tiles, n, 0)
    if with_ctx:
        cb = n_lat // n_ctx
        o = _flash(gq, gk, gv, shape,
                   lambda g, r: (cb, g * GQA_GROUP + r), lambda g: g,
                   lambda g, r: (cb, g * GQA_GROUP + r),
                   (GQA_KV, GQA_GROUP), n_ctx, 1, n_ctx, cb, prev=o)
    return o


def _finish_kernel(lam_ref, x_ref, mod_ref, gates_ref, rg_ref, dg_ref, gg_ref, ro_ref, do_ref,
                   go_ref, sub_ref, wb_ref, wo_ref, ng_ref, *rest, n_lat, tm, d, lambda_init,
                   final_norm):
    lp = lam_ref[...]
    lam = (jnp.exp(jnp.sum(lp[0:1] * lp[1:2], axis=-1, keepdims=True))
           - jnp.exp(jnp.sum(lp[2:3] * lp[3:4], axis=-1, keepdims=True)) + lambda_init)

    def head_norm(o):
        return o * lax.rsqrt(jnp.mean(o * o, axis=-1, keepdims=True) + EPS)

    def retention_head(sl):
        return head_norm(ro_ref[:, sl]) * _silu(rg_ref[:, sl].astype(F32))

    def diff_head(sl):
        o = head_norm(do_ref[0, :, sl] - lam * do_ref[1, :, sl]) * sub_ref[...]
        return o * (1.0 - lambda_init) * _silu(dg_ref[:, sl].astype(F32))

    def gqa_head(sl):
        return go_ref[:, sl].astype(F32) * _silu(gg_ref[:, sl].astype(F32))

    merged = jnp.zeros((tm, d), F32)
    for b, head in enumerate((retention_head, diff_head, gqa_head)):
        br = jnp.concatenate([head(slice(h * LANES, (h + 1) * LANES)).astype(BF16)
                              for h in range(HEADS)], axis=1)
        y = jnp.dot(br, wb_ref[b], preferred_element_type=F32)
        merged = merged + _sigmoid(gates_ref[:, b * d:(b + 1) * d].astype(F32)) * y
    out = jnp.dot(merged.astype(BF16), wo_ref[...], preferred_element_type=F32)

    first_row = pl.program_id(0) * tm
    rows = first_row + lax.broadcasted_iota(jnp.int32, (tm, 1), 0)
    gate = _row_mod(mod_ref, 2 * d, 3 * d, rows >= n_lat)
    x = x_ref[...] + gate * out
    if final_norm:
        (o_ref,) = rest
        o_ref[...] = x * lax.rsqrt(jnp.mean(x * x, axis=-1, keepdims=True) + EPS) * ng_ref[...]
    else:
        next_mod_ref, o_ref, h_ref = rest
        o_ref[...] = x
        h_ref[...] = _modulated_norm(x, ng_ref[...], next_mod_ref, first_row, n_lat, d)


def _finish(xx, z, mods, lam_params, ro, do, go, subln, wb, wo, next_gain, layer, n_lat,
            lambda_init, last):
    n, d = xx.shape
    tm = 256
    n_rows = n_lat if last else n
    bw = HEADS * LANES

    def rows(width, off=0):
        return pl.BlockSpec((tm, width), lambda i, b=off // width: (i, b))

    const2 = lambda i: (0, 0)
    kern = functools.partial(_finish_kernel, n_lat=n_lat, tm=tm, d=d, lambda_init=lambda_init,
                             final_norm=last)
    if last:
        extra_specs, extra_args = [], []
        out_specs = rows(d)
        out_shape = jax.ShapeDtypeStruct((n_rows, d), F32)
    else:
        extra_specs = [pl.BlockSpec((None, 8, 3 * d), lambda i: (layer + 1, 0, 0))]
        extra_args = [mods]
        out_specs = [rows(d), rows(d)]
        out_shape = [jax.ShapeDtypeStruct((n_rows, d), F32), jax.ShapeDtypeStruct((n_rows, d), BF16)]
    return pl.pallas_call(
        kern,
        grid=(n_rows // tm,),
        in_specs=[
            pl.BlockSpec((4, KEY_W), const2),
            rows(d),
            pl.BlockSpec((None, 8, 3 * d), lambda i: (layer, 0, 0)),
            rows(3 * d, Z_GATES), rows(bw, Z_RG), rows(bw, Z_DG), rows(bw, Z_GG),
            rows(bw),
            pl.BlockSpec((2, tm, bw), lambda i: (0, i, 0)),
            rows(bw),
            pl.BlockSpec((1, LANES), const2),
            pl.BlockSpec((None, 3, bw, d), lambda i: (layer, 0, 0, 0),
                         pipeline_mode=pl.Buffered(1)),
            pl.BlockSpec((None, d, d), lambda i: (layer, 0, 0), pipeline_mode=pl.Buffered(1)),
            pl.BlockSpec((1, d), const2),
        ] + extra_specs,
        out_specs=out_specs,
        out_shape=out_shape,
        compiler_params=_params("arbitrary"),
    )(lam_params, xx, mods, z, z, z, z, ro, do, go, subln.reshape(1, LANES), wb, wo,
      next_gain.reshape(1, d), *extra_args)


def kernel(x, c, ctx, c_ctx, norm_gain, w_ada, b_ada, w_in, ret_log_rate, diff_lambda,
           diff_subln_gain, gqa_q_gain, gqa_k_gain, w_branch, w_out, final_norm_gain):
    _, n_lat, d = x.shape
    n_ctx = ctx.shape[1]
    depth = w_in.shape[0]
    assert x.shape[0] == 1 and d == 2048 and w_in.shape[2] == Z_COLS
    assert n_lat % n_ctx == 0 and n_ctx % CHUNK == 0 and n_lat % GRID_W == 0

    xx = jnp.concatenate([x[0], ctx[0]], axis=0)
    c8 = jnp.concatenate([c, c_ctx[None], jnp.zeros((6, d), F32)], axis=0)
    mods = _ada_all(c8, w_ada, b_ada)
    tabs = _rope_tables(n_lat, n_ctx)
    wb16 = w_branch.astype(BF16)
    wo16 = w_out.astype(BF16)

    h = _norm_mod(xx, mods, norm_gain[0], 0, n_lat)
    for l in range(depth):
        last = l == depth - 1
        lambda_init = 0.8 - 0.6 * math.exp(-0.3 * l)
        z = _in_proj(h, w_in, l)
        rq, rk, dq, dk, dv, gq, gk, gv = _prep(z, tabs, gqa_q_gain[l], gqa_k_gain[l])
        ro = _retention(rq, rk, z, ret_log_rate[l], n_lat)
        do = _diff_attention(dq, dk, dv, n_lat, not last)
        go = _gqa_attention(gq, gk, gv, n_lat, not last)
        next_gain = final_norm_gain if last else norm_gain[l + 1]
        out = _finish(xx, z, mods, diff_lambda[l], ro, do, go, diff_subln_gain[l], wb16, wo16,
                      next_gain, l, n_lat, lambda_init, last)
        if last:
            return out[None]
        xx, h = out
```

```python
import functools
import math

import jax
import jax.numpy as jnp
from jax import lax
from jax.experimental import pallas as pl
from jax.experimental.pallas import tpu as pltpu

F32 = jnp.float32
BF16 = jnp.bfloat16

EPS = 1e-6
ROPE_THETA = 10000.0
GRID_W = 64
LANES = 128
VMEM_LIMIT = 56 * 1024 * 1024

HEADS = 8
HEAD_W = 128
KEY_W = 64
GQA_KV = 2
GQA_GROUP = 4
CHUNK = 128

Z_GATES, Z_RQ, Z_RK, Z_RV, Z_RG = 0, 6144, 6656, 7168, 8192
Z_DQ, Z_DK, Z_DV, Z_DG = 9216, 10240, 11264, 12288
Z_GQ, Z_GG, Z_GK, Z_GV = 13312, 14336, 15360, 15616
Z_COLS = 15872
W_TILE = 512
NORM_ROWS = 128
LOG2_E = math.log2(math.e)


def _pick(n, candidates):
    for c in candidates:
        if n % c == 0:
            return c
    raise ValueError(f"no tile in {candidates} divides {n}")


def _params(*sem):
    return pltpu.CompilerParams(dimension_semantics=sem, vmem_limit_bytes=VMEM_LIMIT)


def _sigmoid(x):
    return 0.5 * jnp.tanh(0.5 * x) + 0.5


def _silu(x):
    return x * _sigmoid(x)


def _ada_kernel(c_ref, w_ref, b_ref, o_ref):
    s = _silu(c_ref[...])
    o_ref[...] = jnp.dot(s, w_ref[...], preferred_element_type=F32,
                         precision=lax.Precision.HIGHEST) + b_ref[...]


def _ada_all(c8, w_ada, b_ada):
    depth, d, d3 = w_ada.shape
    tn = 1024
    return pl.pallas_call(
        _ada_kernel,
        grid=(depth, d3 // tn),
        in_specs=[
            pl.BlockSpec((8, d), lambda l, j: (0, 0)),
            pl.BlockSpec((None, d, tn), lambda l, j: (l, 0, j)),
            pl.BlockSpec((None, 1, tn), lambda l, j: (l, 0, j)),
        ],
        out_specs=pl.BlockSpec((None, 8, tn), lambda l, j: (l, 0, j)),
        out_shape=jax.ShapeDtypeStruct((depth, 8, d3), F32),
        compiler_params=_params("arbitrary", "arbitrary"),
    )(c8, w_ada, b_ada.reshape(depth, 1, d3))


def _row_mod(mod_ref, lo, hi, is_ctx):
    return jnp.where(is_ctx, mod_ref[1:2, lo:hi], mod_ref[0:1, lo:hi])


def _modulated_norm(x, gain, mod_ref, first_row, n_lat, d):
    y = x * lax.rsqrt(jnp.mean(x * x, axis=-1, keepdims=True) + EPS) * gain
    rows = first_row + lax.broadcasted_iota(jnp.int32, (x.shape[0], 1), 0)
    is_ctx = rows >= n_lat
    shift = _row_mod(mod_ref, 0, d, is_ctx)
    scale = _row_mod(mod_ref, d, 2 * d, is_ctx)
    return (y * (1.0 + scale) + shift).astype(BF16)


def _norm_kernel(x_ref, mod_ref, g_ref, h_ref, *, n_lat, tm, d):
    h_ref[...] = _modulated_norm(x_ref[...], g_ref[...], mod_ref, pl.program_id(0) * tm, n_lat, d)


def _norm_mod(xx, mods, gain, layer, n_lat):
    n, d = xx.shape
    tm = NORM_ROWS
    return pl.pallas_call(
        functools.partial(_norm_kernel, n_lat=n_lat, tm=tm, d=d),
        grid=(n // tm,),
        in_specs=[
            pl.BlockSpec((tm, d), lambda i: (i, 0)),
            pl.BlockSpec((None, 8, 3 * d), lambda i: (layer, 0, 0)),
            pl.BlockSpec((1, d), lambda i: (0, 0)),
        ],
        out_specs=pl.BlockSpec((tm, d), lambda i: (i, 0)),
        out_shape=jax.ShapeDtypeStruct((n, d), BF16),
        compiler_params=_params("arbitrary"),
    )(xx, mods, gain.reshape(1, d))


def _inproj_kernel(h_ref, w_ref, z_ref):
    z_ref[...] = jnp.dot(h_ref[...], w_ref[...].astype(BF16),
                         preferred_element_type=F32).astype(BF16)


def _w_block(j):
    return jnp.where(j < 12, j + 19, jnp.where(j < 28, j - 12, jnp.where(j < 30, j - 11, 16)))


def _in_proj(h, w, layer):
    n, d = h.shape
    tm = _pick(n, (3328, 1280, 640, 256))
    return pl.pallas_call(
        _inproj_kernel,
        grid=(n // tm, Z_COLS // W_TILE),
        in_specs=[
            pl.BlockSpec((tm, d), lambda i, j: (i, 0)),
            pl.BlockSpec((None, d, W_TILE), lambda i, j: (layer, 0, _w_block(j))),
        ],
        out_specs=pl.BlockSpec((tm, W_TILE), lambda i, j: (i, j)),
        out_shape=jax.ShapeDtypeStruct((n, Z_COLS), BF16),
        compiler_params=_params("arbitrary", "arbitrary"),
    )(h, w)


def _swap_halves(x, half, in_first_half):
    if 2 * half == LANES:
        return pltpu.roll(x, half, 1)
    return jnp.where(in_first_half, pltpu.roll(x, LANES - half, 1), pltpu.roll(x, half, 1))


def _rope(x, c, s, half, in_first_half=None):
    return x * c + _swap_halves(x, half, in_first_half) * s


def _prep_kernel(rq_ref, rk_ref, dq_ref, dk_ref, dv_ref, gq_ref, gk_ref, gv_ref,
                 sc_ref, ss_ref, ac_ref, as_ref, bc_ref, bs_ref, qg_ref, kg_ref,
                 orq_ref, ork_ref, odq_ref, odk_ref, odv_ref, ogq_ref, ogk_ref, ogv_ref):
    tp = rq_ref.shape[0]
    lane = lax.broadcasted_iota(jnp.int32, (tp, LANES), 1)
    low = lane < KEY_W
    pair_low = lane % KEY_W < KEY_W // 2
    sc, ss = sc_ref[...], ss_ref[...]
    ac, as_ = ac_ref[...], as_ref[...]
    bc, bs = bc_ref[...], bs_ref[...]
    ones = jnp.ones((tp, LANES), BF16)

    k_scale = KEY_W ** -0.5
    for p in range(HEADS // 2):
        sl = slice(p * LANES, (p + 1) * LANES)
        for src, dst, mul in ((rq_ref, orq_ref, 1.0), (rk_ref, ork_ref, k_scale)):
            y = _rope(src[:, sl].astype(F32), sc, ss, KEY_W // 2, pair_low) * mul
            dst[:, (2 * p) * LANES:(2 * p + 1) * LANES] = jnp.where(low, y, 0.0).astype(BF16)
            dst[:, (2 * p + 1) * LANES:(2 * p + 2) * LANES] = jnp.where(
                low, pltpu.roll(y, KEY_W, 1), 0.0).astype(BF16)

    d_scale = KEY_W ** -0.5 * LOG2_E
    for h in range(HEADS):
        sl = slice(h * LANES, (h + 1) * LANES)
        q = _rope(dq_ref[:, sl].astype(F32), ac, as_, KEY_W // 2, pair_low) * d_scale
        odq_ref[0, :, sl] = jnp.where(low, q, 0.0).astype(BF16)
        odq_ref[1, :, sl] = jnp.where(low, 0.0, q).astype(BF16)
        odk_ref[:, sl] = _rope(dk_ref[:, sl].astype(F32), ac, as_, KEY_W // 2,
                               pair_low).astype(BF16)
        odv_ref[:, (2 * h) * LANES:(2 * h + 1) * LANES] = dv_ref[:, sl]
        odv_ref[:, (2 * h + 1) * LANES:(2 * h + 2) * LANES] = ones

    g_scale = HEAD_W ** -0.5 * LOG2_E

    def normed(x, gain):
        return x * lax.rsqrt(jnp.mean(x * x, axis=-1, keepdims=True) + EPS) * gain

    for h in range(HEADS):
        sl = slice(h * LANES, (h + 1) * LANES)
        q = _rope(normed(gq_ref[:, sl].astype(F32), qg_ref[...]), bc, bs, HEAD_W // 2)
        ogq_ref[:, sl] = (q * g_scale).astype(BF16)
    for h in range(GQA_KV):
        sl = slice(h * LANES, (h + 1) * LANES)
        k = _rope(normed(gk_ref[:, sl].astype(F32), kg_ref[...]), bc, bs, HEAD_W // 2)
        ogk_ref[:, sl] = k.astype(BF16)
        ogv_ref[:, (2 * h) * LANES:(2 * h + 1) * LANES] = gv_ref[:, sl]
        ogv_ref[:, (2 * h + 1) * LANES:(2 * h + 2) * LANES] = ones


def _prep(z, tabs, q_gain, k_gain):
    n = z.shape[0]
    tp = _pick(n, (640, 256))

    def zspec(off, width):
        return pl.BlockSpec((tp, width), lambda i, b=off // width: (i, b))

    tab = pl.BlockSpec((tp, LANES), lambda i: (i, 0))
    vec = pl.BlockSpec((1, LANES), lambda i: (0, 0))

    def ospec(width):
        return pl.BlockSpec((tp, width), lambda i: (i, 0))

    return pl.pallas_call(
        _prep_kernel,
        grid=(n // tp,),
        in_specs=[zspec(Z_RQ, 512), zspec(Z_RK, 512), zspec(Z_DQ, 1024), zspec(Z_DK, 1024),
                  zspec(Z_DV, 1024), zspec(Z_GQ, 1024), zspec(Z_GK, 256), zspec(Z_GV, 256),
                  tab, tab, tab, tab, tab, tab, vec, vec],
        out_specs=[ospec(1024), ospec(1024),
                   pl.BlockSpec((2, tp, 1024), lambda i: (0, i, 0)),
                   ospec(1024), ospec(2048), ospec(1024), ospec(256), ospec(512)],
        out_shape=[jax.ShapeDtypeStruct((n, 1024), BF16), jax.ShapeDtypeStruct((n, 1024), BF16),
                   jax.ShapeDtypeStruct((2, n, 1024), BF16), jax.ShapeDtypeStruct((n, 1024), BF16),
                   jax.ShapeDtypeStruct((n, 2048), BF16), jax.ShapeDtypeStruct((n, 1024), BF16),
                   jax.ShapeDtypeStruct((n, 256), BF16), jax.ShapeDtypeStruct((n, 512), BF16)],
        compiler_params=_params("arbitrary"),
    )(z, z, z, z, z, z, z, z, *tabs, q_gain.reshape(1, LANES), k_gain.reshape(1, LANES))


def _rope_tables(n_lat, n_ctx):
    def pattern(cos, sin):
        reps = LANES // (2 * cos.shape[1])
        c = jnp.tile(jnp.concatenate([cos, cos], axis=1), (1, reps))
        s = jnp.tile(jnp.concatenate([-sin, sin], axis=1), (1, reps))
        c = jnp.concatenate([c, jnp.ones((n_ctx, LANES), F32)], axis=0)
        s = jnp.concatenate([s, jnp.zeros((n_ctx, LANES), F32)], axis=0)
        return c, s

    def axial(head_dim):
        n_rows = n_lat // GRID_W
        rows = jnp.repeat(jnp.arange(n_rows), GRID_W).astype(F32)
        cols = jnp.tile(jnp.arange(GRID_W), n_rows).astype(F32)
        n_freq = head_dim // 4
        freqs = ROPE_THETA ** (-jnp.arange(n_freq, dtype=F32) / n_freq)
        ang = jnp.concatenate([rows[:, None] * freqs, cols[:, None] * freqs], axis=-1)
        return jnp.cos(ang), jnp.sin(ang)

    freqs = 1.0 / (ROPE_THETA ** jnp.linspace(0.0, 1.0, KEY_W // 2, dtype=F32))
    ang = jnp.arange(n_lat, dtype=F32)[:, None] * freqs
    return (*pattern(jnp.cos(ang), jnp.sin(ang)), *pattern(*axial(KEY_W)), *pattern(*axial(HEAD_W)))


def _log_decay(lr_ref, direction, h, shape):
    return -jnp.exp(jnp.full(shape, lr_ref[direction, h], F32))


def _ret_sum_kernel(lr_ref, k_ref, v_ref, kv_ref):
    h = pl.program_id(0)
    j = lax.broadcasted_iota(jnp.int32, (CHUNK, LANES), 0).astype(F32)
    w_f = jnp.exp(_log_decay(lr_ref, 0, h, (CHUNK, LANES)) * (CHUNK - 1 - j))
    w_b = jnp.exp(_log_decay(lr_ref, 1, h, (CHUNK, LANES)) * j)
    for c in range(k_ref.shape[0] // CHUNK):
        rows = slice(c * CHUNK, (c + 1) * CHUNK)
        k = k_ref[rows, :].astype(F32)
        kk = (k * w_f + pltpu.roll(k * w_b, KEY_W, 1)).T.astype(BF16)
        kv_ref[c] = jnp.dot(kk, v_ref[rows, :], preferred_element_type=F32)


def _ret_scan_kernel(lr_ref, kv_ref, st_ref, *, n_lat_chunks):
    h = pl.program_id(0)
    nc = kv_ref.shape[0]
    shape = (KEY_W, LANES)
    g_f = jnp.exp(_log_decay(lr_ref, 0, h, shape) * CHUNK)
    g_b = jnp.exp(_log_decay(lr_ref, 1, h, shape) * CHUNK)

    def fwd(c, s):
        st_ref[c, 0:KEY_W, :] = s.astype(BF16)
        return g_f * s + kv_ref[c, 0:KEY_W, :]

    def bwd(t, s):
        c = nc - 1 - t
        st_ref[c, KEY_W:, :] = s.astype(BF16)
        return g_b * s + kv_ref[c, KEY_W:, :]

    zero = jnp.zeros(shape, F32)
    s = lax.fori_loop(n_lat_chunks, nc, fwd, zero)
    lax.fori_loop(0, n_lat_chunks, fwd, s)
    lax.fori_loop(0, nc, bwd, zero)


def _ret_out_kernel(lr_ref, q_ref, k_ref, v_ref, st_ref, o_ref):
    h = pl.program_id(0)
    i = lax.broadcasted_iota(jnp.int32, (CHUNK, CHUNK), 0)
    j = lax.broadcasted_iota(jnp.int32, (CHUNK, CHUNK), 1)
    rel = (i - j).astype(F32)
    lg_f = _log_decay(lr_ref, 0, h, (CHUNK, CHUNK))
    lg_b = _log_decay(lr_ref, 1, h, (CHUNK, CHUNK))
    decay = jnp.where(i >= j, jnp.exp(lg_f * jnp.maximum(rel, 0.0)),
                      jnp.exp(lg_b * jnp.maximum(-rel, 0.0)))
    pos = i.astype(F32)
    cross_f = jnp.exp(lg_f * (pos + 1.0))
    cross_b = jnp.exp(lg_b * (CHUNK - pos))
    for c in range(q_ref.shape[0] // CHUNK):
        rows = slice(c * CHUNK, (c + 1) * CHUNK)
        q = q_ref[rows, :]
        att = lax.dot_general(q, k_ref[rows, :], (((1,), (1,)), ((), ())),
                              preferred_element_type=F32) * decay
        qf = q.astype(F32)
        qs = (qf * cross_f + pltpu.roll(qf * cross_b, KEY_W, 1)).astype(BF16)
        lhs = jnp.concatenate([att.astype(BF16), qs], axis=1)
        rhs = jnp.concatenate([v_ref[rows, :], st_ref[c]], axis=0)
        o_ref[rows, :] = jnp.dot(lhs, rhs, preferred_element_type=F32)


def _retention(rq, rk, z, log_rate, n_lat):
    n = z.shape[0]
    nc = n // CHUNK
    tr = _pick(n, (3328, 1280, 640, 256))
    cpt = tr // CHUNK
    smem = pl.BlockSpec(memory_space=pltpu.SMEM)
    head_rows = pl.BlockSpec((tr, LANES), lambda h, i: (i, h))
    v_rows = pl.BlockSpec((tr, LANES), lambda h, i: (i, Z_RV // LANES + h))
    chunk_mats = pl.BlockSpec((None, cpt, CHUNK, LANES), lambda h, i: (h, i, 0, 0))

    kv = pl.pallas_call(
        _ret_sum_kernel,
        grid=(HEADS, n // tr),
        in_specs=[smem, head_rows, v_rows],
        out_specs=chunk_mats,
        out_shape=jax.ShapeDtypeStruct((HEADS, nc, CHUNK, LANES), F32),
        compiler_params=_params("arbitrary", "arbitrary"),
    )(log_rate, rk, z)

    all_chunks = pl.BlockSpec((None, nc, CHUNK, LANES), lambda h: (h, 0, 0, 0))
    st = pl.pallas_call(
        functools.partial(_ret_scan_kernel, n_lat_chunks=n_lat // CHUNK),
        grid=(HEADS,),
        in_specs=[smem, all_chunks],
        out_specs=all_chunks,
        out_shape=jax.ShapeDtypeStruct((HEADS, nc, CHUNK, LANES), BF16),
        compiler_params=_params("arbitrary"),
    )(log_rate, kv)

    return pl.pallas_call(
        _ret_out_kernel,
        grid=(HEADS, n // tr),
        in_specs=[smem, head_rows, head_rows, v_rows, chunk_mats],
        out_specs=head_rows,
        out_shape=jax.ShapeDtypeStruct((n, HEADS * LANES), F32),
        compiler_params=_params("arbitrary", "arbitrary"),
    )(log_rate, rq, rk, z, st)


def _flash_kernel(q_ref, k_ref, v_ref, *rest, tq, tk, unroll):
    o_ref, s0_ref, s1_ref, s2_ref, m_ref, acc_ref = rest[-6:]
    bufs = (s0_ref, s1_ref, s2_ref)
    nk = k_ref.shape[0] // tk
    n_steps = (q_ref.shape[0] // tq) * nk

    def q_rows(qt):
        return pl.ds(pl.multiple_of(qt * tq, tq), tq)

    def key_rows(c):
        return pl.ds(pl.multiple_of(c * tk, tk), tk)

    def scores(qt, c, dst):
        dst[...] = lax.dot_general(q_ref[q_rows(qt), :], k_ref[key_rows(c), :],
                                   (((1,), (1,)), ((), ())), preferred_element_type=F32)

    def consume(qt, c, src):
        s = src[...]
        m = jnp.where(c == 0, -jnp.inf, m_ref[...])
        m_new = jnp.maximum(m, jnp.max(s, axis=-1, keepdims=True))
        p = jnp.exp2(s - m_new)
        acc = jnp.exp2(m - m_new) * acc_ref[...] + jnp.dot(
            p.astype(BF16), v_ref[key_rows(c), :], preferred_element_type=F32)
        acc_ref[...] = acc
        m_ref[...] = m_new
        o_ref[q_rows(qt), :] = (acc[:, :LANES] / acc[:, LANES:]).astype(o_ref.dtype)

    def step(slot, qt, c):
        wrap = c == nk - 1
        qt_next = jnp.where(wrap, qt + 1, qt)
        c_next = jnp.where(wrap, 0, c + 1)
        scores(qt_next, c_next, bufs[(slot + 1) % 3])
        consume(qt, c, bufs[slot])
        return qt_next, c_next

    def group(_, carry):
        qt, c = carry
        for u in range(unroll):
            qt, c = step(u % 3, qt, c)
        return qt, c

    m_ref[...] = jnp.full(m_ref.shape, -jnp.inf, F32)
    acc_ref[...] = jnp.zeros(acc_ref.shape, F32)
    zero = jnp.int32(0)
    scores(zero, zero, s0_ref)
    qt, c = lax.fori_loop(0, (n_steps - 1) // unroll, group, (zero, zero))
    for u in range((n_steps - 1) % unroll):
        qt, c = step(u % 3, qt, c)
    consume(qt, c, bufs[(n_steps - 1) % unroll % 3])


FLASH_UNROLL = 6


def _flash(q, k, v, out_shape, q_map, kv_map, o_map, grid, tq, q_tiles, n_keys, key_block,
           prev=None):
    tk = _pick(n_keys, (1280, 640, 256))
    q_block = (None,) * (q.ndim - 2) + (tq * q_tiles, LANES)
    o_block = (None,) * (len(out_shape.shape) - 2) + (tq * q_tiles, LANES)
    in_specs = [
        pl.BlockSpec(q_block, q_map),
        pl.BlockSpec((n_keys, LANES), lambda g, r: (key_block, kv_map(g)),
                     pipeline_mode=pl.Buffered(1)),
        pl.BlockSpec((n_keys, 2 * LANES), lambda g, r: (key_block, kv_map(g)),
                     pipeline_mode=pl.Buffered(1)),
    ]
    args = [q, k, v]
    aliases = {}
    if prev is not None:
        in_specs.append(pl.BlockSpec(memory_space=pl.ANY))
        args.append(prev)
        aliases = {3: 0}
    return pl.pallas_call(
        functools.partial(_flash_kernel, tq=tq, tk=tk, unroll=FLASH_UNROLL),
        grid=grid,
        in_specs=in_specs,
        out_specs=pl.BlockSpec(o_block, o_map),
        out_shape=out_shape,
        scratch_shapes=[pltpu.VMEM((tq, tk), F32), pltpu.VMEM((tq, tk), F32),
                        pltpu.VMEM((tq, tk), F32),
                        pltpu.VMEM((tq, 1), F32), pltpu.VMEM((tq, 2 * LANES), F32)],
        input_output_aliases=aliases,
        compiler_params=_params("arbitrary", "arbitrary"),
    )(*args)


def _query_tiling(n_lat):
    tq = _pick(n_lat, (1024, 512, 256))
    q_tiles = _pick(n_lat // tq, (8, 4, 2, 1))
    return tq, q_tiles, n_lat // (tq * q_tiles)


def _diff_attention(dq, dk, dv, n_lat, with_ctx):
    n = dk.shape[0]
    n_ctx = n - n_lat
    tq, q_tiles, nb = _query_tiling(n_lat)
    shape = jax.ShapeDtypeStruct((2, n, HEADS * LANES), F32)
    o = _flash(dq, dk, dv, shape,
               lambda g, r: (r // nb, r % nb, g), lambda g: g, lambda g, r: (r // nb, r % nb, g),
               (HEADS, 2 * nb), tq, q_tiles, n, 0)
    if with_ctx:
        cb = n_lat // n_ctx
        o = _flash(dq, dk, dv, shape,
                   lambda g, r: (r, cb, g), lambda g: g, lambda g, r: (r, cb, g),
                   (HEADS, 2), n_ctx, 1, n_ctx, cb, prev=o)
    return o


def _gqa_attention(gq, gk, gv, n_lat, with_ctx):
    n = gk.shape[0]
    n_ctx = n - n_lat
    tq, q_tiles, nb = _query_tiling(n_lat)
    shape = jax.ShapeDtypeStruct((n, HEADS * LANES), BF16)
    o = _flash(gq, gk, gv, shape,
               lambda g, r: (r % nb, g * GQA_GROUP + r // nb), lambda g: g,
               lambda g, r: (r % nb, g * GQA_GROUP + r // nb),
               (GQA_KV, GQA_GROUP * nb), tq, q_tiles, n, 0)
    if with_ctx:
        cb = n_lat // n_ctx
        o = _flash(gq, gk, gv, shape,
                   lambda g, r: (cb, g * GQA_GROUP + r), lambda g: g,
                   lambda g, r: (cb, g * GQA_GROUP + r),
                   (GQA_KV, GQA_GROUP), n_ctx, 1, n_ctx, cb, prev=o)
    return o


def _finish_kernel(lam_ref, x_ref, mod_ref, gates_ref, rg_ref, dg_ref, gg_ref, ro_ref, do_ref,
                   go_ref, sub_ref, wb_ref, wo_ref, ng_ref, *rest, n_lat, tm, d, lambda_init,
                   final_norm):
    lp = lam_ref[...]
    lam = (jnp.exp(jnp.sum(lp[0:1] * lp[1:2], axis=-1, keepdims=True))
           - jnp.exp(jnp.sum(lp[2:3] * lp[3:4], axis=-1, keepdims=True)) + lambda_init)

    def head_norm(o):
        return o * lax.rsqrt(jnp.mean(o * o, axis=-1, keepdims=True) + EPS)

    def retention_head(sl):
        return head_norm(ro_ref[:, sl]) * _silu(rg_ref[:, sl].astype(F32))

    def diff_head(sl):
        o = head_norm(do_ref[0, :, sl] - lam * do_ref[1, :, sl]) * sub_ref[...]
        return o * (1.0 - lambda_init) * _silu(dg_ref[:, sl].astype(F32))

    def gqa_head(sl):
        return go_ref[:, sl].astype(F32) * _silu(gg_ref[:, sl].astype(F32))

    branches = [jnp.concatenate([head(slice(h * LANES, (h + 1) * LANES)).astype(BF16)
                                 for h in range(HEADS)], axis=1)
                for head in (retention_head, diff_head, gqa_head)]
    half = d // 2
    merged = []
    for lo in (0, half):
        part = jnp.zeros((tm, half), F32)
        for b, br in enumerate(branches):
            y = jnp.dot(br, wb_ref[b, :, lo:lo + half], preferred_element_type=F32)
            gate_cols = gates_ref[:, b * d + lo:b * d + lo + half].astype(F32)
            part = part + _sigmoid(gate_cols) * y
        merged.append(part.astype(BF16))
    out = jnp.dot(jnp.concatenate(merged, axis=1), wo_ref[...], preferred_element_type=F32)

    first_row = pl.program_id(0) * tm
    rows = first_row + lax.broadcasted_iota(jnp.int32, (tm, 1), 0)
    gate = _row_mod(mod_ref, 2 * d, 3 * d, rows >= n_lat)
    x = x_ref[...] + gate * out
    if final_norm:
        (o_ref,) = rest
        o_ref[...] = x * lax.rsqrt(jnp.mean(x * x, axis=-1, keepdims=True) + EPS) * ng_ref[...]
    else:
        next_mod_ref, o_ref, h_ref = rest
        o_ref[...] = x
        h_ref[...] = _modulated_norm(x, ng_ref[...], next_mod_ref, first_row, n_lat, d)


def _finish(xx, z, mods, lam_params, ro, do, go, subln, wb, wo, next_gain, layer, n_lat,
            lambda_init, last):
    n, d = xx.shape
    tm = 256
    n_rows = n_lat if last else n
    bw = HEADS * LANES

    def rows(width, off=0):
        return pl.BlockSpec((tm, width), lambda i, b=off // width: (i, b))

    const2 = lambda i: (0, 0)
    kern = functools.partial(_finish_kernel, n_lat=n_lat, tm=tm, d=d, lambda_init=lambda_init,
                             final_norm=last)
    if last:
        extra_specs, extra_args = [], []
        out_specs = rows(d)
        out_shape = jax.ShapeDtypeStruct((n_rows, d), F32)
    else:
        extra_specs = [pl.BlockSpec((None, 8, 3 * d), lambda i: (layer + 1, 0, 0))]
        extra_args = [mods]
        out_specs = [rows(d), rows(d)]
        out_shape = [jax.ShapeDtypeStruct((n_rows, d), F32), jax.ShapeDtypeStruct((n_rows, d), BF16)]
    return pl.pallas_call(
        kern,
        grid=(n_rows // tm,),
        in_specs=[
            pl.BlockSpec((4, KEY_W), const2),
            rows(d),
            pl.BlockSpec((None, 8, 3 * d), lambda i: (layer, 0, 0)),
            rows(3 * d, Z_GATES), rows(bw, Z_RG), rows(bw, Z_DG), rows(bw, Z_GG),
            rows(bw),
            pl.BlockSpec((2, tm, bw), lambda i: (0, i, 0)),
            rows(bw),
            pl.BlockSpec((1, LANES), const2),
            pl.BlockSpec((None, 3, bw, d), lambda i: (layer, 0, 0, 0),
                         pipeline_mode=pl.Buffered(1)),
            pl.BlockSpec((None, d, d), lambda i: (layer, 0, 0), pipeline_mode=pl.Buffered(1)),
            pl.BlockSpec((1, d), const2),
        ] + extra_specs,
        out_specs=out_specs,
        out_shape=out_shape,
        compiler_params=_params("arbitrary"),
    )(lam_params, xx, mods, z, z, z, z, ro, do, go, subln.reshape(1, LANES), wb, wo,
      next_gain.reshape(1, d), *extra_args)


def kernel(x, c, ctx, c_ctx, norm_gain, w_ada, b_ada, w_in, ret_log_rate, diff_lambda,
           diff_subln_gain, gqa_q_gain, gqa_k_gain, w_branch, w_out, final_norm_gain):
    _, n_lat, d = x.shape
    n_ctx = ctx.shape[1]
    depth = w_in.shape[0]
    assert x.shape[0] == 1 and d == 2048 and w_in.shape[2] == Z_COLS
    assert n_lat % n_ctx == 0 and n_ctx % CHUNK == 0 and n_lat % GRID_W == 0

    xx = jnp.concatenate([x[0], ctx[0]], axis=0)
    c8 = jnp.concatenate([c, c_ctx[None], jnp.zeros((6, d), F32)], axis=0)
    mods = _ada_all(c8, w_ada, b_ada)
    tabs = _rope_tables(n_lat, n_ctx)
    wb16 = w_branch.astype(BF16)
    wo16 = w_out.astype(BF16)

    h = _norm_mod(xx, mods, norm_gain[0], 0, n_lat)
    for l in range(depth):
        last = l == depth - 1
        lambda_init = 0.8 - 0.6 * math.exp(-0.3 * l)
        z = _in_proj(h, w_in, l)
        rq, rk, dq, dk, dv, gq, gk, gv = _prep(z, tabs, gqa_q_gain[l], gqa_k_gain[l])
        ro = _retention(rq, rk, z, ret_log_rate[l], n_lat)
        do = _diff_attention(dq, dk, dv, n_lat, not last)
        go = _gqa_attention(gq, gk, gv, n_lat, not last)
        next_gain = final_norm_gain if last else norm_gain[l + 1]
        out = _finish(xx, z, mods, diff_lambda[l], ro, do, go, diff_subln_gain[l], wb16, wo16,
                      next_gain, l, n_lat, lambda_init, last)
        if last:
            return out[None]
        xx, h = out
```
